```python
import jax, jax.numpy as jnp
from jax import lax
import numpy as np

D_MODEL = 2048
BATCH = 4
SEQ = 2048
DEPTH = 1

CONV_CH = D_MODEL // 2
CONV_K = 3
RET_HEADS = 8
RET_DK = D_MODEL // (2 * RET_HEADS)
RET_DV = 2 * RET_DK
RET_QK = RET_HEADS * RET_DK
RET_V = RET_HEADS * RET_DV
CHUNK = 128
ROPE_BASE = 10000.0
N_GROUPS = 4
EXPERTS_PER_GROUP = 8
N_EXPERTS = N_GROUPS * EXPERTS_PER_GROUP
TOP_K = 2
D_EXPERT = D_MODEL // 2
EXPERT_BLOCK = 128
EPS = 1e-6
IN_COLS = 3 * CONV_CH + 2 * RET_QK + 2 * RET_V + 2 * D_MODEL

kernel_name = "hybrid_conv_retention_hmoe_adaln"


def rmsnorm(x, g):
    xf = x.astype(jnp.float32)
    xf = xf * lax.rsqrt(jnp.mean(xf * xf, axis=-1, keepdims=True) + EPS)
    return (xf * g.astype(jnp.float32)).astype(x.dtype)


def modulate(h, shift, scale):
    return h * (1 + scale[:, None, :]) + shift[:, None, :]


def causal_dwconv(z, w):
    S = z.shape[1]
    zp = jnp.pad(z, ((0, 0), (CONV_K - 1, 0), (0, 0)))
    out = zp[:, 0:S] * w[0]
    for tap in range(1, CONV_K):
        out = out + zp[:, tap:tap + S] * w[tap]
    return out


def rotary(t):
    S, d = t.shape[-2], t.shape[-1]
    inv = ROPE_BASE ** (-jnp.arange(0, d, 2, dtype=jnp.float32) / d)
    ang = jnp.arange(S, dtype=jnp.float32)[:, None] * inv[None, :]
    cos, sin = jnp.cos(ang).astype(t.dtype), jnp.sin(ang).astype(t.dtype)
    t1, t2 = t[..., : d // 2], t[..., d // 2:]
    return jnp.concatenate([t1 * cos - t2 * sin, t1 * sin + t2 * cos], axis=-1)


def retention_chunkwise(q, k, v):
    B, H, S, dk = q.shape
    dv = v.shape[-1]
    N = S // CHUNK
    dt = q.dtype
    log_g = jnp.log1p(-(2.0 ** (-5.0 - jnp.arange(H, dtype=jnp.float32))))
    i = jnp.arange(CHUNK, dtype=jnp.float32)
    diff = i[:, None] - i[None, :]
    inner_decay = jnp.where(diff >= 0, jnp.exp(log_g[:, None, None] * jnp.maximum(diff, 0.0)), 0.0).astype(dt)
    q_decay = jnp.exp(log_g[:, None] * (i + 1.0)).astype(dt)
    k_decay = jnp.exp(log_g[:, None] * (CHUNK - 1.0 - i)).astype(dt)
    chunk_decay = jnp.exp(log_g * CHUNK).astype(dt)[:, None, None]
    qc = q.reshape(B, H, N, CHUNK, dk)
    kc = k.reshape(B, H, N, CHUNK, dk)
    vc = v.reshape(B, H, N, CHUNK, dv)
    scores = jnp.einsum('bhnid,bhnjd->bhnij', qc, kc) * inner_decay[None, :, None]
    inner = jnp.einsum('bhnij,bhnje->bhnie', scores, vc)
    kv = jnp.einsum('bhnjd,hj,bhnje->nbhde', kc, k_decay, vc)

    def step(state, kv_n):
        return state * chunk_decay + kv_n, state

    _, s_prev = lax.scan(step, jnp.zeros((B, H, dk, dv), kv.dtype), kv)
    cross = jnp.einsum('bhnid,hi,nbhde->bhnie', qc, q_decay, s_prev)
    return (inner + cross).reshape(B, H, S, dv)


def hier_moe(h, w_router_group, b_router_group, w_router_expert, b_router_expert, w_gate, w_up, w_down):
    T, D = h.shape
    g_logits = (h @ w_router_group).astype(jnp.float32) + b_router_group.astype(jnp.float32)
    g_prob = jax.nn.softmax(g_logits, axis=-1)
    g_sel = jnp.argmax(g_logits, axis=-1)
    tok = jnp.arange(T)
    g_w = g_prob[tok, g_sel][:, None]
    e_logits = ((h @ w_router_expert).astype(jnp.float32) + b_router_expert.astype(jnp.float32)).reshape(T, N_GROUPS, EXPERTS_PER_GROUP)
    e_prob = jax.nn.softmax(e_logits[tok, g_sel], axis=-1)
    top_p, top_i = lax.top_k(e_prob, TOP_K)
    top_p = top_p / jnp.sum(top_p, axis=-1, keepdims=True)
    combine = g_w * top_p
    expert_id = g_sel[:, None] * EXPERTS_PER_GROUP + top_i

    A = T * TOP_K
    flat_e = expert_id.reshape(A)
    flat_tok = jnp.repeat(tok, TOP_K)
    flat_w = combine.reshape(A)
    order = jnp.argsort(flat_e)
    se, stok, sw = flat_e[order], flat_tok[order], flat_w[order]
    counts = jnp.bincount(flat_e, length=N_EXPERTS)
    start = jnp.cumsum(counts) - counts
    padded = (counts + EXPERT_BLOCK - 1) // EXPERT_BLOCK * EXPERT_BLOCK
    pend = jnp.cumsum(padded)
    pstart = pend - padded
    dest = pstart[se] + (jnp.arange(A) - start[se])
    R = A + N_EXPERTS * EXPERT_BLOCK
    n_blk = R // EXPERT_BLOCK
    x_buf = jnp.zeros((R, D), h.dtype).at[dest].set(h[stok])
    tok_buf = jnp.zeros((R,), jnp.int32).at[dest].set(stok.astype(jnp.int32))
    w_buf = jnp.zeros((R,), h.dtype).at[dest].set(sw.astype(h.dtype))
    blk_e = jnp.minimum(jnp.searchsorted(pend, jnp.arange(n_blk) * EXPERT_BLOCK, side='right'), N_EXPERTS - 1)

    def run_block(args):
        xb, e = args
        return (jax.nn.silu(xb @ w_gate[e]) * (xb @ w_up[e])) @ w_down[e]

    y_buf = lax.map(run_block, (x_buf.reshape(n_blk, EXPERT_BLOCK, D), blk_e)).reshape(R, D)
    return jnp.zeros((T, D), h.dtype).at[tok_buf].add(y_buf * w_buf[:, None])


def setup_inputs(seed: int = 0) -> dict:
    key = jax.random.key(seed)
    ks = jax.random.split(key, 20)
    D, L = D_MODEL, DEPTH
    nrm = lambda k, shape, fan_in, s=1.0: (jax.random.normal(k, shape, jnp.float32) * (s * fan_in ** -0.5))
    return {
        "x": jax.random.normal(ks[0], (BATCH, SEQ, D), jnp.float32),
        "c": jax.random.normal(ks[1], (BATCH, D), jnp.float32),
        "w_ada": nrm(ks[2], (L, D, 6 * D), D, 0.5),
        "b_ada": 0.01 * jax.random.normal(ks[3], (L, 6 * D), jnp.float32),
        "norm1_g": 1.0 + 0.05 * jax.random.normal(ks[4], (L, D), jnp.float32),
        "w_in": nrm(ks[5], (L, D, IN_COLS), D),
        "conv_w": nrm(ks[6], (L, CONV_K, CONV_CH), CONV_K),
        "w_conv_out": nrm(ks[7], (L, CONV_CH, D), CONV_CH),
        "w_ret_out": nrm(ks[8], (L, RET_V, D), RET_V),
        "w_o": nrm(ks[9], (L, D, D), D),
        "norm2_g": 1.0 + 0.05 * jax.random.normal(ks[10], (L, D), jnp.float32),
        "w_router_group": nrm(ks[11], (L, D, N_GROUPS), D),
        "b_router_group": 0.01 * jax.random.normal(ks[12], (L, N_GROUPS), jnp.float32),
        "w_router_expert": nrm(ks[13], (L, D, N_EXPERTS), D),
        "b_router_expert": 0.01 * jax.random.normal(ks[14], (L, N_EXPERTS), jnp.float32),
        "w_gate": nrm(ks[15], (L, N_EXPERTS, D, D_EXPERT), D),
        "w_up": nrm(ks[16], (L, N_EXPERTS, D, D_EXPERT), D),
        "w_down": nrm(ks[17], (L, N_EXPERTS, D_EXPERT, D), D_EXPERT),
        "norm_f_g": 1.0 + 0.05 * jax.random.normal(ks[18], (D,), jnp.float32),
    }


def reference(x, c, w_ada, b_ada, norm1_g, w_in, conv_w, w_conv_out, w_ret_out, w_o, norm2_g,
              w_router_group, b_router_group, w_router_expert, b_router_expert,
              w_gate, w_up, w_down, norm_f_g):
    B, S, D = x.shape
    sizes = (CONV_CH, CONV_CH, CONV_CH, RET_QK, RET_QK, RET_V, RET_V, D_MODEL, D_MODEL)
    split_at = tuple(int(s) for s in np.cumsum(sizes)[:-1])
    c_act = jax.nn.silu(c)
    for l in range(DEPTH):
        mod = c_act @ w_ada[l] + b_ada[l]
        sh1, sc1, g1, sh2, sc2, g2 = jnp.split(mod, 6, axis=-1)

        h = modulate(rmsnorm(x, norm1_g[l]), sh1, sc1)
        u = h @ w_in[l]
        cb, cc, cx, q, k, v, rg, gate_a, gate_b = jnp.split(u, split_at, axis=-1)

        y_a = (cb * causal_dwconv(cc * cx, conv_w[l])) @ w_conv_out[l]

        qh = rotary(q.reshape(B, S, RET_HEADS, RET_DK).transpose(0, 2, 1, 3))
        kh = rotary(k.reshape(B, S, RET_HEADS, RET_DK).transpose(0, 2, 1, 3)) * (RET_DK ** -0.5)
        vh = v.reshape(B, S, RET_HEADS, RET_DV).transpose(0, 2, 1, 3)
        ret = retention_chunkwise(qh, kh, vh)
        rf = ret.astype(jnp.float32)
        rf = rf * lax.rsqrt(jnp.mean(rf * rf, axis=-1, keepdims=True) + EPS)
        ret = rf.astype(x.dtype).transpose(0, 2, 1, 3).reshape(B, S, RET_V)
        y_b = (jax.nn.silu(rg) * ret) @ w_ret_out[l]

        merged = jax.nn.sigmoid(gate_a) * y_a + jax.nn.sigmoid(gate_b) * y_b
        x = x + g1[:, None, :] * (merged @ w_o[l])

        h2 = modulate(rmsnorm(x, norm2_g[l]), sh2, sc2).reshape(B * S, D)
        y = hier_moe(h2, w_router_group[l], b_router_group[l], w_router_expert[l], b_router_expert[l],
                     w_gate[l], w_up[l], w_down[l]).reshape(B, S, D)
        x = x + g2[:, None, :] * y
    return rmsnorm(x, norm_f_g)
```

```python
import functools

import numpy as np
import jax
import jax.numpy as jnp
from jax import lax
from jax.experimental import pallas as pl
from jax.experimental.pallas import tpu as pltpu

F32 = jnp.float32
BF16 = jnp.bfloat16
I32 = jnp.int32

D_MODEL = 2048
CONV_CH = 1024
CONV_K = 3
RET_HEADS = 8
RET_DK = 128
RET_DV = 256
CHUNK = 128
ROPE_BASE = 10000.0
N_GROUPS = 4
EXPERTS_PER_GROUP = 8
N_EXPERTS = 32
D_EXPERT = 1024
EXPERT_BLOCK = 128
EPS = 1e-6
IN_COLS = 13312
LANES = 128
COL_BLK = 1024
CB_BLK, CC_BLK, CX_BLK, Q_BLK, K_BLK, V_BLK, RG_BLK, GA_BLK, GB_BLK = 0, 1, 2, 3, 4, 5, 7, 9, 11

VMEM_LIMIT = 56 * 1024 * 1024


def _cparams(sem, vmem=VMEM_LIMIT):
    return pltpu.CompilerParams(dimension_semantics=sem, vmem_limit_bytes=vmem)


def _sigmoid(v):
    return 1.0 / (1.0 + jnp.exp(-v))


def _silu(v):
    return v * _sigmoid(v)


def _ada_kernel(c_ref, w_ref, b_ref, o_ref):
    c = c_ref[...]
    o_ref[...] = jnp.dot(_silu(c), w_ref[...], precision=lax.Precision.HIGHEST,
                         preferred_element_type=F32) + b_ref[...]


def _ada(c_pad, w, b):
    rows, d = c_pad.shape
    n = w.shape[1]
    tn = 512
    return pl.pallas_call(
        _ada_kernel,
        grid=(n // tn,),
        in_specs=[pl.BlockSpec((rows, d), lambda j: (0, 0)),
                  pl.BlockSpec((d, tn), lambda j: (0, j)),
                  pl.BlockSpec((1, tn), lambda j: (0, j))],
        out_specs=pl.BlockSpec((rows, tn), lambda j: (0, j)),
        out_shape=jax.ShapeDtypeStruct((rows, n), F32),
        compiler_params=_cparams(("arbitrary",)),
        name="ada",
    )(c_pad, w, b)


def _rms_mod(x, g, shift, scale):
    ms = jnp.mean(x * x, axis=-1, keepdims=True)
    xn = x * lax.rsqrt(ms + EPS) * g
    return xn * (1.0 + scale) + shift


def _in_kernel(x_ref, mod_ref, g_ref, w_ref, o_ref, h_ref):
    @pl.when(pl.program_id(1) == 0)
    def _():
        h = _rms_mod(x_ref[...], g_ref[...], mod_ref[0, 0:1, :], mod_ref[0, 1:2, :])
        h_ref[...] = h.astype(BF16)

    o_ref[...] = jnp.dot(h_ref[...], w_ref[...].astype(BF16),
                         preferred_element_type=F32).astype(o_ref.dtype)


def _in_proj(x2, mod3, g, w, seq):
    t, d = x2.shape
    n = w.shape[1]
    tm, tn = 1024, 512
    per_b = seq // tm
    return pl.pallas_call(
        _in_kernel,
        grid=(t // tm, n // tn),
        in_specs=[pl.BlockSpec((tm, d), lambda i, j: (i, 0)),
                  pl.BlockSpec((1, 6, d), lambda i, j: (i // per_b, 0, 0)),
                  pl.BlockSpec((1, d), lambda i, j: (0, 0)),
                  pl.BlockSpec((d, tn), lambda i, j: (0, j))],
        out_specs=pl.BlockSpec((tm, tn), lambda i, j: (i, j)),
        out_shape=jax.ShapeDtypeStruct((t, n), BF16),
        scratch_shapes=[pltpu.VMEM((tm, d), BF16)],
        compiler_params=_cparams(("arbitrary", "arbitrary")),
        name="in_proj",
    )(x2, mod3, g, w)


def _ret_kernel(q_ref, k_ref, v0_ref, v1_ref, rg0_ref, rg1_ref, cos_ref, sin_ref,
                idec_ref, qdec_ref, kdec_ref, o_ref, state_ref, *, chunk_decay):
    @pl.when(pl.program_id(1) == 0)
    def _():
        state_ref[...] = jnp.zeros_like(state_ref)

    cos = cos_ref[...]
    sin = sin_ref[...]

    def rot(t):
        return t * cos + pltpu.roll(t, RET_DK // 2, axis=1) * sin

    for h in range(RET_HEADS):
        v_ref, rg_ref = (v0_ref, rg0_ref) if h < RET_HEADS // 2 else (v1_ref, rg1_ref)
        vc = (h % (RET_HEADS // 2)) * RET_DV
        qh = rot(q_ref[:, h * RET_DK:(h + 1) * RET_DK].astype(F32))
        kh = rot(k_ref[:, h * RET_DK:(h + 1) * RET_DK].astype(F32)) * (RET_DK ** -0.5)
        vh = v_ref[:, vc:vc + RET_DV]
        qb = qh.astype(BF16)
        kb = kh.astype(BF16)
        scores = lax.dot_general(qb, kb, (((1,), (1,)), ((), ())),
                                 preferred_element_type=F32) * idec_ref[h]
        inner = jnp.dot(scores.astype(BF16), vh, preferred_element_type=F32)
        state = state_ref[h]
        cross = jnp.dot((qh * qdec_ref[h]).astype(BF16), state.astype(BF16),
                        preferred_element_type=F32)
        kv = lax.dot_general((kh * kdec_ref[h]).astype(BF16), vh, (((0,), (0,)), ((), ())),
                             preferred_element_type=F32)
        state_ref[h] = state * chunk_decay[h] + kv
        r = inner + cross
        r = r * lax.rsqrt(jnp.mean(r * r, axis=-1, keepdims=True) + EPS)
        gate = _silu(rg_ref[:, vc:vc + RET_DV].astype(F32))
        o_ref[:, h * RET_DV:(h + 1) * RET_DV] = (gate * r).astype(o_ref.dtype)


def _retention(u, batch, seq):
    t = u.shape[0]
    nch = seq // CHUNK
    inv = ROPE_BASE ** (-jnp.arange(0, RET_DK, 2, dtype=F32) / RET_DK)
    ang = jnp.arange(seq, dtype=F32)[:, None] * inv[None, :]
    cos, sin = jnp.cos(ang), jnp.sin(ang)
    cos_t = jnp.concatenate([cos, cos], axis=-1)
    sin_t = jnp.concatenate([-sin, sin], axis=-1)
    log_g = jnp.log1p(-(2.0 ** (-5.0 - jnp.arange(RET_HEADS, dtype=F32))))
    i = jnp.arange(CHUNK, dtype=F32)
    diff = i[:, None] - i[None, :]
    idec = jnp.where(diff >= 0, jnp.exp(log_g[:, None, None] * jnp.maximum(diff, 0.0)), 0.0)
    qdec = jnp.broadcast_to(jnp.exp(log_g[:, None] * (i + 1.0))[:, :, None],
                            (RET_HEADS, CHUNK, RET_DK))
    kdec = jnp.broadcast_to(jnp.exp(log_g[:, None] * (CHUNK - 1.0 - i))[:, :, None],
                            (RET_HEADS, CHUNK, RET_DK))
    log_g_np = np.log1p(-(2.0 ** (-5.0 - np.arange(RET_HEADS, dtype=np.float32)))).astype(np.float32)
    chunk_decay = tuple(float(np.exp(np.float32(lg * np.float32(CHUNK)))) for lg in log_g_np)

    def ublk(c):
        return pl.BlockSpec((CHUNK, COL_BLK), lambda b, n, c=c: (b * nch + n, c))

    def table():
        return pl.BlockSpec((RET_HEADS, CHUNK, RET_DK), lambda b, n: (0, 0, 0))

    return pl.pallas_call(
        functools.partial(_ret_kernel, chunk_decay=chunk_decay),
        grid=(batch, nch),
        in_specs=[ublk(Q_BLK), ublk(K_BLK), ublk(V_BLK), ublk(V_BLK + 1),
                  ublk(RG_BLK), ublk(RG_BLK + 1),
                  pl.BlockSpec((CHUNK, RET_DK), lambda b, n: (n, 0)),
                  pl.BlockSpec((CHUNK, RET_DK), lambda b, n: (n, 0)),
                  table(), table(), table()],
        out_specs=pl.BlockSpec((CHUNK, RET_HEADS * RET_DV), lambda b, n: (b * nch + n, 0)),
        out_shape=jax.ShapeDtypeStruct((t, RET_HEADS * RET_DV), BF16),
        scratch_shapes=[pltpu.VMEM((RET_HEADS, RET_DK, RET_DV), F32)],
        compiler_params=_cparams(("arbitrary", "arbitrary")),
        name="retention",
    )(u, u, u, u, u, u, cos_t, sin_t, idec, qdec, kdec)


HALO = 8


def _mix_kernel(cb_ref, cc_ref, cx_ref, hcc_ref, hcx_ref, z_ref, ga_ref, gb_ref, cw_ref,
                wc_ref, wr_ref, o_ref, p_ref, m_ref, *, tiles_per_seq):
    tm = cb_ref.shape[0]

    @pl.when(pl.program_id(1) == 0)
    def _():
        m = cc_ref[...].astype(F32) * cx_ref[...].astype(F32)
        halo = hcc_ref[...].astype(F32) * hcx_ref[...].astype(F32)
        first = (pl.program_id(0) % tiles_per_seq) == 0
        m_ref[0:HALO, :] = jnp.where(first, 0.0, halo)
        m_ref[HALO:HALO + tm, :] = m
        conv = (m_ref[HALO - 2:HALO - 2 + tm, :] * cw_ref[0:1, :]
                + m_ref[HALO - 1:HALO - 1 + tm, :] * cw_ref[1:2, :]
                + m * cw_ref[2:3, :])
        p_ref[...] = (cb_ref[...].astype(F32) * conv).astype(BF16)

    ya = jnp.dot(p_ref[...], wc_ref[...].astype(BF16), preferred_element_type=F32)
    yb = jnp.dot(z_ref[...], wr_ref[...].astype(BF16), preferred_element_type=F32)
    merged = _sigmoid(ga_ref[...].astype(F32)) * ya + _sigmoid(gb_ref[...].astype(F32)) * yb
    o_ref[...] = merged.astype(o_ref.dtype)


def _mix(u, z, conv_w, w_conv_out, w_ret_out, seq):
    t = u.shape[0]
    d = w_conv_out.shape[1]
    tm, tn = 512, 512
    hb = tm // HALO
    gblk = COL_BLK // tn

    def halo(c):
        return pl.BlockSpec((HALO, COL_BLK), lambda i, j, c=c: (jnp.maximum(i * hb - 1, 0), c))

    return pl.pallas_call(
        functools.partial(_mix_kernel, tiles_per_seq=seq // tm),
        grid=(t // tm, d // tn),
        in_specs=[pl.BlockSpec((tm, COL_BLK), lambda i, j: (i, CB_BLK)),
                  pl.BlockSpec((tm, COL_BLK), lambda i, j: (i, CC_BLK)),
                  pl.BlockSpec((tm, COL_BLK), lambda i, j: (i, CX_BLK)),
                  halo(CC_BLK), halo(CX_BLK),
                  pl.BlockSpec((tm, z.shape[1]), lambda i, j: (i, 0)),
                  pl.BlockSpec((tm, tn), lambda i, j: (i, GA_BLK * gblk + j)),
                  pl.BlockSpec((tm, tn), lambda i, j: (i, GB_BLK * gblk + j)),
                  pl.BlockSpec((CONV_K, CONV_CH), lambda i, j: (0, 0)),
                  pl.BlockSpec((CONV_CH, tn), lambda i, j: (0, j)),
                  pl.BlockSpec((z.shape[1], tn), lambda i, j: (0, j))],
        out_specs=pl.BlockSpec((tm, tn), lambda i, j: (i, j)),
        out_shape=jax.ShapeDtypeStruct((t, d), BF16),
        scratch_shapes=[pltpu.VMEM((tm, CONV_CH), BF16),
                        pltpu.VMEM((tm + HALO, CONV_CH), F32)],
        compiler_params=_cparams(("arbitrary", "arbitrary")),
        name="mix",
    )(u, u, u, u, u, z, u, u, conv_w, w_conv_out, w_ret_out)


def _out_kernel(m_ref, w_ref, x_ref, mod_ref, o_ref):
    acc = jnp.dot(m_ref[...], w_ref[...].astype(BF16), preferred_element_type=F32)
    o_ref[...] = x_ref[...] + mod_ref[0, 2:3, :] * acc


def _out_proj(merged, w_o, x2, mod3, seq):
    t, d = x2.shape
    tm, tn = 512, 512
    per_b = seq // tm
    return pl.pallas_call(
        _out_kernel,
        grid=(t // tm, d // tn),
        in_specs=[pl.BlockSpec((tm, d), lambda i, j: (i, 0)),
                  pl.BlockSpec((d, tn), lambda i, j: (0, j)),
                  pl.BlockSpec((tm, tn), lambda i, j: (i, j)),
                  pl.BlockSpec((1, 6, tn), lambda i, j: (i // per_b, 0, j))],
        out_specs=pl.BlockSpec((tm, tn), lambda i, j: (i, j)),
        out_shape=jax.ShapeDtypeStruct((t, d), F32),
        compiler_params=_cparams(("arbitrary", "arbitrary")),
        name="out_proj",
    )(merged, w_o, x2, mod3)


ROUTE_E0 = N_GROUPS


def _split_bf16(v):
    hi = v.astype(BF16)
    lo = (v - hi.astype(F32)).astype(BF16)
    return hi, lo


def _route_kernel(x_ref, mod_ref, g_ref, wr_ref, br_ref, h_ref, ef_ref, cw_ref):
    h2 = _rms_mod(x_ref[...], g_ref[...], mod_ref[0, 3:4, :], mod_ref[0, 4:5, :])
    h_ref[...] = h2
    hh, hl = _split_bf16(h2)
    wh, wl = _split_bf16(wr_ref[...])
    logits = (jnp.dot(hh, wh, preferred_element_type=F32)
              + (jnp.dot(hh, wl, preferred_element_type=F32)
                 + jnp.dot(hl, wh, preferred_element_type=F32))) + br_ref[...]
    lane = lax.broadcasted_iota(I32, logits.shape, 1).astype(F32)
    neg = -jnp.inf
    big = float(LANES)
    gl = jnp.where(lane < N_GROUPS, logits, neg)
    gmax = jnp.max(gl, axis=-1, keepdims=True)
    gsel = jnp.min(jnp.where(gl == gmax, lane, big), axis=-1, keepdims=True)
    gsum = jnp.sum(jnp.exp(gl - gmax), axis=-1, keepdims=True)
    g_w = 1.0 / gsum
    lo_lane = ROUTE_E0 + EXPERTS_PER_GROUP * gsel
    emask = (lane >= lo_lane) & (lane < lo_lane + EXPERTS_PER_GROUP)
    el = jnp.where(emask, logits, neg)
    m1 = jnp.max(el, axis=-1, keepdims=True)
    i1 = jnp.min(jnp.where(el == m1, lane, big), axis=-1, keepdims=True)
    el2 = jnp.where(lane == i1, neg, el)
    m2 = jnp.max(el2, axis=-1, keepdims=True)
    i2 = jnp.min(jnp.where(el2 == m2, lane, big), axis=-1, keepdims=True)
    esum = jnp.sum(jnp.exp(el - m1), axis=-1, keepdims=True)
    p1 = 1.0 / esum
    p2 = jnp.exp(m2 - m1) / esum
    tot = p1 + p2
    c1 = g_w * (p1 / tot)
    c2 = g_w * (p2 / tot)
    ef_ref[...] = jnp.where(lane == 0.0, i1 - ROUTE_E0, jnp.where(lane == 1.0, i2 - ROUTE_E0, 0.0))
    cw_ref[...] = jnp.where(lane == 0.0, c1, jnp.where(lane == 1.0, c2, 0.0))


def _route(x1, mod3, g, w_r, b_r, seq):
    t, d = x1.shape
    tm = 512
    per_b = seq // tm
    return pl.pallas_call(
        _route_kernel,
        grid=(t // tm,),
        in_specs=[pl.BlockSpec((tm, d), lambda i: (i, 0)),
                  pl.BlockSpec((1, 6, d), lambda i: (i // per_b, 0, 0)),
                  pl.BlockSpec((1, d), lambda i: (0, 0)),
                  pl.BlockSpec((d, LANES), lambda i: (0, 0)),
                  pl.BlockSpec((1, LANES), lambda i: (0, 0))],
        out_specs=[pl.BlockSpec((tm, d), lambda i: (i, 0)),
                   pl.BlockSpec((tm, LANES), lambda i: (i, 0)),
                   pl.BlockSpec((tm, LANES), lambda i: (i, 0))],
        out_shape=[jax.ShapeDtypeStruct((t, d), F32),
                   jax.ShapeDtypeStruct((t, LANES), F32),
                   jax.ShapeDtypeStruct((t, LANES), F32)],
        compiler_params=_cparams(("arbitrary",)),
        name="route",
    )(x1, mod3, g, w_r, b_r)


META_TB = 512
BLK_ROWS = 256


def _meta_kernel(ef_ref, dest_ref, blk_ref, pref_ref):
    t = ef_ref.shape[0]
    nb = t // META_TB
    lane = lax.broadcasted_iota(I32, (META_TB, LANES), 1).astype(F32)
    r_i = lax.broadcasted_iota(I32, (META_TB, META_TB), 0)
    c_i = lax.broadcasted_iota(I32, (META_TB, META_TB), 1)
    tril = jnp.where(r_i > c_i, 1.0, 0.0).astype(BF16)

    def onehots(i):
        ef = ef_ref[pl.ds(i * META_TB, META_TB), :]
        oh1 = jnp.where(lane == ef[:, 0:1], 1.0, 0.0)
        oh2 = jnp.where(lane == ef[:, 1:2], 1.0, 0.0)
        return oh1, oh2

    def pass1(i, carry):
        oh1, oh2 = onehots(i)
        cnt = oh1 + oh2
        pref = jnp.dot(tril, cnt.astype(BF16), preferred_element_type=F32) + carry
        pref_ref[pl.ds(i * META_TB, META_TB), :] = pref
        return carry + jnp.sum(cnt, axis=0, keepdims=True)

    counts = lax.fori_loop(0, nb, pass1, jnp.zeros((1, LANES), F32))
    nblk = jnp.floor((counts + (EXPERT_BLOCK - 1)) * (1.0 / EXPERT_BLOCK))
    u_r = lax.broadcasted_iota(I32, (LANES, LANES), 0)
    u_c = lax.broadcasted_iota(I32, (LANES, LANES), 1)
    upper = jnp.where(u_r <= u_c, 1.0, 0.0).astype(BF16)
    nblk8 = jnp.broadcast_to(nblk, (8, LANES))
    pend_blk = jnp.dot(nblk8.astype(BF16), upper, preferred_element_type=F32)[0:1, :]
    pstart_blk = pend_blk - nblk
    pstart = pstart_blk * float(EXPERT_BLOCK)

    def pass2(i, _):
        oh1, oh2 = onehots(i)
        base = pref_ref[pl.ds(i * META_TB, META_TB), :] + pstart
        d1 = jnp.sum(oh1 * base, axis=-1, keepdims=True)
        d2 = jnp.sum(oh2 * base, axis=-1, keepdims=True)
        dest = jnp.where(lane == 0.0, d1, jnp.where(lane == 1.0, d2, 0.0))
        dest_ref[pl.ds(i * META_TB, META_TB), :] = dest.astype(I32)
        return 0

    lax.fori_loop(0, nb, pass2, 0)

    blane = lax.broadcasted_iota(I32, (BLK_ROWS, LANES), 1).astype(F32)
    brow = lax.broadcasted_iota(I32, (BLK_ROWS, LANES), 0).astype(F32)
    emask = blane < N_EXPERTS
    owner = jnp.sum(jnp.where(emask & (pend_blk <= brow), 1.0, 0.0), axis=-1, keepdims=True)
    owner = jnp.minimum(owner, float(N_EXPERTS - 1))
    total = jnp.sum(jnp.where(emask, nblk, 0.0), axis=-1, keepdims=True)
    used = jnp.where(brow < total, 1.0, 0.0)
    blk_ref[...] = jnp.where(blane == 0.0, owner, jnp.where(blane == 1.0, used, 0.0)).astype(I32)


def _meta(ef):
    t = ef.shape[0]
    return pl.pallas_call(
        _meta_kernel,
        out_shape=[jax.ShapeDtypeStruct((t, LANES), I32),
                   jax.ShapeDtypeStruct((BLK_ROWS, LANES), I32)],
        scratch_shapes=[pltpu.VMEM((t, LANES), F32)],
        compiler_params=pltpu.CompilerParams(vmem_limit_bytes=VMEM_LIMIT),
        name="meta",
    )(ef)


DISP_TM = 256


def _dispatch_kernel(dest_ref, h_ref, xin_ref, xbuf_ref, sem):
    del xin_ref
    tm = h_ref.shape[0]

    def row_copy(j, slot):
        d = dest_ref[0, 0, 2 * j + slot]
        return pltpu.make_async_copy(h_ref.at[pl.ds(j, 1), :], xbuf_ref.at[pl.ds(d, 1), :], sem)

    def issue(j, _):
        row_copy(j, 0).start()
        row_copy(j, 1).start()
        return 0

    lax.fori_loop(0, tm, issue, 0, unroll=8)

    def drain(j, _):
        row_copy(j, 0).wait()
        row_copy(j, 1).wait()
        return 0

    lax.fori_loop(0, tm, drain, 0, unroll=8)


def _dispatch(dest3, h2, xbuf0):
    t, d = h2.shape
    tm = DISP_TM
    return pl.pallas_call(
        _dispatch_kernel,
        grid=(t // tm,),
        in_specs=[pl.BlockSpec((1, 1, 2 * tm), lambda i: (i, 0, 0), memory_space=pltpu.SMEM),
                  pl.BlockSpec((tm, d), lambda i: (i, 0)),
                  pl.BlockSpec(memory_space=pl.ANY)],
        out_specs=pl.BlockSpec(memory_space=pl.ANY),
        out_shape=jax.ShapeDtypeStruct(xbuf0.shape, xbuf0.dtype),
        scratch_shapes=[pltpu.SemaphoreType.DMA(())],
        input_output_aliases={2: 0},
        compiler_params=_cparams(("arbitrary",)),
        name="dispatch",
    )(dest3, h2, xbuf0)


CAST_ROWS = 256


def _moe_kernel(blk_e_ref, used_ref, first_ref, nxt_ref, x_ref, wg_hbm, wu_hbm, wd_hbm, o_ref,
                wg_st, wu_st, wd_st, wg_bf, wu_bf, wd_bf, sems):
    b = pl.program_id(0)

    def weight_copies(e):
        return (pltpu.make_async_copy(wg_hbm.at[e], wg_st, sems.at[0]),
                pltpu.make_async_copy(wu_hbm.at[e], wu_st, sems.at[1]),
                pltpu.make_async_copy(wd_hbm.at[e], wd_st, sems.at[2]))

    @pl.when(b == 0)
    def _():
        for cp in weight_copies(blk_e_ref[0]):
            cp.start()

    @pl.when(first_ref[b] == 1)
    def _():
        for cp in weight_copies(blk_e_ref[b]):
            cp.wait()

        def cast(st, bf):
            def body(i, _):
                rows = pl.ds(pl.multiple_of(i * CAST_ROWS, CAST_ROWS), CAST_ROWS)
                bf[rows, :] = st[rows, :].astype(BF16)
                return 0
            lax.fori_loop(0, st.shape[0] // CAST_ROWS, body, 0)

        cast(wg_st, wg_bf)
        cast(wu_st, wu_bf)
        cast(wd_st, wd_bf)

        @pl.when(nxt_ref[b] >= 0)
        def _():
            for cp in weight_copies(nxt_ref[b]):
                cp.start()

    @pl.when(used_ref[b] == 1)
    def _():
        xb = x_ref[...].astype(BF16)
        g = jnp.dot(xb, wg_bf[...], preferred_element_type=F32)
        u = jnp.dot(xb, wu_bf[...], preferred_element_type=F32)
        hmid = (_silu(g) * u).astype(BF16)
        o_ref[...] = jnp.dot(hmid, wd_bf[...], preferred_element_type=F32)

    @pl.when(used_ref[b] == 0)
    def _():
        o_ref[...] = jnp.zeros_like(o_ref)


def _moe(blk_e, used, first, nxt, xbuf, w_gate, w_up, w_down, last_used):
    r, d = xbuf.shape
    n_blk = r // EXPERT_BLOCK
    de = w_gate.shape[2]

    def xmap(b, blk_e_ref, used_ref, first_ref, nxt_ref):
        del blk_e_ref, first_ref, nxt_ref
        return (jnp.where(used_ref[b] == 1, b, 0), 0)

    grid_spec = pltpu.PrefetchScalarGridSpec(
        num_scalar_prefetch=4,
        grid=(n_blk,),
        in_specs=[pl.BlockSpec((EXPERT_BLOCK, d), xmap),
                  pl.BlockSpec(memory_space=pl.ANY),
                  pl.BlockSpec(memory_space=pl.ANY),
                  pl.BlockSpec(memory_space=pl.ANY)],
        out_specs=pl.BlockSpec((EXPERT_BLOCK, d), lambda b, *_: (b, 0)),
        scratch_shapes=[pltpu.VMEM((d, de), F32), pltpu.VMEM((d, de), F32), pltpu.VMEM((de, d), F32),
                        pltpu.VMEM((d, de), BF16), pltpu.VMEM((d, de), BF16), pltpu.VMEM((de, d), BF16),
                        pltpu.SemaphoreType.DMA((3,))],
    )
    del last_used
    return pl.pallas_call(
        _moe_kernel,
        grid_spec=grid_spec,
        out_shape=jax.ShapeDtypeStruct((r, d), F32),
        compiler_params=_cparams(("arbitrary",), vmem=60 * 1024 * 1024),
        name="moe",
    )(blk_e, used, first, nxt, xbuf, w_gate, w_up, w_down)


FIN_TM = 256


def _final_kernel(dest_ref, x_ref, cw_ref, mod_ref, g_ref, y_hbm, o_ref, ybuf, sem):
    tm = x_ref.shape[0]

    def row_copy(j, slot):
        d = dest_ref[0, 0, 2 * j + slot]
        return pltpu.make_async_copy(y_hbm.at[pl.ds(d, 1), :], ybuf.at[slot, pl.ds(j, 1), :], sem)

    def issue(j, _):
        row_copy(j, 0).start()
        row_copy(j, 1).start()
        return 0

    lax.fori_loop(0, tm, issue, 0, unroll=8)

    def drain(j, _):
        row_copy(j, 0).wait()
        row_copy(j, 1).wait()
        return 0

    lax.fori_loop(0, tm, drain, 0, unroll=8)

    cw = cw_ref[...]
    y = ybuf[0] * cw[:, 0:1] + ybuf[1] * cw[:, 1:2]
    xo = x_ref[...] + mod_ref[0, 5:6, :] * y
    ms = jnp.mean(xo * xo, axis=-1, keepdims=True)
    o_ref[...] = xo * lax.rsqrt(ms + EPS) * g_ref[...]


def _final(dest3, x1, cw, mod3, g, ybuf, seq):
    t, d = x1.shape
    tm = FIN_TM
    per_b = seq // tm
    return pl.pallas_call(
        _final_kernel,
        grid=(t // tm,),
        in_specs=[pl.BlockSpec((1, 1, 2 * tm), lambda i: (i, 0, 0), memory_space=pltpu.SMEM),
                  pl.BlockSpec((tm, d), lambda i: (i, 0)),
                  pl.BlockSpec((tm, LANES), lambda i: (i, 0)),
                  pl.BlockSpec((1, 6, d), lambda i: (i // per_b, 0, 0)),
                  pl.BlockSpec((1, d), lambda i: (0, 0)),
                  pl.BlockSpec(memory_space=pl.ANY)],
        out_specs=pl.BlockSpec((tm, d), lambda i: (i, 0)),
        out_shape=jax.ShapeDtypeStruct((t, d), F32),
        scratch_shapes=[pltpu.VMEM((2, tm, d), F32), pltpu.SemaphoreType.DMA(())],
        compiler_params=_cparams(("arbitrary",)),
        name="final",
    )(dest3, x1, cw, mod3, g, ybuf)


def kernel(x, c, w_ada, b_ada, norm1_g, w_in, conv_w, w_conv_out, w_ret_out, w_o, norm2_g,
           w_router_group, b_router_group, w_router_expert, b_router_expert,
           w_gate, w_up, w_down, norm_f_g):
    batch, seq, d = x.shape
    t = batch * seq
    depth = w_ada.shape[0]
    assert d == D_MODEL and w_in.shape[2] == IN_COLS and w_gate.shape[1] == N_EXPERTS
    assert depth == 1, "the final kernel fuses the last rmsnorm into the single layer"
    n_rows = t * 2 + N_EXPERTS * EXPERT_BLOCK
    n_blk = n_rows // EXPERT_BLOCK
    assert n_blk <= BLK_ROWS

    c_pad = jnp.pad(c, ((0, 8 - batch), (0, 0)))
    xs = x.reshape(t, d)
    for l in range(depth):
        mod = _ada(c_pad, w_ada[l], b_ada[l].reshape(1, -1))[:batch]
        mod3 = mod.reshape(batch, 6, d)

        u = _in_proj(xs, mod3, norm1_g[l].reshape(1, d), w_in[l], seq)
        z = _retention(u, batch, seq)
        merged = _mix(u, z, conv_w[l], w_conv_out[l], w_ret_out[l], seq)
        x1 = _out_proj(merged, w_o[l], xs, mod3, seq)

        w_r = jnp.pad(jnp.concatenate([w_router_group[l], w_router_expert[l]], axis=1),
                      ((0, 0), (0, LANES - N_GROUPS - N_EXPERTS)))
        b_r = jnp.pad(jnp.concatenate([b_router_group[l], b_router_expert[l]]),
                      (0, LANES - N_GROUPS - N_EXPERTS)).reshape(1, LANES)
        h2, ef, cw = _route(x1, mod3, norm2_g[l].reshape(1, d), w_r, b_r, seq)

        dest, blk = _meta(ef)
        blk_e = blk[:n_blk, 0]
        used = blk[:n_blk, 1]
        prev_e = jnp.concatenate([jnp.full((1,), -1, I32), blk_e[:-1]])
        first = ((blk_e != prev_e) & (used == 1)).astype(I32)
        idx = jnp.arange(n_blk, dtype=I32)
        run_start = jnp.where(first == 1, idx, n_blk)
        nxt_start = lax.cummin(jnp.concatenate([run_start[1:], jnp.full((1,), n_blk, I32)]),
                               axis=0, reverse=True)
        nxt = jnp.where(nxt_start < n_blk, blk_e[jnp.minimum(nxt_start, n_blk - 1)], -1).astype(I32)

        dest2 = dest[:, :2]
        xbuf = _dispatch(dest2.reshape(t // DISP_TM, 1, 2 * DISP_TM), h2,
                         jnp.zeros((n_rows, d), F32))
        ybuf = _moe(blk_e, used, first, nxt, xbuf, w_gate[l], w_up[l], w_down[l], None)
        xs = _final(dest2.reshape(t // FIN_TM, 1, 2 * FIN_TM), x1, cw, mod3,
                    norm_f_g.reshape(1, d), ybuf, seq)
    return xs.reshape(batch, seq, d)
```

```python
import functools

import numpy as np
import jax
import jax.numpy as jnp
from jax import lax
from jax.experimental import pallas as pl
from jax.experimental.pallas import tpu as pltpu

F32 = jnp.float32
BF16 = jnp.bfloat16
I32 = jnp.int32

D_MODEL = 2048
CONV_CH = 1024
CONV_K = 3
RET_HEADS = 8
RET_DK = 128
RET_DV = 256
CHUNK = 128
ROPE_BASE = 10000.0
N_GROUPS = 4
EXPERTS_PER_GROUP = 8
N_EXPERTS = 32
D_EXPERT = 1024
EXPERT_BLOCK = 128
EPS = 1e-6
IN_COLS = 13312
LANES = 128
COL_BLK = 1024
CB_BLK, CC_BLK, CX_BLK, Q_BLK, K_BLK, V_BLK, RG_BLK, GA_BLK, GB_BLK = 0, 1, 2, 3, 4, 5, 7, 9, 11

VMEM_LIMIT = 56 * 1024 * 1024


def _cparams(sem, vmem=VMEM_LIMIT):
    return pltpu.CompilerParams(dimension_semantics=sem, vmem_limit_bytes=vmem)


def _sigmoid(v):
    return 1.0 / (1.0 + jnp.exp(-v))


def _silu(v):
    return v * _sigmoid(v)


def _ada_kernel(c_ref, w_ref, b_ref, o_ref):
    c = c_ref[...]
    o_ref[...] = jnp.dot(_silu(c), w_ref[...], precision=lax.Precision.HIGHEST,
                         preferred_element_type=F32) + b_ref[...]


def _ada(c_pad, w, b):
    rows, d = c_pad.shape
    n = w.shape[1]
    tn = 512
    return pl.pallas_call(
        _ada_kernel,
        grid=(n // tn,),
        in_specs=[pl.BlockSpec((rows, d), lambda j: (0, 0)),
                  pl.BlockSpec((d, tn), lambda j: (0, j)),
                  pl.BlockSpec((1, tn), lambda j: (0, j))],
        out_specs=pl.BlockSpec((rows, tn), lambda j: (0, j)),
        out_shape=jax.ShapeDtypeStruct((rows, n), F32),
        compiler_params=_cparams(("arbitrary",)),
        name="ada",
    )(c_pad, w, b)


def _rms_mod(x, g, shift, scale):
    ms = jnp.mean(x * x, axis=-1, keepdims=True)
    xn = x * lax.rsqrt(ms + EPS) * g
    return xn * (1.0 + scale) + shift


def _in_kernel(x_ref, mod_ref, g_ref, w_ref, o_ref, h_ref):
    @pl.when(pl.program_id(1) == 0)
    def _():
        h = _rms_mod(x_ref[...], g_ref[...], mod_ref[0, 0:1, :], mod_ref[0, 1:2, :])
        h_ref[...] = h.astype(BF16)

    o_ref[...] = jnp.dot(h_ref[...], w_ref[...].astype(BF16),
                         preferred_element_type=F32).astype(o_ref.dtype)


def _in_proj(x2, mod3, g, w, seq):
    t, d = x2.shape
    n = w.shape[1]
    tm, tn = 1024, 1024
    per_b = seq // tm
    return pl.pallas_call(
        _in_kernel,
        grid=(t // tm, n // tn),
        in_specs=[pl.BlockSpec((tm, d), lambda i, j: (i, 0)),
                  pl.BlockSpec((1, 6, d), lambda i, j: (i // per_b, 0, 0)),
                  pl.BlockSpec((1, d), lambda i, j: (0, 0)),
                  pl.BlockSpec((d, tn), lambda i, j: (0, j))],
        out_specs=pl.BlockSpec((tm, tn), lambda i, j: (i, j)),
        out_shape=jax.ShapeDtypeStruct((t, n), BF16),
        scratch_shapes=[pltpu.VMEM((tm, d), BF16)],
        compiler_params=_cparams(("arbitrary", "arbitrary")),
        name="in_proj",
    )(x2, mod3, g, w)


def _ret_kernel(q_ref, k_ref, v0_ref, v1_ref, rg0_ref, rg1_ref, cos_ref, sin_ref,
                idec_ref, qdec_ref, kdec_ref, o_ref, state_ref, *, chunk_decay):
    @pl.when(pl.program_id(1) == 0)
    def _():
        state_ref[...] = jnp.zeros_like(state_ref)

    cos = cos_ref[...]
    sin = sin_ref[...]

    def rot(t):
        return t * cos + pltpu.roll(t, RET_DK // 2, axis=1) * sin

    for h in range(RET_HEADS):
        v_ref, rg_ref = (v0_ref, rg0_ref) if h < RET_HEADS // 2 else (v1_ref, rg1_ref)
        vc = (h % (RET_HEADS // 2)) * RET_DV
        qh = rot(q_ref[:, h * RET_DK:(h + 1) * RET_DK].astype(F32))
        kh = rot(k_ref[:, h * RET_DK:(h + 1) * RET_DK].astype(F32)) * (RET_DK ** -0.5)
        vh = v_ref[:, vc:vc + RET_DV]
        qb = qh.astype(BF16)
        kb = kh.astype(BF16)
        scores = lax.dot_general(qb, kb, (((1,), (1,)), ((), ())),
                                 preferred_element_type=F32) * idec_ref[h]
        inner = jnp.dot(scores.astype(BF16), vh, preferred_element_type=F32)
        state = state_ref[h]
        cross = jnp.dot((qh * qdec_ref[h]).astype(BF16), state.astype(BF16),
                        preferred_element_type=F32)
        kv = lax.dot_general((kh * kdec_ref[h]).astype(BF16), vh, (((0,), (0,)), ((), ())),
                             preferred_element_type=F32)
        state_ref[h] = state * chunk_decay[h] + kv
        r = inner + cross
        r = r * lax.rsqrt(jnp.mean(r * r, axis=-1, keepdims=True) + EPS)
        gate = _silu(rg_ref[:, vc:vc + RET_DV].astype(F32))
        o_ref[:, h * RET_DV:(h + 1) * RET_DV] = (gate * r).astype(o_ref.dtype)


def _retention(u, batch, seq):
    t = u.shape[0]
    nch = seq // CHUNK
    inv = ROPE_BASE ** (-jnp.arange(0, RET_DK, 2, dtype=F32) / RET_DK)
    ang = jnp.arange(seq, dtype=F32)[:, None] * inv[None, :]
    cos, sin = jnp.cos(ang), jnp.sin(ang)
    cos_t = jnp.concatenate([cos, cos], axis=-1)
    sin_t = jnp.concatenate([-sin, sin], axis=-1)
    log_g = jnp.log1p(-(2.0 ** (-5.0 - jnp.arange(RET_HEADS, dtype=F32))))
    i = jnp.arange(CHUNK, dtype=F32)
    diff = i[:, None] - i[None, :]
    idec = jnp.where(diff >= 0, jnp.exp(log_g[:, None, None] * jnp.maximum(diff, 0.0)), 0.0)
    qdec = jnp.broadcast_to(jnp.exp(log_g[:, None] * (i + 1.0))[:, :, None],
                            (RET_HEADS, CHUNK, RET_DK))
    kdec = jnp.broadcast_to(jnp.exp(log_g[:, None] * (CHUNK - 1.0 - i))[:, :, None],
                            (RET_HEADS, CHUNK, RET_DK))
    log_g_np = np.log1p(-(2.0 ** (-5.0 - np.arange(RET_HEADS, dtype=np.float32)))).astype(np.float32)
    chunk_decay = tuple(float(np.exp(np.float32(lg * np.float32(CHUNK)))) for lg in log_g_np)

    def ublk(c):
        return pl.BlockSpec((CHUNK, COL_BLK), lambda b, n, c=c: (b * nch + n, c))

    def table():
        return pl.BlockSpec((RET_HEADS, CHUNK, RET_DK), lambda b, n: (0, 0, 0))

    return pl.pallas_call(
        functools.partial(_ret_kernel, chunk_decay=chunk_decay),
        grid=(batch, nch),
        in_specs=[ublk(Q_BLK), ublk(K_BLK), ublk(V_BLK), ublk(V_BLK + 1),
                  ublk(RG_BLK), ublk(RG_BLK + 1),
                  pl.BlockSpec((CHUNK, RET_DK), lambda b, n: (n, 0)),
                  pl.BlockSpec((CHUNK, RET_DK), lambda b, n: (n, 0)),
                  table(), table(), table()],
        out_specs=pl.BlockSpec((CHUNK, RET_HEADS * RET_DV), lambda b, n: (b * nch + n, 0)),
        out_shape=jax.ShapeDtypeStruct((t, RET_HEADS * RET_DV), BF16),
        scratch_shapes=[pltpu.VMEM((RET_HEADS, RET_DK, RET_DV), F32)],
        compiler_params=_cparams(("arbitrary", "arbitrary")),
        name="retention",
    )(u, u, u, u, u, u, cos_t, sin_t, idec, qdec, kdec)


HALO = 8


MIX_TN = 512


def _mix_kernel(cb_ref, cc_ref, cx_ref, hcc_ref, hcx_ref, z_ref, ga0_ref, ga1_ref, gb0_ref, gb1_ref,
                cw_ref, wc_ref, wr_ref, wo_ref, x_ref, mod_ref, o_ref, m_ref, mg_ref,
                *, tiles_per_seq):
    tm = cb_ref.shape[0]
    d = o_ref.shape[1]
    m = cc_ref[...].astype(F32) * cx_ref[...].astype(F32)
    halo = hcc_ref[...].astype(F32) * hcx_ref[...].astype(F32)
    first = (pl.program_id(0) % tiles_per_seq) == 0
    m_ref[0:HALO, :] = jnp.where(first, 0.0, halo)
    m_ref[HALO:HALO + tm, :] = m
    conv = (m_ref[HALO - 2:HALO - 2 + tm, :] * cw_ref[0:1, :]
            + m_ref[HALO - 1:HALO - 1 + tm, :] * cw_ref[1:2, :]
            + m * cw_ref[2:3, :])
    p = (cb_ref[...].astype(F32) * conv).astype(BF16)
    z = z_ref[...]
    for n in range(d // MIX_TN):
        cols = slice(n * MIX_TN, (n + 1) * MIX_TN)
        ga_ref, gb_ref = (ga0_ref, gb0_ref) if n < COL_BLK // MIX_TN else (ga1_ref, gb1_ref)
        gcols = slice((n * MIX_TN) % COL_BLK, (n * MIX_TN) % COL_BLK + MIX_TN)
        ya = jnp.dot(p, wc_ref[:, cols], preferred_element_type=F32)
        yb = jnp.dot(z, wr_ref[:, cols], preferred_element_type=F32)
        merged = (_sigmoid(ga_ref[:, gcols].astype(F32)) * ya
                  + _sigmoid(gb_ref[:, gcols].astype(F32)) * yb)
        mg_ref[:, cols] = merged.astype(BF16)
    mg = mg_ref[...]
    for n in range(d // MIX_TN):
        cols = slice(n * MIX_TN, (n + 1) * MIX_TN)
        acc = jnp.dot(mg, wo_ref[:, cols], preferred_element_type=F32)
        o_ref[:, cols] = x_ref[:, cols] + mod_ref[0, 2:3, cols] * acc


def _mix(u, z, conv_w, wc_bf, wr_bf, wo_bf, x2, mod3, seq):
    t, d = x2.shape
    tm = 256
    hb = tm // HALO
    per_b = seq // tm

    def ublk(c):
        return pl.BlockSpec((tm, COL_BLK), lambda i, c=c: (i, c))

    def halo(c):
        return pl.BlockSpec((HALO, COL_BLK), lambda i, c=c: (jnp.maximum(i * hb - 1, 0), c))

    def resident(shape):
        return pl.BlockSpec(shape, lambda i: (0, 0), pipeline_mode=pl.Buffered(1))

    return pl.pallas_call(
        functools.partial(_mix_kernel, tiles_per_seq=per_b),
        grid=(t // tm,),
        in_specs=[ublk(CB_BLK), ublk(CC_BLK), ublk(CX_BLK), halo(CC_BLK), halo(CX_BLK),
                  pl.BlockSpec((tm, z.shape[1]), lambda i: (i, 0)),
                  ublk(GA_BLK), ublk(GA_BLK + 1), ublk(GB_BLK), ublk(GB_BLK + 1),
                  pl.BlockSpec((CONV_K, CONV_CH), lambda i: (0, 0)),
                  resident(wc_bf.shape), resident(wr_bf.shape), resident(wo_bf.shape),
                  pl.BlockSpec((tm, d), lambda i: (i, 0)),
                  pl.BlockSpec((1, 6, d), lambda i: (i // per_b, 0, 0))],
        out_specs=pl.BlockSpec((tm, d), lambda i: (i, 0)),
        out_shape=jax.ShapeDtypeStruct((t, d), F32),
        scratch_shapes=[pltpu.VMEM((tm + HALO, CONV_CH), F32),
                        pltpu.VMEM((tm, d), BF16)],
        compiler_params=_cparams(("arbitrary",)),
        name="mix",
    )(u, u, u, u, u, z, u, u, u, u, conv_w, wc_bf, wr_bf, wo_bf, x2, mod3)


ROUTE_E0 = N_GROUPS


def _split_bf16(v):
    hi = v.astype(BF16)
    lo = (v - hi.astype(F32)).astype(BF16)
    return hi, lo


def _route_kernel(x_ref, mod_ref, g_ref, wr_ref, br_ref, h_ref, ef_ref, cw_ref):
    h2 = _rms_mod(x_ref[...], g_ref[...], mod_ref[0, 3:4, :], mod_ref[0, 4:5, :])
    h_ref[...] = h2
    hh, hl = _split_bf16(h2)
    wh, wl = _split_bf16(wr_ref[...])
    logits = (jnp.dot(hh, wh, preferred_element_type=F32)
              + (jnp.dot(hh, wl, preferred_element_type=F32)
                 + jnp.dot(hl, wh, preferred_element_type=F32))) + br_ref[...]
    lane = lax.broadcasted_iota(I32, logits.shape, 1).astype(F32)
    neg = -jnp.inf
    big = float(LANES)
    gl = jnp.where(lane < N_GROUPS, logits, neg)
    gmax = jnp.max(gl, axis=-1, keepdims=True)
    gsel = jnp.min(jnp.where(gl == gmax, lane, big), axis=-1, keepdims=True)
    gsum = jnp.sum(jnp.exp(gl - gmax), axis=-1, keepdims=True)
    g_w = 1.0 / gsum
    lo_lane = ROUTE_E0 + EXPERTS_PER_GROUP * gsel
    emask = (lane >= lo_lane) & (lane < lo_lane + EXPERTS_PER_GROUP)
    el = jnp.where(emask, logits, neg)
    m1 = jnp.max(el, axis=-1, keepdims=True)
    i1 = jnp.min(jnp.where(el == m1, lane, big), axis=-1, keepdims=True)
    el2 = jnp.where(lane == i1, neg, el)
    m2 = jnp.max(el2, axis=-1, keepdims=True)
    i2 = jnp.min(jnp.where(el2 == m2, lane, big), axis=-1, keepdims=True)
    esum = jnp.sum(jnp.exp(el - m1), axis=-1, keepdims=True)
    p1 = 1.0 / esum
    p2 = jnp.exp(m2 - m1) / esum
    tot = p1 + p2
    c1 = g_w * (p1 / tot)
    c2 = g_w * (p2 / tot)
    ef_ref[...] = jnp.where(lane == 0.0, i1 - ROUTE_E0, jnp.where(lane == 1.0, i2 - ROUTE_E0, 0.0))
    cw_ref[...] = jnp.where(lane == 0.0, c1, jnp.where(lane == 1.0, c2, 0.0))


def _route(x1, mod3, g, w_r, b_r, seq):
    t, d = x1.shape
    tm = 512
    per_b = seq // tm
    return pl.pallas_call(
        _route_kernel,
        grid=(t // tm,),
        in_specs=[pl.BlockSpec((tm, d), lambda i: (i, 0)),
                  pl.BlockSpec((1, 6, d), lambda i: (i // per_b, 0, 0)),
                  pl.BlockSpec((1, d), lambda i: (0, 0)),
                  pl.BlockSpec((d, LANES), lambda i: (0, 0)),
                  pl.BlockSpec((1, LANES), lambda i: (0, 0))],
        out_specs=[pl.BlockSpec((tm, d), lambda i: (i, 0)),
                   pl.BlockSpec((tm, LANES), lambda i: (i, 0)),
                   pl.BlockSpec((tm, LANES), lambda i: (i, 0))],
        out_shape=[jax.ShapeDtypeStruct((t, d), F32),
                   jax.ShapeDtypeStruct((t, LANES), F32),
                   jax.ShapeDtypeStruct((t, LANES), F32)],
        compiler_params=_cparams(("arbitrary",)),
        name="route",
    )(x1, mod3, g, w_r, b_r)


META_TB = 512
BLK_ROWS = 256


def _meta_kernel(ef_ref, dest_ref, blk_ref, pref_ref):
    t = ef_ref.shape[0]
    nb = t // META_TB
    lane = lax.broadcasted_iota(I32, (META_TB, LANES), 1).astype(F32)
    r_i = lax.broadcasted_iota(I32, (META_TB, META_TB), 0)
    c_i = lax.broadcasted_iota(I32, (META_TB, META_TB), 1)
    tril = jnp.where(r_i > c_i, 1.0, 0.0).astype(BF16)

    def onehots(i):
        ef = ef_ref[pl.ds(i * META_TB, META_TB), :]
        oh1 = jnp.where(lane == ef[:, 0:1], 1.0, 0.0)
        oh2 = jnp.where(lane == ef[:, 1:2], 1.0, 0.0)
        return oh1, oh2

    def pass1(i, carry):
        oh1, oh2 = onehots(i)
        cnt = oh1 + oh2
        pref = jnp.dot(tril, cnt.astype(BF16), preferred_element_type=F32) + carry
        pref_ref[pl.ds(i * META_TB, META_TB), :] = pref
        return carry + jnp.sum(cnt, axis=0, keepdims=True)

    counts = lax.fori_loop(0, nb, pass1, jnp.zeros((1, LANES), F32))
    nblk = jnp.floor((counts + (EXPERT_BLOCK - 1)) * (1.0 / EXPERT_BLOCK))
    u_r = lax.broadcasted_iota(I32, (LANES, LANES), 0)
    u_c = lax.broadcasted_iota(I32, (LANES, LANES), 1)
    upper = jnp.where(u_r <= u_c, 1.0, 0.0).astype(BF16)
    nblk8 = jnp.broadcast_to(nblk, (8, LANES))
    pend_blk = jnp.dot(nblk8.astype(BF16), upper, preferred_element_type=F32)[0:1, :]
    pstart_blk = pend_blk - nblk
    pstart = pstart_blk * float(EXPERT_BLOCK)

    def pass2(i, _):
        oh1, oh2 = onehots(i)
        base = pref_ref[pl.ds(i * META_TB, META_TB), :] + pstart
        d1 = jnp.sum(oh1 * base, axis=-1, keepdims=True)
        d2 = jnp.sum(oh2 * base, axis=-1, keepdims=True)
        dest = jnp.where(lane == 0.0, d1, jnp.where(lane == 1.0, d2, 0.0))
        dest_ref[pl.ds(i * META_TB, META_TB), :] = dest.astype(I32)
        return 0

    lax.fori_loop(0, nb, pass2, 0)

    blane = lax.broadcasted_iota(I32, (BLK_ROWS, LANES), 1).astype(F32)
    brow = lax.broadcasted_iota(I32, (BLK_ROWS, LANES), 0).astype(F32)
    emask = blane < N_EXPERTS
    owner = jnp.sum(jnp.where(emask & (pend_blk <= brow), 1.0, 0.0), axis=-1, keepdims=True)
    owner = jnp.minimum(owner, float(N_EXPERTS - 1))
    total = jnp.sum(jnp.where(emask, nblk, 0.0), axis=-1, keepdims=True)
    used = jnp.where(brow < total, 1.0, 0.0)
    blk_ref[...] = jnp.where(blane == 0.0, owner, jnp.where(blane == 1.0, used, 0.0)).astype(I32)


def _meta(ef):
    t = ef.shape[0]
    return pl.pallas_call(
        _meta_kernel,
        out_shape=[jax.ShapeDtypeStruct((t, LANES), I32),
                   jax.ShapeDtypeStruct((BLK_ROWS, LANES), I32)],
        scratch_shapes=[pltpu.VMEM((t, LANES), F32)],
        compiler_params=pltpu.CompilerParams(vmem_limit_bytes=VMEM_LIMIT),
        name="meta",
    )(ef)


DISP_TM = 256


def _dispatch_kernel(dest_ref, h_ref, xin_ref, xbuf_ref, sem):
    del xin_ref
    tm = h_ref.shape[0]

    def row_copy(j, slot):
        d = dest_ref[0, 0, 2 * j + slot]
        return pltpu.make_async_copy(h_ref.at[pl.ds(j, 1), :], xbuf_ref.at[pl.ds(d, 1), :], sem)

    def issue(j, _):
        row_copy(j, 0).start()
        row_copy(j, 1).start()
        return 0

    lax.fori_loop(0, tm, issue, 0, unroll=8)

    def drain(j, _):
        row_copy(j, 0).wait()
        row_copy(j, 1).wait()
        return 0

    lax.fori_loop(0, tm, drain, 0, unroll=8)


def _dispatch(dest3, h2, xbuf0):
    t, d = h2.shape
    tm = DISP_TM
    return pl.pallas_call(
        _dispatch_kernel,
        grid=(t // tm,),
        in_specs=[pl.BlockSpec((1, 1, 2 * tm), lambda i: (i, 0, 0), memory_space=pltpu.SMEM),
                  pl.BlockSpec((tm, d), lambda i: (i, 0)),
                  pl.BlockSpec(memory_space=pl.ANY)],
        out_specs=pl.BlockSpec(memory_space=pl.ANY),
        out_shape=jax.ShapeDtypeStruct(xbuf0.shape, xbuf0.dtype),
        scratch_shapes=[pltpu.SemaphoreType.DMA(())],
        input_output_aliases={2: 0},
        compiler_params=_cparams(("arbitrary",)),
        name="dispatch",
    )(dest3, h2, xbuf0)


CAST_ROWS = 256


def _moe_kernel(blk_e_ref, used_ref, first_ref, nxt_ref, x_ref, wg_hbm, wu_hbm, wd_hbm, o_ref,
                wg_st, wu_st, wd_st, wg_bf, wu_bf, wd_bf, sems):
    b = pl.program_id(0)

    def weight_copies(e):
        return (pltpu.make_async_copy(wg_hbm.at[e], wg_st, sems.at[0]),
                pltpu.make_async_copy(wu_hbm.at[e], wu_st, sems.at[1]),
                pltpu.make_async_copy(wd_hbm.at[e], wd_st, sems.at[2]))

    @pl.when(b == 0)
    def _():
        for cp in weight_copies(blk_e_ref[0]):
            cp.start(priority=1)

    @pl.when(first_ref[b] == 1)
    def _():
        for cp in weight_copies(blk_e_ref[b]):
            cp.wait()

        def cast(st, bf):
            def body(i, _):
                rows = pl.ds(pl.multiple_of(i * CAST_ROWS, CAST_ROWS), CAST_ROWS)
                bf[rows, :] = st[rows, :].astype(BF16)
                return 0
            lax.fori_loop(0, st.shape[0] // CAST_ROWS, body, 0)

        cast(wg_st, wg_bf)
        cast(wu_st, wu_bf)
        cast(wd_st, wd_bf)

        @pl.when(nxt_ref[b] >= 0)
        def _():
            for cp in weight_copies(nxt_ref[b]):
                cp.start(priority=1)

    @pl.when(used_ref[b] == 1)
    def _():
        xb = x_ref[...].astype(BF16)
        g = jnp.dot(xb, wg_bf[...], preferred_element_type=F32)
        u = jnp.dot(xb, wu_bf[...], preferred_element_type=F32)
        hmid = (_silu(g) * u).astype(BF16)
        o_ref[...] = jnp.dot(hmid, wd_bf[...], preferred_element_type=F32)

    @pl.when(used_ref[b] == 0)
    def _():
        o_ref[...] = jnp.zeros_like(o_ref)


def _moe(blk_e, used, first, nxt, xbuf, w_gate, w_up, w_down, last_used):
    r, d = xbuf.shape
    n_blk = r // EXPERT_BLOCK
    de = w_gate.shape[2]

    def xmap(b, blk_e_ref, used_ref, first_ref, nxt_ref):
        del blk_e_ref, first_ref, nxt_ref
        return (jnp.where(used_ref[b] == 1, b, 0), 0)

    grid_spec = pltpu.PrefetchScalarGridSpec(
        num_scalar_prefetch=4,
        grid=(n_blk,),
        in_specs=[pl.BlockSpec((EXPERT_BLOCK, d), xmap),
                  pl.BlockSpec(memory_space=pl.ANY),
                  pl.BlockSpec(memory_space=pl.ANY),
                  pl.BlockSpec(memory_space=pl.ANY)],
        out_specs=pl.BlockSpec((EXPERT_BLOCK, d), lambda b, *_: (b, 0)),
        scratch_shapes=[pltpu.VMEM((d, de), F32), pltpu.VMEM((d, de), F32), pltpu.VMEM((de, d), F32),
                        pltpu.VMEM((d, de), BF16), pltpu.VMEM((d, de), BF16), pltpu.VMEM((de, d), BF16),
                        pltpu.SemaphoreType.DMA((3,))],
    )
    del last_used
    return pl.pallas_call(
        _moe_kernel,
        grid_spec=grid_spec,
        out_shape=jax.ShapeDtypeStruct((r, d), F32),
        compiler_params=_cparams(("arbitrary",), vmem=60 * 1024 * 1024),
        name="moe",
    )(blk_e, used, first, nxt, xbuf, w_gate, w_up, w_down)


FIN_TM = 256


def _final_kernel(dest_ref, x_ref, cw_ref, mod_ref, g_ref, y_hbm, o_ref, ybuf, sem):
    tm = x_ref.shape[0]

    def row_copy(j, slot):
        d = dest_ref[0, 0, 2 * j + slot]
        return pltpu.make_async_copy(y_hbm.at[pl.ds(d, 1), :], ybuf.at[slot, pl.ds(j, 1), :], sem)

    def issue(j, _):
        row_copy(j, 0).start()
        row_copy(j, 1).start()
        return 0

    lax.fori_loop(0, tm, issue, 0, unroll=8)

    def drain(j, _):
        row_copy(j, 0).wait()
        row_copy(j, 1).wait()
        return 0

    lax.fori_loop(0, tm, drain, 0, unroll=8)

    cw = cw_ref[...]
    y = ybuf[0] * cw[:, 0:1] + ybuf[1] * cw[:, 1:2]
    xo = x_ref[...] + mod_ref[0, 5:6, :] * y
    ms = jnp.mean(xo * xo, axis=-1, keepdims=True)
    o_ref[...] = xo * lax.rsqrt(ms + EPS) * g_ref[...]


def _final(dest3, x1, cw, mod3, g, ybuf, seq):
    t, d = x1.shape
    tm = FIN_TM
    per_b = seq // tm
    return pl.pallas_call(
        _final_kernel,
        grid=(t // tm,),
        in_specs=[pl.BlockSpec((1, 1, 2 * tm), lambda i: (i, 0, 0), memory_space=pltpu.SMEM),
                  pl.BlockSpec((tm, d), lambda i: (i, 0)),
                  pl.BlockSpec((tm, LANES), lambda i: (i, 0)),
                  pl.BlockSpec((1, 6, d), lambda i: (i // per_b, 0, 0)),
                  pl.BlockSpec((1, d), lambda i: (0, 0)),
                  pl.BlockSpec(memory_space=pl.ANY)],
        out_specs=pl.BlockSpec((tm, d), lambda i: (i, 0)),
        out_shape=jax.ShapeDtypeStruct((t, d), F32),
        scratch_shapes=[pltpu.VMEM((2, tm, d), F32), pltpu.SemaphoreType.DMA(())],
        compiler_params=_cparams(("arbitrary",)),
        name="final",
    )(dest3, x1, cw, mod3, g, ybuf)


def kernel(x, c, w_ada, b_ada, norm1_g, w_in, conv_w, w_conv_out, w_ret_out, w_o, norm2_g,
           w_router_group, b_router_group, w_router_expert, b_router_expert,
           w_gate, w_up, w_down, norm_f_g):
    batch, seq, d = x.shape
    t = batch * seq
    depth = w_ada.shape[0]
    assert d == D_MODEL and w_in.shape[2] == IN_COLS and w_gate.shape[1] == N_EXPERTS
    assert depth == 1, "the final kernel fuses the last rmsnorm into the single layer"
    n_rows = t * 2 + N_EXPERTS * EXPERT_BLOCK
    n_blk = n_rows // EXPERT_BLOCK
    assert n_blk <= BLK_ROWS

    c_pad = jnp.pad(c, ((0, 8 - batch), (0, 0)))
    xs = x.reshape(t, d)
    for l in range(depth):
        mod = _ada(c_pad, w_ada[l], b_ada[l].reshape(1, -1))[:batch]
        mod3 = mod.reshape(batch, 6, d)

        u = _in_proj(xs, mod3, norm1_g[l].reshape(1, d), w_in[l], seq)
        z = _retention(u, batch, seq)
        x1 = _mix(u, z, conv_w[l], w_conv_out[l].astype(BF16), w_ret_out[l].astype(BF16),
                  w_o[l].astype(BF16), xs, mod3, seq)

        w_r = jnp.pad(jnp.concatenate([w_router_group[l], w_router_expert[l]], axis=1),
                      ((0, 0), (0, LANES - N_GROUPS - N_EXPERTS)))
        b_r = jnp.pad(jnp.concatenate([b_router_group[l], b_router_expert[l]]),
                      (0, LANES - N_GROUPS - N_EXPERTS)).reshape(1, LANES)
        h2, ef, cw = _route(x1, mod3, norm2_g[l].reshape(1, d), w_r, b_r, seq)

        dest, blk = _meta(ef)
        blk_e = blk[:n_blk, 0]
        used = blk[:n_blk, 1]
        prev_e = jnp.concatenate([jnp.full((1,), -1, I32), blk_e[:-1]])
        first = ((blk_e != prev_e) & (used == 1)).astype(I32)
        idx = jnp.arange(n_blk, dtype=I32)
        run_start = jnp.where(first == 1, idx, n_blk)
        nxt_start = lax.cummin(jnp.concatenate([run_start[1:], jnp.full((1,), n_blk, I32)]),
                               axis=0, reverse=True)
        nxt = jnp.where(nxt_start < n_blk, blk_e[jnp.minimum(nxt_start, n_blk - 1)], -1).astype(I32)

        dest2 = dest[:, :2]
        xbuf = _dispatch(dest2.reshape(t // DISP_TM, 1, 2 * DISP_TM), h2,
                         jnp.zeros((n_rows, d), F32))
        ybuf = _moe(blk_e, used, first, nxt, xbuf, w_gate[l], w_up[l], w_down[l], None)
        xs = _final(dest2.reshape(t // FIN_TM, 1, 2 * FIN_TM), x1, cw, mod3,
                    norm_f_g.reshape(1, d), ybuf, seq)
    return xs.reshape(batch, seq, d)
```

```python
import functools

import numpy as np
import jax
import jax.numpy as jnp
from jax import lax
from jax.experimental import pallas as pl
from jax.experimental.pallas import tpu as pltpu

F32 = jnp.float32
BF16 = jnp.bfloat16
I32 = jnp.int32

D_MODEL = 2048
CONV_CH = 1024
CONV_K = 3
RET_HEADS = 8
RET_DK = 128
RET_DV = 256
CHUNK = 128
ROPE_BASE = 10000.0
N_GROUPS = 4
EXPERTS_PER_GROUP = 8
N_EXPERTS = 32
D_EXPERT = 1024
EXPERT_BLOCK = 128
EPS = 1e-6
IN_COLS = 13312
LANES = 128
COL_BLK = 1024
CB_BLK, CC_BLK, CX_BLK, Q_BLK, K_BLK, V_BLK, RG_BLK, GA_BLK, GB_BLK = 0, 1, 2, 3, 4, 5, 7, 9, 11

VMEM_LIMIT = 56 * 1024 * 1024


def _cparams(sem, vmem=VMEM_LIMIT):
    return pltpu.CompilerParams(dimension_semantics=sem, vmem_limit_bytes=vmem)


def _sigmoid(v):
    return 1.0 / (1.0 + jnp.exp(-v))


def _silu(v):
    return v * _sigmoid(v)


def _ada_kernel(c_ref, w_ref, b_ref, o_ref):
    c = c_ref[...]
    o_ref[...] = jnp.dot(_silu(c), w_ref[...], precision=lax.Precision.HIGHEST,
                         preferred_element_type=F32) + b_ref[...]


def _ada(c_pad, w, b):
    rows, d = c_pad.shape
    n = w.shape[1]
    tn = 512
    return pl.pallas_call(
        _ada_kernel,
        grid=(n // tn,),
        in_specs=[pl.BlockSpec((rows, d), lambda j: (0, 0)),
                  pl.BlockSpec((d, tn), lambda j: (0, j)),
                  pl.BlockSpec((1, tn), lambda j: (0, j))],
        out_specs=pl.BlockSpec((rows, tn), lambda j: (0, j)),
        out_shape=jax.ShapeDtypeStruct((rows, n), F32),
        compiler_params=_cparams(("arbitrary",)),
        name="ada",
    )(c_pad, w, b)


def _rms_mod(x, g, shift, scale):
    ms = jnp.mean(x * x, axis=-1, keepdims=True)
    xn = x * lax.rsqrt(ms + EPS) * g
    return xn * (1.0 + scale) + shift


def _in_kernel(x_ref, mod_ref, g_ref, w_ref, o_ref, h_ref):
    @pl.when(pl.program_id(1) == 0)
    def _():
        h = _rms_mod(x_ref[...], g_ref[...], mod_ref[0, 0:1, :], mod_ref[0, 1:2, :])
        h_ref[...] = h.astype(BF16)

    o_ref[...] = jnp.dot(h_ref[...], w_ref[...].astype(BF16),
                         preferred_element_type=F32).astype(o_ref.dtype)


def _in_proj(x2, mod3, g, w, seq):
    t, d = x2.shape
    n = w.shape[1]
    tm, tn = 1024, 1024
    per_b = seq // tm
    return pl.pallas_call(
        _in_kernel,
        grid=(t // tm, n // tn),
        in_specs=[pl.BlockSpec((tm, d), lambda i, j: (i, 0)),
                  pl.BlockSpec((1, 6, d), lambda i, j: (i // per_b, 0, 0)),
                  pl.BlockSpec((1, d), lambda i, j: (0, 0)),
                  pl.BlockSpec((d, tn), lambda i, j: (0, j))],
        out_specs=pl.BlockSpec((tm, tn), lambda i, j: (i, j)),
        out_shape=jax.ShapeDtypeStruct((t, n), BF16),
        scratch_shapes=[pltpu.VMEM((tm, d), BF16)],
        compiler_params=_cparams(("arbitrary", "arbitrary")),
        name="in_proj",
    )(x2, mod3, g, w)


def _ret_kernel(q_ref, k_ref, v0_ref, v1_ref, rg0_ref, rg1_ref, cos_ref, sin_ref,
                idec_ref, qdec_ref, kdec_ref, o_ref, state_ref, *, chunk_decay):
    @pl.when(pl.program_id(1) == 0)
    def _():
        state_ref[...] = jnp.zeros_like(state_ref)

    cos = cos_ref[...]
    sin = sin_ref[...]

    def rot(t):
        return t * cos + pltpu.roll(t, RET_DK // 2, axis=1) * sin

    for h in range(RET_HEADS):
        v_ref, rg_ref = (v0_ref, rg0_ref) if h < RET_HEADS // 2 else (v1_ref, rg1_ref)
        vc = (h % (RET_HEADS // 2)) * RET_DV
        qh = rot(q_ref[:, h * RET_DK:(h + 1) * RET_DK].astype(F32))
        kh = rot(k_ref[:, h * RET_DK:(h + 1) * RET_DK].astype(F32)) * (RET_DK ** -0.5)
        vh = v_ref[:, vc:vc + RET_DV]
        qb = qh.astype(BF16)
        kb = kh.astype(BF16)
        scores = lax.dot_general(qb, kb, (((1,), (1,)), ((), ())),
                                 preferred_element_type=F32) * idec_ref[h]
        inner = jnp.dot(scores.astype(BF16), vh, preferred_element_type=F32)
        state = state_ref[h]
        cross = jnp.dot((qh * qdec_ref[h]).astype(BF16), state.astype(BF16),
                        preferred_element_type=F32)
        kv = lax.dot_general((kh * kdec_ref[h]).astype(BF16), vh, (((0,), (0,)), ((), ())),
                             preferred_element_type=F32)
        state_ref[h] = state * chunk_decay[h] + kv
        r = inner + cross
        r = r * lax.rsqrt(jnp.mean(r * r, axis=-1, keepdims=True) + EPS)
        gate = _silu(rg_ref[:, vc:vc + RET_DV].astype(F32))
        o_ref[:, h * RET_DV:(h + 1) * RET_DV] = (gate * r).astype(o_ref.dtype)


def _retention(u, batch, seq):
    t = u.shape[0]
    nch = seq // CHUNK
    inv = ROPE_BASE ** (-jnp.arange(0, RET_DK, 2, dtype=F32) / RET_DK)
    ang = jnp.arange(seq, dtype=F32)[:, None] * inv[None, :]
    cos, sin = jnp.cos(ang), jnp.sin(ang)
    cos_t = jnp.concatenate([cos, cos], axis=-1)
    sin_t = jnp.concatenate([-sin, sin], axis=-1)
    log_g = jnp.log1p(-(2.0 ** (-5.0 - jnp.arange(RET_HEADS, dtype=F32))))
    i = jnp.arange(CHUNK, dtype=F32)
    diff = i[:, None] - i[None, :]
    idec = jnp.where(diff >= 0, jnp.exp(log_g[:, None, None] * jnp.maximum(diff, 0.0)), 0.0)
    qdec = jnp.broadcast_to(jnp.exp(log_g[:, None] * (i + 1.0))[:, :, None],
                            (RET_HEADS, CHUNK, RET_DK))
    kdec = jnp.broadcast_to(jnp.exp(log_g[:, None] * (CHUNK - 1.0 - i))[:, :, None],
                            (RET_HEADS, CHUNK, RET_DK))
    log_g_np = np.log1p(-(2.0 ** (-5.0 - np.arange(RET_HEADS, dtype=np.float32)))).astype(np.float32)
    chunk_decay = tuple(float(np.exp(np.float32(lg * np.float32(CHUNK)))) for lg in log_g_np)

    def ublk(c):
        return pl.BlockSpec((CHUNK, COL_BLK), lambda b, n, c=c: (b * nch + n, c))

    def table():
        return pl.BlockSpec((RET_HEADS, CHUNK, RET_DK), lambda b, n: (0, 0, 0))

    return pl.pallas_call(
        functools.partial(_ret_kernel, chunk_decay=chunk_decay),
        grid=(batch, nch),
        in_specs=[ublk(Q_BLK), ublk(K_BLK), ublk(V_BLK), ublk(V_BLK + 1),
                  ublk(RG_BLK), ublk(RG_BLK + 1),
                  pl.BlockSpec((CHUNK, RET_DK), lambda b, n: (n, 0)),
                  pl.BlockSpec((CHUNK, RET_DK), lambda b, n: (n, 0)),
                  table(), table(), table()],
        out_specs=pl.BlockSpec((CHUNK, RET_HEADS * RET_DV), lambda b, n: (b * nch + n, 0)),
        out_shape=jax.ShapeDtypeStruct((t, RET_HEADS * RET_DV), BF16),
        scratch_shapes=[pltpu.VMEM((RET_HEADS, RET_DK, RET_DV), F32)],
        compiler_params=_cparams(("arbitrary", "arbitrary")),
        name="retention",
    )(u, u, u, u, u, u, cos_t, sin_t, idec, qdec, kdec)


HALO = 8


MIX_TN = 512


def _mix_kernel(cb_ref, cc_ref, cx_ref, hcc_ref, hcx_ref, z_ref, ga0_ref, ga1_ref, gb0_ref, gb1_ref,
                cw_ref, wc_ref, wr_ref, wo_ref, x_ref, mod_ref, o_ref, m_ref, mg_ref,
                *, tiles_per_seq):
    tm = cb_ref.shape[0]
    d = o_ref.shape[1]
    m = cc_ref[...].astype(F32) * cx_ref[...].astype(F32)
    halo = hcc_ref[...].astype(F32) * hcx_ref[...].astype(F32)
    first = (pl.program_id(0) % tiles_per_seq) == 0
    m_ref[0:HALO, :] = jnp.where(first, 0.0, halo)
    m_ref[HALO:HALO + tm, :] = m
    conv = (m_ref[HALO - 2:HALO - 2 + tm, :] * cw_ref[0:1, :]
            + m_ref[HALO - 1:HALO - 1 + tm, :] * cw_ref[1:2, :]
            + m * cw_ref[2:3, :])
    p = (cb_ref[...].astype(F32) * conv).astype(BF16)
    z = z_ref[...]
    for n in range(d // MIX_TN):
        cols = slice(n * MIX_TN, (n + 1) * MIX_TN)
        ga_ref, gb_ref = (ga0_ref, gb0_ref) if n < COL_BLK // MIX_TN else (ga1_ref, gb1_ref)
        gcols = slice((n * MIX_TN) % COL_BLK, (n * MIX_TN) % COL_BLK + MIX_TN)
        ya = jnp.dot(p, wc_ref[:, cols], preferred_element_type=F32)
        yb = jnp.dot(z, wr_ref[:, cols], preferred_element_type=F32)
        merged = (_sigmoid(ga_ref[:, gcols].astype(F32)) * ya
                  + _sigmoid(gb_ref[:, gcols].astype(F32)) * yb)
        mg_ref[:, cols] = merged.astype(BF16)
    mg = mg_ref[...]
    for n in range(d // MIX_TN):
        cols = slice(n * MIX_TN, (n + 1) * MIX_TN)
        acc = jnp.dot(mg, wo_ref[:, cols], preferred_element_type=F32)
        o_ref[:, cols] = x_ref[:, cols] + mod_ref[0, 2:3, cols] * acc


def _mix(u, z, conv_w, wc_bf, wr_bf, wo_bf, x2, mod3, seq):
    t, d = x2.shape
    tm = 256
    hb = tm // HALO
    per_b = seq // tm

    def ublk(c):
        return pl.BlockSpec((tm, COL_BLK), lambda i, c=c: (i, c))

    def halo(c):
        return pl.BlockSpec((HALO, COL_BLK), lambda i, c=c: (jnp.maximum(i * hb - 1, 0), c))

    def resident(shape):
        return pl.BlockSpec(shape, lambda i: (0, 0), pipeline_mode=pl.Buffered(1))

    return pl.pallas_call(
        functools.partial(_mix_kernel, tiles_per_seq=per_b),
        grid=(t // tm,),
        in_specs=[ublk(CB_BLK), ublk(CC_BLK), ublk(CX_BLK), halo(CC_BLK), halo(CX_BLK),
                  pl.BlockSpec((tm, z.shape[1]), lambda i: (i, 0)),
                  ublk(GA_BLK), ublk(GA_BLK + 1), ublk(GB_BLK), ublk(GB_BLK + 1),
                  pl.BlockSpec((CONV_K, CONV_CH), lambda i: (0, 0)),
                  resident(wc_bf.shape), resident(wr_bf.shape), resident(wo_bf.shape),
                  pl.BlockSpec((tm, d), lambda i: (i, 0)),
                  pl.BlockSpec((1, 6, d), lambda i: (i // per_b, 0, 0))],
        out_specs=pl.BlockSpec((tm, d), lambda i: (i, 0)),
        out_shape=jax.ShapeDtypeStruct((t, d), F32),
        scratch_shapes=[pltpu.VMEM((tm + HALO, CONV_CH), F32),
                        pltpu.VMEM((tm, d), BF16)],
        compiler_params=_cparams(("arbitrary",)),
        name="mix",
    )(u, u, u, u, u, z, u, u, u, u, conv_w, wc_bf, wr_bf, wo_bf, x2, mod3)


ROUTE_E0 = N_GROUPS


def _split_bf16(v):
    hi = v.astype(BF16)
    lo = (v - hi.astype(F32)).astype(BF16)
    return hi, lo


U32 = jnp.uint32
HALF = D_MODEL // 2


def _pack_pair(lo, hi):
    ulo = lax.bitcast_convert_type(lo.astype(BF16).astype(F32), U32)
    uhi = lax.bitcast_convert_type(hi.astype(BF16).astype(F32), U32)
    return (ulo >> 16) | uhi


def _unpack_pair(w):
    lo = lax.bitcast_convert_type(w << 16, F32)
    hi = lax.bitcast_convert_type(w & jnp.uint32(0xFFFF0000), F32)
    return lo, hi


def _route_kernel(x_ref, mod_ref, g_ref, wr_ref, br_ref, h_ref, ef_ref, cw_ref):
    h2 = _rms_mod(x_ref[...], g_ref[...], mod_ref[0, 3:4, :], mod_ref[0, 4:5, :])
    h_ref[...] = _pack_pair(h2[:, :HALF], h2[:, HALF:])
    hh, hl = _split_bf16(h2)
    wh, wl = _split_bf16(wr_ref[...])
    logits = (jnp.dot(hh, wh, preferred_element_type=F32)
              + (jnp.dot(hh, wl, preferred_element_type=F32)
                 + jnp.dot(hl, wh, preferred_element_type=F32))) + br_ref[...]
    lane = lax.broadcasted_iota(I32, logits.shape, 1).astype(F32)
    neg = -jnp.inf
    big = float(LANES)
    gl = jnp.where(lane < N_GROUPS, logits, neg)
    gmax = jnp.max(gl, axis=-1, keepdims=True)
    gsel = jnp.min(jnp.where(gl == gmax, lane, big), axis=-1, keepdims=True)
    gsum = jnp.sum(jnp.exp(gl - gmax), axis=-1, keepdims=True)
    g_w = 1.0 / gsum
    lo_lane = ROUTE_E0 + EXPERTS_PER_GROUP * gsel
    emask = (lane >= lo_lane) & (lane < lo_lane + EXPERTS_PER_GROUP)
    el = jnp.where(emask, logits, neg)
    m1 = jnp.max(el, axis=-1, keepdims=True)
    i1 = jnp.min(jnp.where(el == m1, lane, big), axis=-1, keepdims=True)
    el2 = jnp.where(lane == i1, neg, el)
    m2 = jnp.max(el2, axis=-1, keepdims=True)
    i2 = jnp.min(jnp.where(el2 == m2, lane, big), axis=-1, keepdims=True)
    esum = jnp.sum(jnp.exp(el - m1), axis=-1, keepdims=True)
    p1 = 1.0 / esum
    p2 = jnp.exp(m2 - m1) / esum
    tot = p1 + p2
    c1 = g_w * (p1 / tot)
    c2 = g_w * (p2 / tot)
    ef_ref[...] = jnp.where(lane == 0.0, i1 - ROUTE_E0, jnp.where(lane == 1.0, i2 - ROUTE_E0, 0.0))
    cw_ref[...] = jnp.where(lane == 0.0, c1, jnp.where(lane == 1.0, c2, 0.0))


def _route(x1, mod3, g, w_r, b_r, seq):
    t, d = x1.shape
    tm = 512
    per_b = seq // tm
    return pl.pallas_call(
        _route_kernel,
        grid=(t // tm,),
        in_specs=[pl.BlockSpec((tm, d), lambda i: (i, 0)),
                  pl.BlockSpec((1, 6, d), lambda i: (i // per_b, 0, 0)),
                  pl.BlockSpec((1, d), lambda i: (0, 0)),
                  pl.BlockSpec((d, LANES), lambda i: (0, 0)),
                  pl.BlockSpec((1, LANES), lambda i: (0, 0))],
        out_specs=[pl.BlockSpec((tm, HALF), lambda i: (i, 0)),
                   pl.BlockSpec((tm, LANES), lambda i: (i, 0)),
                   pl.BlockSpec((tm, LANES), lambda i: (i, 0))],
        out_shape=[jax.ShapeDtypeStruct((t, HALF), U32),
                   jax.ShapeDtypeStruct((t, LANES), F32),
                   jax.ShapeDtypeStruct((t, LANES), F32)],
        compiler_params=_cparams(("arbitrary",)),
        name="route",
    )(x1, mod3, g, w_r, b_r)


META_TB = 512
BLK_ROWS = 256


def _meta_kernel(ef_ref, dest_ref, blk_ref, pref_ref):
    t = ef_ref.shape[0]
    nb = t // META_TB
    lane = lax.broadcasted_iota(I32, (META_TB, LANES), 1).astype(F32)
    r_i = lax.broadcasted_iota(I32, (META_TB, META_TB), 0)
    c_i = lax.broadcasted_iota(I32, (META_TB, META_TB), 1)
    tril = jnp.where(r_i > c_i, 1.0, 0.0).astype(BF16)

    def onehots(i):
        ef = ef_ref[pl.ds(i * META_TB, META_TB), :]
        oh1 = jnp.where(lane == ef[:, 0:1], 1.0, 0.0)
        oh2 = jnp.where(lane == ef[:, 1:2], 1.0, 0.0)
        return oh1, oh2

    def pass1(i, carry):
        oh1, oh2 = onehots(i)
        cnt = oh1 + oh2
        pref = jnp.dot(tril, cnt.astype(BF16), preferred_element_type=F32) + carry
        pref_ref[pl.ds(i * META_TB, META_TB), :] = pref
        return carry + jnp.sum(cnt, axis=0, keepdims=True)

    counts = lax.fori_loop(0, nb, pass1, jnp.zeros((1, LANES), F32))
    nblk = jnp.floor((counts + (EXPERT_BLOCK - 1)) * (1.0 / EXPERT_BLOCK))
    u_r = lax.broadcasted_iota(I32, (LANES, LANES), 0)
    u_c = lax.broadcasted_iota(I32, (LANES, LANES), 1)
    upper = jnp.where(u_r <= u_c, 1.0, 0.0).astype(BF16)
    nblk8 = jnp.broadcast_to(nblk, (8, LANES))
    pend_blk = jnp.dot(nblk8.astype(BF16), upper, preferred_element_type=F32)[0:1, :]
    pstart_blk = pend_blk - nblk
    pstart = pstart_blk * float(EXPERT_BLOCK)

    def pass2(i, _):
        oh1, oh2 = onehots(i)
        base = pref_ref[pl.ds(i * META_TB, META_TB), :] + pstart
        d1 = jnp.sum(oh1 * base, axis=-1, keepdims=True)
        d2 = jnp.sum(oh2 * base, axis=-1, keepdims=True)
        dest = jnp.where(lane == 0.0, d1, jnp.where(lane == 1.0, d2, 0.0))
        dest_ref[pl.ds(i * META_TB, META_TB), :] = dest.astype(I32)
        return 0

    lax.fori_loop(0, nb, pass2, 0)

    blane = lax.broadcasted_iota(I32, (BLK_ROWS, LANES), 1).astype(F32)
    brow = lax.broadcasted_iota(I32, (BLK_ROWS, LANES), 0).astype(F32)
    emask = blane < N_EXPERTS
    owner = jnp.sum(jnp.where(emask & (pend_blk <= brow), 1.0, 0.0), axis=-1, keepdims=True)
    owner = jnp.minimum(owner, float(N_EXPERTS - 1))
    total = jnp.sum(jnp.where(emask, nblk, 0.0), axis=-1, keepdims=True)
    used = jnp.where(brow < total, 1.0, 0.0)
    mine = emask & (blane == owner)
    cnt_b = jnp.sum(jnp.where(mine, counts, 0.0), axis=-1, keepdims=True)
    ps_b = jnp.sum(jnp.where(mine, pstart_blk, 0.0), axis=-1, keepdims=True)
    nvalid = jnp.clip(cnt_b - (brow - ps_b) * float(EXPERT_BLOCK), 0.0, float(EXPERT_BLOCK))
    pad_lo = nvalid * used
    blk_ref[...] = jnp.where(blane == 0.0, owner,
                             jnp.where(blane == 1.0, used,
                                       jnp.where(blane == 2.0, pad_lo, 0.0))).astype(I32)


def _meta(ef):
    t = ef.shape[0]
    return pl.pallas_call(
        _meta_kernel,
        out_shape=[jax.ShapeDtypeStruct((t, LANES), I32),
                   jax.ShapeDtypeStruct((BLK_ROWS, LANES), I32)],
        scratch_shapes=[pltpu.VMEM((t, LANES), F32)],
        compiler_params=pltpu.CompilerParams(vmem_limit_bytes=VMEM_LIMIT),
        name="meta",
    )(ef)


DISP_TM = 256
TOP_K = 2


def _dispatch_kernel(pad_lo_ref, dest_ref, h_ref, xbuf_ref, stage, zrow, sems, zsem):
    i = pl.program_id(0)
    n = pl.num_programs(0)
    tm = h_ref.shape[0]
    n_blk = pad_lo_ref.shape[0]
    slot = i % 2

    @pl.when(i == 0)
    def _():
        zrow[...] = jnp.zeros_like(zrow)

        def pad_rows(action):
            def blk(b, _):
                lo = pad_lo_ref[b]
                base = pl.multiple_of(b * EXPERT_BLOCK, EXPERT_BLOCK)

                @pl.when(lo == 0)
                def _():
                    action(pltpu.make_async_copy(zrow, xbuf_ref.at[pl.ds(base, EXPERT_BLOCK)], zsem))

                @pl.when(lo > 0)
                def _():
                    def row(r, _):
                        action(pltpu.make_async_copy(
                            zrow.at[pl.ds(0, 1)], xbuf_ref.at[pl.ds(base + r, 1)], zsem))
                        return 0
                    lax.fori_loop(lo, EXPERT_BLOCK, row, 0)
                return 0
            lax.fori_loop(0, n_blk, blk, 0)

        pad_rows(lambda cp: cp.start())
        pad_rows(lambda cp: cp.wait())

    stage[slot] = h_ref[...]
    for j in range(tm):
        for k in range(TOP_K):
            d = dest_ref[0, 0, TOP_K * j + k]
            pltpu.make_async_copy(stage.at[slot, pl.ds(j, 1)], xbuf_ref.at[pl.ds(d, 1)],
                                  sems.at[slot]).start(priority=k)

    def wait_tile(s):
        for _ in range(TOP_K):
            pltpu.make_async_copy(stage.at[s], xbuf_ref.at[pl.ds(0, tm)], sems.at[s]).wait()

    @pl.when(i > 0)
    def _():
        wait_tile(1 - slot)

    @pl.when(i == n - 1)
    def _():
        wait_tile(slot)


def _dispatch(pad_lo, dest3, hp, n_rows):
    t, w = hp.shape
    tm = DISP_TM
    grid_spec = pltpu.PrefetchScalarGridSpec(
        num_scalar_prefetch=1,
        grid=(t // tm,),
        in_specs=[pl.BlockSpec((1, 1, TOP_K * tm), lambda i, *_: (i, 0, 0), memory_space=pltpu.SMEM),
                  pl.BlockSpec((tm, w), lambda i, *_: (i, 0))],
        out_specs=pl.BlockSpec(memory_space=pl.ANY),
        scratch_shapes=[pltpu.VMEM((2, tm, w), U32), pltpu.VMEM((EXPERT_BLOCK, w), U32),
                        pltpu.SemaphoreType.DMA((2,)), pltpu.SemaphoreType.DMA(())],
    )
    return pl.pallas_call(
        _dispatch_kernel,
        grid_spec=grid_spec,
        out_shape=jax.ShapeDtypeStruct((n_rows, w), U32),
        compiler_params=_cparams(("arbitrary",)),
        name="dispatch",
    )(pad_lo, dest3, hp)


CAST_ROWS = 256


def _moe_kernel(blk_e_ref, used_ref, first_ref, nxt_ref, x_ref, wg_hbm, wu_hbm, wd_hbm, o_ref,
                wg_st, wu_st, wd_st, wg_bf, wu_bf, wd_bf, sems):
    b = pl.program_id(0)

    def weight_copies(e):
        return (pltpu.make_async_copy(wg_hbm.at[e], wg_st, sems.at[0]),
                pltpu.make_async_copy(wu_hbm.at[e], wu_st, sems.at[1]),
                pltpu.make_async_copy(wd_hbm.at[e], wd_st, sems.at[2]))

    @pl.when(b == 0)
    def _():
        for cp in weight_copies(blk_e_ref[0]):
            cp.start(priority=1)

    @pl.when(first_ref[b] == 1)
    def _():
        for cp in weight_copies(blk_e_ref[b]):
            cp.wait()

        def cast(st, bf):
            def body(i, _):
                rows = pl.ds(pl.multiple_of(i * CAST_ROWS, CAST_ROWS), CAST_ROWS)
                bf[rows, :] = st[rows, :].astype(BF16)
                return 0
            lax.fori_loop(0, st.shape[0] // CAST_ROWS, body, 0)

        cast(wg_st, wg_bf)
        cast(wu_st, wu_bf)
        cast(wd_st, wd_bf)

        @pl.when(nxt_ref[b] >= 0)
        def _():
            for cp in weight_copies(nxt_ref[b]):
                cp.start(priority=1)

    @pl.when(used_ref[b] == 1)
    def _():
        xlo, xhi = _unpack_pair(x_ref[...])
        xb = jnp.concatenate([xlo.astype(BF16), xhi.astype(BF16)], axis=1)
        g = jnp.dot(xb, wg_bf[...], preferred_element_type=F32)
        u = jnp.dot(xb, wu_bf[...], preferred_element_type=F32)
        hmid = (_silu(g) * u).astype(BF16)
        y = jnp.dot(hmid, wd_bf[...], preferred_element_type=F32)
        o_ref[...] = _pack_pair(y[:, :HALF], y[:, HALF:])

    @pl.when(used_ref[b] == 0)
    def _():
        o_ref[...] = jnp.zeros_like(o_ref)


def _moe(blk_e, used, first, nxt, xbuf, w_gate, w_up, w_down):
    r, half = xbuf.shape
    n_blk = r // EXPERT_BLOCK
    d, de = w_gate.shape[1:]

    def xmap(b, blk_e_ref, used_ref, first_ref, nxt_ref):
        del blk_e_ref, first_ref, nxt_ref
        return (jnp.where(used_ref[b] == 1, b, 0), 0)

    grid_spec = pltpu.PrefetchScalarGridSpec(
        num_scalar_prefetch=4,
        grid=(n_blk,),
        in_specs=[pl.BlockSpec((EXPERT_BLOCK, half), xmap),
                  pl.BlockSpec(memory_space=pl.ANY),
                  pl.BlockSpec(memory_space=pl.ANY),
                  pl.BlockSpec(memory_space=pl.ANY)],
        out_specs=pl.BlockSpec((EXPERT_BLOCK, half), lambda b, *_: (b, 0)),
        scratch_shapes=[pltpu.VMEM((d, de), F32), pltpu.VMEM((d, de), F32), pltpu.VMEM((de, d), F32),
                        pltpu.VMEM((d, de), BF16), pltpu.VMEM((d, de), BF16), pltpu.VMEM((de, d), BF16),
                        pltpu.SemaphoreType.DMA((3,))],
    )
    return pl.pallas_call(
        _moe_kernel,
        grid_spec=grid_spec,
        out_shape=jax.ShapeDtypeStruct((r, half), U32),
        compiler_params=_cparams(("arbitrary",), vmem=60 * 1024 * 1024),
        name="moe",
    )(blk_e, used, first, nxt, xbuf, w_gate, w_up, w_down)


FIN_TM = 256


def _final_kernel(dest_ref, dnext_ref, x_ref, cw_ref, mod_ref, g_ref, y_hbm, o_ref, ybuf, sems):
    i = pl.program_id(0)
    n = pl.num_programs(0)
    tm = x_ref.shape[0]
    slot = i % 2

    def gather(d_ref, s):
        for j in range(tm):
            for k in range(TOP_K):
                d = d_ref[0, 0, TOP_K * j + k]
                pltpu.make_async_copy(y_hbm.at[pl.ds(d, 1)], ybuf.at[s, k, pl.ds(j, 1)],
                                      sems.at[s]).start(priority=k)

    @pl.when(i == 0)
    def _():
        gather(dest_ref, 0)

    @pl.when(i + 1 < n)
    def _():
        gather(dnext_ref, 1 - slot)

    for k in range(TOP_K):
        pltpu.make_async_copy(y_hbm.at[pl.ds(0, tm)], ybuf.at[slot, k], sems.at[slot]).wait()

    cw = cw_ref[...]
    c0, c1 = cw[:, 0:1], cw[:, 1:2]
    lo0, hi0 = _unpack_pair(ybuf[slot, 0])
    lo1, hi1 = _unpack_pair(ybuf[slot, 1])
    g2 = mod_ref[0, 5:6, :]
    xlo = x_ref[:, :HALF] + g2[:, :HALF] * (lo0 * c0 + lo1 * c1)
    xhi = x_ref[:, HALF:] + g2[:, HALF:] * (hi0 * c0 + hi1 * c1)
    ms = (jnp.sum(xlo * xlo, axis=-1, keepdims=True)
          + jnp.sum(xhi * xhi, axis=-1, keepdims=True)) * (1.0 / D_MODEL)
    r = lax.rsqrt(ms + EPS)
    o_ref[:, :HALF] = xlo * r * g_ref[:, :HALF]
    o_ref[:, HALF:] = xhi * r * g_ref[:, HALF:]


def _final(dest3, x1, cw, mod3, g, ybuf, seq):
    t, d = x1.shape
    tm = FIN_TM
    nt = t // tm
    per_b = seq // tm

    def dspec(shift):
        return pl.BlockSpec((1, 1, TOP_K * tm), lambda i: (jnp.minimum(i + shift, nt - 1), 0, 0),
                            memory_space=pltpu.SMEM)

    return pl.pallas_call(
        _final_kernel,
        grid=(nt,),
        in_specs=[dspec(0), dspec(1),
                  pl.BlockSpec((tm, d), lambda i: (i, 0)),
                  pl.BlockSpec((tm, LANES), lambda i: (i, 0)),
                  pl.BlockSpec((1, 6, d), lambda i: (i // per_b, 0, 0)),
                  pl.BlockSpec((1, d), lambda i: (0, 0)),
                  pl.BlockSpec(memory_space=pl.ANY)],
        out_specs=pl.BlockSpec((tm, d), lambda i: (i, 0)),
        out_shape=jax.ShapeDtypeStruct((t, d), F32),
        scratch_shapes=[pltpu.VMEM((2, TOP_K, tm, HALF), U32), pltpu.SemaphoreType.DMA((2,))],
        compiler_params=_cparams(("arbitrary",)),
        name="final",
    )(dest3, dest3, x1, cw, mod3, g, ybuf)


def kernel(x, c, w_ada, b_ada, norm1_g, w_in, conv_w, w_conv_out, w_ret_out, w_o, norm2_g,
           w_router_group, b_router_group, w_router_expert, b_router_expert,
           w_gate, w_up, w_down, norm_f_g):
    batch, seq, d = x.shape
    t = batch * seq
    depth = w_ada.shape[0]
    assert d == D_MODEL and w_in.shape[2] == IN_COLS and w_gate.shape[1] == N_EXPERTS
    assert depth == 1, "the final kernel fuses the last rmsnorm into the single layer"
    n_rows = t * 2 + N_EXPERTS * EXPERT_BLOCK
    n_blk = n_rows // EXPERT_BLOCK
    assert n_blk <= BLK_ROWS

    c_pad = jnp.pad(c, ((0, 8 - batch), (0, 0)))
    xs = x.reshape(t, d)
    for l in range(depth):
        mod = _ada(c_pad, w_ada[l], b_ada[l].reshape(1, -1))[:batch]
        mod3 = mod.reshape(batch, 6, d)

        u = _in_proj(xs, mod3, norm1_g[l].reshape(1, d), w_in[l], seq)
        z = _retention(u, batch, seq)
        x1 = _mix(u, z, conv_w[l], w_conv_out[l].astype(BF16), w_ret_out[l].astype(BF16),
                  w_o[l].astype(BF16), xs, mod3, seq)

        w_r = jnp.pad(jnp.concatenate([w_router_group[l], w_router_expert[l]], axis=1),
                      ((0, 0), (0, LANES - N_GROUPS - N_EXPERTS)))
        b_r = jnp.pad(jnp.concatenate([b_router_group[l], b_router_expert[l]]),
                      (0, LANES - N_GROUPS - N_EXPERTS)).reshape(1, LANES)
        h2, ef, cw = _route(x1, mod3, norm2_g[l].reshape(1, d), w_r, b_r, seq)

        dest, blk = _meta(ef)
        blk_e = blk[:n_blk, 0]
        used = blk[:n_blk, 1]
        prev_e = jnp.concatenate([jnp.full((1,), -1, I32), blk_e[:-1]])
        first = ((blk_e != prev_e) & (used == 1)).astype(I32)
        idx = jnp.arange(n_blk, dtype=I32)
        run_start = jnp.where(first == 1, idx, n_blk)
        nxt_start = lax.cummin(jnp.concatenate([run_start[1:], jnp.full((1,), n_blk, I32)]),
                               axis=0, reverse=True)
        nxt = jnp.where(nxt_start < n_blk, blk_e[jnp.minimum(nxt_start, n_blk - 1)], -1).astype(I32)

        dest2 = dest[:, :TOP_K]
        xbuf = _dispatch(blk[:n_blk, 2], dest2.reshape(t // DISP_TM, 1, TOP_K * DISP_TM), h2, n_rows)
        ybuf = _moe(blk_e, used, first, nxt, xbuf, w_gate[l], w_up[l], w_down[l])
        xs = _final(dest2.reshape(t // FIN_TM, 1, TOP_K * FIN_TM), x1, cw, mod3,
                    norm_f_g.reshape(1, d), ybuf, seq)
    return xs.reshape(batch, seq, d)
```

```python
import functools

import numpy as np
import jax
import jax.numpy as jnp
from jax import lax
from jax.experimental import pallas as pl
from jax.experimental.pallas import tpu as pltpu

F32 = jnp.float32
BF16 = jnp.bfloat16
I32 = jnp.int32

D_MODEL = 2048
CONV_CH = 1024
CONV_K = 3
RET_HEADS = 8
RET_DK = 128
RET_DV = 256
CHUNK = 128
ROPE_BASE = 10000.0
N_GROUPS = 4
EXPERTS_PER_GROUP = 8
N_EXPERTS = 32
D_EXPERT = 1024
EXPERT_BLOCK = 128
EPS = 1e-6
IN_COLS = 13312
LANES = 128
COL_BLK = 1024
CB_BLK, CC_BLK, CX_BLK, Q_BLK, K_BLK, V_BLK, RG_BLK, GA_BLK, GB_BLK = 0, 1, 2, 3, 4, 5, 7, 9, 11

VMEM_LIMIT = 56 * 1024 * 1024


def _cparams(sem, vmem=VMEM_LIMIT):
    return pltpu.CompilerParams(dimension_semantics=sem, vmem_limit_bytes=vmem)


def _sigmoid(v):
    return 1.0 / (1.0 + jnp.exp(-v))


def _silu(v):
    return v * _sigmoid(v)


def _ada_kernel(c_ref, w_ref, b_ref, o_ref):
    c = c_ref[...]
    o_ref[...] = jnp.dot(_silu(c), w_ref[...], precision=lax.Precision.HIGHEST,
                         preferred_element_type=F32) + b_ref[...]


def _ada(c_pad, w, b):
    rows, d = c_pad.shape
    n = w.shape[1]
    tn = 512
    return pl.pallas_call(
        _ada_kernel,
        grid=(n // tn,),
        in_specs=[pl.BlockSpec((rows, d), lambda j: (0, 0)),
                  pl.BlockSpec((d, tn), lambda j: (0, j)),
                  pl.BlockSpec((1, tn), lambda j: (0, j))],
        out_specs=pl.BlockSpec((rows, tn), lambda j: (0, j)),
        out_shape=jax.ShapeDtypeStruct((rows, n), F32),
        compiler_params=_cparams(("arbitrary",)),
        name="ada",
    )(c_pad, w, b)


def _rms_mod(x, g, shift, scale):
    ms = jnp.mean(x * x, axis=-1, keepdims=True)
    xn = x * lax.rsqrt(ms + EPS) * g
    return xn * (1.0 + scale) + shift


def _in_kernel(x_ref, mod_ref, g_ref, w_ref, o_ref, h_ref):
    @pl.when(pl.program_id(1) == 0)
    def _():
        h = _rms_mod(x_ref[...], g_ref[...], mod_ref[0, 0:1, :], mod_ref[0, 1:2, :])
        h_ref[...] = h.astype(BF16)

    o_ref[...] = jnp.dot(h_ref[...], w_ref[...].astype(BF16),
                         preferred_element_type=F32).astype(o_ref.dtype)


def _in_proj(x2, mod3, g, w, seq):
    t, d = x2.shape
    n = w.shape[1]
    tm, tn = 1024, 1024
    per_b = seq // tm
    return pl.pallas_call(
        _in_kernel,
        grid=(t // tm, n // tn),
        in_specs=[pl.BlockSpec((tm, d), lambda i, j: (i, 0)),
                  pl.BlockSpec((1, 6, d), lambda i, j: (i // per_b, 0, 0)),
                  pl.BlockSpec((1, d), lambda i, j: (0, 0)),
                  pl.BlockSpec((d, tn), lambda i, j: (0, j))],
        out_specs=pl.BlockSpec((tm, tn), lambda i, j: (i, j)),
        out_shape=jax.ShapeDtypeStruct((t, n), BF16),
        scratch_shapes=[pltpu.VMEM((tm, d), BF16)],
        compiler_params=_cparams(("arbitrary", "arbitrary")),
        name="in_proj",
    )(x2, mod3, g, w)


def _ret_kernel(q_ref, k_ref, v0_ref, v1_ref, rg0_ref, rg1_ref, cos_ref, sin_ref,
                idec_ref, qdec_ref, kdec_ref, o_ref, state_ref, *, chunk_decay):
    @pl.when(pl.program_id(1) == 0)
    def _():
        state_ref[...] = jnp.zeros_like(state_ref)

    cos = cos_ref[...]
    sin = sin_ref[...]

    def rot(t):
        return t * cos + pltpu.roll(t, RET_DK // 2, axis=1) * sin

    for h in range(RET_HEADS):
        v_ref, rg_ref = (v0_ref, rg0_ref) if h < RET_HEADS // 2 else (v1_ref, rg1_ref)
        vc = (h % (RET_HEADS // 2)) * RET_DV
        qh = rot(q_ref[:, h * RET_DK:(h + 1) * RET_DK].astype(F32))
        kh = rot(k_ref[:, h * RET_DK:(h + 1) * RET_DK].astype(F32)) * (RET_DK ** -0.5)
        vh = v_ref[:, vc:vc + RET_DV]
        qb = qh.astype(BF16)
        kb = kh.astype(BF16)
        scores = lax.dot_general(qb, kb, (((1,), (1,)), ((), ())),
                                 preferred_element_type=F32) * idec_ref[h]
        inner = jnp.dot(scores.astype(BF16), vh, preferred_element_type=F32)
        state = state_ref[h]
        cross = jnp.dot((qh * qdec_ref[h]).astype(BF16), state.astype(BF16),
                        preferred_element_type=F32)
        kv = lax.dot_general((kh * kdec_ref[h]).astype(BF16), vh, (((0,), (0,)), ((), ())),
                             preferred_element_type=F32)
        state_ref[h] = state * chunk_decay[h] + kv
        r = inner + cross
        r = r * lax.rsqrt(jnp.mean(r * r, axis=-1, keepdims=True) + EPS)
        gate = _silu(rg_ref[:, vc:vc + RET_DV].astype(F32))
        o_ref[:, h * RET_DV:(h + 1) * RET_DV] = (gate * r).astype(o_ref.dtype)


def _retention(u, batch, seq):
    t = u.shape[0]
    nch = seq // CHUNK
    inv = ROPE_BASE ** (-jnp.arange(0, RET_DK, 2, dtype=F32) / RET_DK)
    ang = jnp.arange(seq, dtype=F32)[:, None] * inv[None, :]
    cos, sin = jnp.cos(ang), jnp.sin(ang)
    cos_t = jnp.concatenate([cos, cos], axis=-1)
    sin_t = jnp.concatenate([-sin, sin], axis=-1)
    log_g = jnp.log1p(-(2.0 ** (-5.0 - jnp.arange(RET_HEADS, dtype=F32))))
    i = jnp.arange(CHUNK, dtype=F32)
    diff = i[:, None] - i[None, :]
    idec = jnp.where(diff >= 0, jnp.exp(log_g[:, None, None] * jnp.maximum(diff, 0.0)), 0.0)
    qdec = jnp.broadcast_to(jnp.exp(log_g[:, None] * (i + 1.0))[:, :, None],
                            (RET_HEADS, CHUNK, RET_DK))
    kdec = jnp.broadcast_to(jnp.exp(log_g[:, None] * (CHUNK - 1.0 - i))[:, :, None],
                            (RET_HEADS, CHUNK, RET_DK))
    log_g_np = np.log1p(-(2.0 ** (-5.0 - np.arange(RET_HEADS, dtype=np.float32)))).astype(np.float32)
    chunk_decay = tuple(float(np.exp(np.float32(lg * np.float32(CHUNK)))) for lg in log_g_np)

    def ublk(c):
        return pl.BlockSpec((CHUNK, COL_BLK), lambda b, n, c=c: (b * nch + n, c))

    def table():
        return pl.BlockSpec((RET_HEADS, CHUNK, RET_DK), lambda b, n: (0, 0, 0))

    return pl.pallas_call(
        functools.partial(_ret_kernel, chunk_decay=chunk_decay),
        grid=(batch, nch),
        in_specs=[ublk(Q_BLK), ublk(K_BLK), ublk(V_BLK), ublk(V_BLK + 1),
                  ublk(RG_BLK), ublk(RG_BLK + 1),
                  pl.BlockSpec((CHUNK, RET_DK), lambda b, n: (n, 0)),
                  pl.BlockSpec((CHUNK, RET_DK), lambda b, n: (n, 0)),
                  table(), table(), table()],
        out_specs=pl.BlockSpec((CHUNK, RET_HEADS * RET_DV), lambda b, n: (b * nch + n, 0)),
        out_shape=jax.ShapeDtypeStruct((t, RET_HEADS * RET_DV), BF16),
        scratch_shapes=[pltpu.VMEM((RET_HEADS, RET_DK, RET_DV), F32)],
        compiler_params=_cparams(("arbitrary", "arbitrary")),
        name="retention",
    )(u, u, u, u, u, u, cos_t, sin_t, idec, qdec, kdec)


HALO = 8


MIX_TN = 512


def _mix_kernel(cb_ref, cc_ref, cx_ref, hcc_ref, hcx_ref, z_ref, ga0_ref, ga1_ref, gb0_ref, gb1_ref,
                cw_ref, wc_ref, wr_ref, wo_ref, x_ref, mod_ref, o_ref, m_ref, mg_ref,
                *, tiles_per_seq):
    tm = cb_ref.shape[0]
    d = o_ref.shape[1]
    m = cc_ref[...].astype(F32) * cx_ref[...].astype(F32)
    halo = hcc_ref[...].astype(F32) * hcx_ref[...].astype(F32)
    first = (pl.program_id(0) % tiles_per_seq) == 0
    m_ref[0:HALO, :] = jnp.where(first, 0.0, halo)
    m_ref[HALO:HALO + tm, :] = m
    conv = (m_ref[HALO - 2:HALO - 2 + tm, :] * cw_ref[0:1, :]
            + m_ref[HALO - 1:HALO - 1 + tm, :] * cw_ref[1:2, :]
            + m * cw_ref[2:3, :])
    p = (cb_ref[...].astype(F32) * conv).astype(BF16)
    z = z_ref[...]
    for n in range(d // MIX_TN):
        cols = slice(n * MIX_TN, (n + 1) * MIX_TN)
        ga_ref, gb_ref = (ga0_ref, gb0_ref) if n < COL_BLK // MIX_TN else (ga1_ref, gb1_ref)
        gcols = slice((n * MIX_TN) % COL_BLK, (n * MIX_TN) % COL_BLK + MIX_TN)
        ya = jnp.dot(p, wc_ref[:, cols], preferred_element_type=F32)
        yb = jnp.dot(z, wr_ref[:, cols], preferred_element_type=F32)
        merged = (_sigmoid(ga_ref[:, gcols].astype(F32)) * ya
                  + _sigmoid(gb_ref[:, gcols].astype(F32)) * yb)
        mg_ref[:, cols] = merged.astype(BF16)
    mg = mg_ref[...]
    for n in range(d // MIX_TN):
        cols = slice(n * MIX_TN, (n + 1) * MIX_TN)
        acc = jnp.dot(mg, wo_ref[:, cols], preferred_element_type=F32)
        o_ref[:, cols] = x_ref[:, cols] + mod_ref[0, 2:3, cols] * acc


def _mix(u, z, conv_w, wc_bf, wr_bf, wo_bf, x2, mod3, seq):
    t, d = x2.shape
    tm = 256
    hb = tm // HALO
    per_b = seq // tm

    def ublk(c):
        return pl.BlockSpec((tm, COL_BLK), lambda i, c=c: (i, c))

    def halo(c):
        return pl.BlockSpec((HALO, COL_BLK), lambda i, c=c: (jnp.maximum(i * hb - 1, 0), c))

    def resident(shape):
        return pl.BlockSpec(shape, lambda i: (0, 0), pipeline_mode=pl.Buffered(1))

    return pl.pallas_call(
        functools.partial(_mix_kernel, tiles_per_seq=per_b),
        grid=(t // tm,),
        in_specs=[ublk(CB_BLK), ublk(CC_BLK), ublk(CX_BLK), halo(CC_BLK), halo(CX_BLK),
                  pl.BlockSpec((tm, z.shape[1]), lambda i: (i, 0)),
                  ublk(GA_BLK), ublk(GA_BLK + 1), ublk(GB_BLK), ublk(GB_BLK + 1),
                  pl.BlockSpec((CONV_K, CONV_CH), lambda i: (0, 0)),
                  resident(wc_bf.shape), resident(wr_bf.shape), resident(wo_bf.shape),
                  pl.BlockSpec((tm, d), lambda i: (i, 0)),
                  pl.BlockSpec((1, 6, d), lambda i: (i // per_b, 0, 0))],
        out_specs=pl.BlockSpec((tm, d), lambda i: (i, 0)),
        out_shape=jax.ShapeDtypeStruct((t, d), F32),
        scratch_shapes=[pltpu.VMEM((tm + HALO, CONV_CH), F32),
                        pltpu.VMEM((tm, d), BF16)],
        compiler_params=_cparams(("arbitrary",)),
        name="mix",
    )(u, u, u, u, u, z, u, u, u, u, conv_w, wc_bf, wr_bf, wo_bf, x2, mod3)


ROUTE_E0 = N_GROUPS


def _split_bf16(v):
    hi = v.astype(BF16)
    lo = (v - hi.astype(F32)).astype(BF16)
    return hi, lo


U32 = jnp.uint32
HALF = D_MODEL // 2


def _pack_pair(lo, hi):
    ulo = lax.bitcast_convert_type(lo.astype(BF16).astype(F32), U32)
    uhi = lax.bitcast_convert_type(hi.astype(BF16).astype(F32), U32)
    return (ulo >> 16) | uhi


def _unpack_pair(w):
    lo = lax.bitcast_convert_type(w << 16, F32)
    hi = lax.bitcast_convert_type(w & jnp.uint32(0xFFFF0000), F32)
    return lo, hi


ROW_SUB = HALF // LANES


def _row_slab(r):
    if isinstance(r, int):
        return pl.ds(r * ROW_SUB, ROW_SUB)
    return pl.ds(pl.multiple_of(r * ROW_SUB, ROW_SUB), ROW_SUB)


def _store_rows(ref, row0, v):
    for c in range(ROW_SUB):
        ref[pl.ds(row0 * ROW_SUB + c, v.shape[0], stride=ROW_SUB), :] = v[:, c * LANES:(c + 1) * LANES]


def _load_rows(ref, row0, n):
    return jnp.concatenate(
        [ref[pl.ds(row0 * ROW_SUB + c, n, stride=ROW_SUB), :] for c in range(ROW_SUB)], axis=1)


def _route_kernel(x_ref, mod_ref, g_ref, wr_ref, br_ref, h_ref, ef_ref, cw_ref):
    h2 = _rms_mod(x_ref[...], g_ref[...], mod_ref[0, 3:4, :], mod_ref[0, 4:5, :])
    h_ref[...] = _pack_pair(h2[:, :HALF], h2[:, HALF:])
    hh, hl = _split_bf16(h2)
    wh, wl = _split_bf16(wr_ref[...])
    logits = (jnp.dot(hh, wh, preferred_element_type=F32)
              + (jnp.dot(hh, wl, preferred_element_type=F32)
                 + jnp.dot(hl, wh, preferred_element_type=F32))) + br_ref[...]
    lane = lax.broadcasted_iota(I32, logits.shape, 1).astype(F32)
    neg = -jnp.inf
    big = float(LANES)
    gl = jnp.where(lane < N_GROUPS, logits, neg)
    gmax = jnp.max(gl, axis=-1, keepdims=True)
    gsel = jnp.min(jnp.where(gl == gmax, lane, big), axis=-1, keepdims=True)
    gsum = jnp.sum(jnp.exp(gl - gmax), axis=-1, keepdims=True)
    g_w = 1.0 / gsum
    lo_lane = ROUTE_E0 + EXPERTS_PER_GROUP * gsel
    emask = (lane >= lo_lane) & (lane < lo_lane + EXPERTS_PER_GROUP)
    el = jnp.where(emask, logits, neg)
    m1 = jnp.max(el, axis=-1, keepdims=True)
    i1 = jnp.min(jnp.where(el == m1, lane, big), axis=-1, keepdims=True)
    el2 = jnp.where(lane == i1, neg, el)
    m2 = jnp.max(el2, axis=-1, keepdims=True)
    i2 = jnp.min(jnp.where(el2 == m2, lane, big), axis=-1, keepdims=True)
    esum = jnp.sum(jnp.exp(el - m1), axis=-1, keepdims=True)
    p1 = 1.0 / esum
    p2 = jnp.exp(m2 - m1) / esum
    tot = p1 + p2
    c1 = g_w * (p1 / tot)
    c2 = g_w * (p2 / tot)
    ef_ref[...] = jnp.where(lane == 0.0, i1 - ROUTE_E0, jnp.where(lane == 1.0, i2 - ROUTE_E0, 0.0))
    cw_ref[...] = jnp.where(lane == 0.0, c1, jnp.where(lane == 1.0, c2, 0.0))


def _route(x1, mod3, g, w_r, b_r, seq):
    t, d = x1.shape
    tm = 512
    per_b = seq // tm
    return pl.pallas_call(
        _route_kernel,
        grid=(t // tm,),
        in_specs=[pl.BlockSpec((tm, d), lambda i: (i, 0)),
                  pl.BlockSpec((1, 6, d), lambda i: (i // per_b, 0, 0)),
                  pl.BlockSpec((1, d), lambda i: (0, 0)),
                  pl.BlockSpec((d, LANES), lambda i: (0, 0)),
                  pl.BlockSpec((1, LANES), lambda i: (0, 0))],
        out_specs=[pl.BlockSpec((tm, HALF), lambda i: (i, 0)),
                   pl.BlockSpec((tm, LANES), lambda i: (i, 0)),
                   pl.BlockSpec((tm, LANES), lambda i: (i, 0))],
        out_shape=[jax.ShapeDtypeStruct((t, HALF), U32),
                   jax.ShapeDtypeStruct((t, LANES), F32),
                   jax.ShapeDtypeStruct((t, LANES), F32)],
        compiler_params=_cparams(("arbitrary",)),
        name="route",
    )(x1, mod3, g, w_r, b_r)


META_TB = 512
BLK_ROWS = 256


def _meta_kernel(ef_ref, dest_ref, blk_ref, pref_ref):
    t = ef_ref.shape[0]
    nb = t // META_TB
    lane = lax.broadcasted_iota(I32, (META_TB, LANES), 1).astype(F32)
    r_i = lax.broadcasted_iota(I32, (META_TB, META_TB), 0)
    c_i = lax.broadcasted_iota(I32, (META_TB, META_TB), 1)
    tril = jnp.where(r_i > c_i, 1.0, 0.0).astype(BF16)

    def onehots(i):
        ef = ef_ref[pl.ds(i * META_TB, META_TB), :]
        oh1 = jnp.where(lane == ef[:, 0:1], 1.0, 0.0)
        oh2 = jnp.where(lane == ef[:, 1:2], 1.0, 0.0)
        return oh1, oh2

    def pass1(i, carry):
        oh1, oh2 = onehots(i)
        cnt = oh1 + oh2
        pref = jnp.dot(tril, cnt.astype(BF16), preferred_element_type=F32) + carry
        pref_ref[pl.ds(i * META_TB, META_TB), :] = pref
        return carry + jnp.sum(cnt, axis=0, keepdims=True)

    counts = lax.fori_loop(0, nb, pass1, jnp.zeros((1, LANES), F32))
    nblk = jnp.floor((counts + (EXPERT_BLOCK - 1)) * (1.0 / EXPERT_BLOCK))
    u_r = lax.broadcasted_iota(I32, (LANES, LANES), 0)
    u_c = lax.broadcasted_iota(I32, (LANES, LANES), 1)
    upper = jnp.where(u_r <= u_c, 1.0, 0.0).astype(BF16)
    nblk8 = jnp.broadcast_to(nblk, (8, LANES))
    pend_blk = jnp.dot(nblk8.astype(BF16), upper, preferred_element_type=F32)[0:1, :]
    pstart_blk = pend_blk - nblk
    pstart = pstart_blk * float(EXPERT_BLOCK)

    def pass2(i, _):
        oh1, oh2 = onehots(i)
        base = pref_ref[pl.ds(i * META_TB, META_TB), :] + pstart
        d1 = jnp.sum(oh1 * base, axis=-1, keepdims=True)
        d2 = jnp.sum(oh2 * base, axis=-1, keepdims=True)
        dest = jnp.where(lane == 0.0, d1, jnp.where(lane == 1.0, d2, 0.0))
        dest_ref[pl.ds(i * META_TB, META_TB), :] = dest.astype(I32)
        return 0

    lax.fori_loop(0, nb, pass2, 0)

    blane = lax.broadcasted_iota(I32, (BLK_ROWS, LANES), 1).astype(F32)
    brow = lax.broadcasted_iota(I32, (BLK_ROWS, LANES), 0).astype(F32)
    emask = blane < N_EXPERTS
    owner = jnp.sum(jnp.where(emask & (pend_blk <= brow), 1.0, 0.0), axis=-1, keepdims=True)
    owner = jnp.minimum(owner, float(N_EXPERTS - 1))
    total = jnp.sum(jnp.where(emask, nblk, 0.0), axis=-1, keepdims=True)
    used = jnp.where(brow < total, 1.0, 0.0)
    mine = emask & (blane == owner)
    cnt_b = jnp.sum(jnp.where(mine, counts, 0.0), axis=-1, keepdims=True)
    ps_b = jnp.sum(jnp.where(mine, pstart_blk, 0.0), axis=-1, keepdims=True)
    nvalid = jnp.clip(cnt_b - (brow - ps_b) * float(EXPERT_BLOCK), 0.0, float(EXPERT_BLOCK))
    pad_lo = nvalid * used
    blk_ref[...] = jnp.where(blane == 0.0, owner,
                             jnp.where(blane == 1.0, used,
                                       jnp.where(blane == 2.0, pad_lo, 0.0))).astype(I32)


def _meta(ef):
    t = ef.shape[0]
    return pl.pallas_call(
        _meta_kernel,
        out_shape=[jax.ShapeDtypeStruct((t, LANES), I32),
                   jax.ShapeDtypeStruct((BLK_ROWS, LANES), I32)],
        scratch_shapes=[pltpu.VMEM((t, LANES), F32)],
        compiler_params=pltpu.CompilerParams(vmem_limit_bytes=VMEM_LIMIT),
        name="meta",
    )(ef)


DISP_TM = 256
TOP_K = 2


def _dispatch_kernel(pad_lo_ref, dest_ref, h_ref, xbuf_ref, stage, zrow, sems, zsem):
    i = pl.program_id(0)
    n = pl.num_programs(0)
    tm = h_ref.shape[0]
    n_blk = pad_lo_ref.shape[0]
    slot = i % 2

    @pl.when(i == 0)
    def _():
        zrow[...] = jnp.zeros_like(zrow)

        def pad_rows(action):
            def blk(b, _):
                lo = pad_lo_ref[b]
                base = pl.multiple_of(b * EXPERT_BLOCK, EXPERT_BLOCK)

                @pl.when(lo == 0)
                def _():
                    whole = pl.ds(pl.multiple_of(base * ROW_SUB, EXPERT_BLOCK * ROW_SUB),
                                  EXPERT_BLOCK * ROW_SUB)
                    action(pltpu.make_async_copy(zrow, xbuf_ref.at[whole], zsem))

                @pl.when(lo > 0)
                def _():
                    def row(r, _):
                        action(pltpu.make_async_copy(zrow.at[_row_slab(0)],
                                                     xbuf_ref.at[_row_slab(base + r)], zsem))
                        return 0
                    lax.fori_loop(lo, EXPERT_BLOCK, row, 0)
                return 0
            lax.fori_loop(0, n_blk, blk, 0)

        pad_rows(lambda cp: cp.start())
        pad_rows(lambda cp: cp.wait())

    _store_rows(stage.at[slot], 0, h_ref[...])
    for j in range(tm):
        for k in range(TOP_K):
            d = dest_ref[0, 0, TOP_K * j + k]
            pltpu.make_async_copy(stage.at[slot, _row_slab(j)], xbuf_ref.at[_row_slab(d)],
                                  sems.at[slot]).start(priority=k)

    def wait_tile(s):
        for _ in range(TOP_K):
            pltpu.make_async_copy(stage.at[s], xbuf_ref.at[pl.ds(0, tm * ROW_SUB)], sems.at[s]).wait()

    @pl.when(i > 0)
    def _():
        wait_tile(1 - slot)

    @pl.when(i == n - 1)
    def _():
        wait_tile(slot)


def _dispatch(pad_lo, dest3, hp, n_rows):
    t, w = hp.shape
    tm = DISP_TM
    grid_spec = pltpu.PrefetchScalarGridSpec(
        num_scalar_prefetch=1,
        grid=(t // tm,),
        in_specs=[pl.BlockSpec((1, 1, TOP_K * tm), lambda i, *_: (i, 0, 0), memory_space=pltpu.SMEM),
                  pl.BlockSpec((tm, w), lambda i, *_: (i, 0))],
        out_specs=pl.BlockSpec(memory_space=pl.ANY),
        scratch_shapes=[pltpu.VMEM((2, tm * ROW_SUB, LANES), U32),
                        pltpu.VMEM((EXPERT_BLOCK * ROW_SUB, LANES), U32),
                        pltpu.SemaphoreType.DMA((2,)), pltpu.SemaphoreType.DMA(())],
    )
    return pl.pallas_call(
        _dispatch_kernel,
        grid_spec=grid_spec,
        out_shape=jax.ShapeDtypeStruct((n_rows * ROW_SUB, LANES), U32),
        compiler_params=_cparams(("arbitrary",)),
        name="dispatch",
    )(pad_lo, dest3, hp)


CAST_ROWS = 256
MOE_SUB = 4


def _moe_kernel(blk_e_ref, used_ref, first_ref, nxt_ref, x_ref, wg_hbm, wu_hbm, wd_hbm, o_ref,
                wg_st, wu_st, wd_st, wg_bf, wu_bf, wd_bf, sems):
    step = pl.program_id(0)

    def weight_copies(e):
        return (pltpu.make_async_copy(wg_hbm.at[e], wg_st, sems.at[0]),
                pltpu.make_async_copy(wu_hbm.at[e], wu_st, sems.at[1]),
                pltpu.make_async_copy(wd_hbm.at[e], wd_st, sems.at[2]))

    @pl.when(step == 0)
    def _():
        for cp in weight_copies(blk_e_ref[0]):
            cp.start(priority=1)

    def switch_expert(b):
        for cp in weight_copies(blk_e_ref[b]):
            cp.wait()

        def cast(st, bf):
            def body(i, _):
                rows = pl.ds(pl.multiple_of(i * CAST_ROWS, CAST_ROWS), CAST_ROWS)
                bf[rows, :] = st[rows, :].astype(BF16)
                return 0
            lax.fori_loop(0, st.shape[0] // CAST_ROWS, body, 0)

        cast(wg_st, wg_bf)
        cast(wu_st, wu_bf)
        cast(wd_st, wd_bf)

        @pl.when(nxt_ref[b] >= 0)
        def _():
            for cp in weight_copies(nxt_ref[b]):
                cp.start(priority=1)

    def run_block(row0):
        xlo, xhi = _unpack_pair(_load_rows(x_ref, row0, EXPERT_BLOCK))
        xb = jnp.concatenate([xlo.astype(BF16), xhi.astype(BF16)], axis=1)
        g = jnp.dot(xb, wg_bf[...], preferred_element_type=F32)
        u = jnp.dot(xb, wu_bf[...], preferred_element_type=F32)
        hmid = (_silu(g) * u).astype(BF16)
        y = jnp.dot(hmid, wd_bf[...], preferred_element_type=F32)
        _store_rows(o_ref, row0, _pack_pair(y[:, :HALF], y[:, HALF:]))

    for j in range(MOE_SUB):
        b = step * MOE_SUB + j
        row0 = j * EXPERT_BLOCK
        pl.when(first_ref[b] == 1)(functools.partial(switch_expert, b))
        pl.when(used_ref[b] == 1)(functools.partial(run_block, row0))

        @pl.when(used_ref[b] == 0)
        def _():
            o_ref[pl.ds(row0 * ROW_SUB, EXPERT_BLOCK * ROW_SUB), :] = jnp.zeros(
                (EXPERT_BLOCK * ROW_SUB, LANES), U32)


def _moe(blk_e, used, first, nxt, xbuf, w_gate, w_up, w_down):
    r = xbuf.shape[0]
    rows = MOE_SUB * EXPERT_BLOCK * ROW_SUB
    d, de = w_gate.shape[1:]

    def xmap(s, blk_e_ref, used_ref, first_ref, nxt_ref):
        del blk_e_ref, first_ref, nxt_ref
        return (jnp.where(used_ref[s * MOE_SUB] == 1, s, 0), 0)

    grid_spec = pltpu.PrefetchScalarGridSpec(
        num_scalar_prefetch=4,
        grid=(r // rows,),
        in_specs=[pl.BlockSpec((rows, LANES), xmap),
                  pl.BlockSpec(memory_space=pl.ANY),
                  pl.BlockSpec(memory_space=pl.ANY),
                  pl.BlockSpec(memory_space=pl.ANY)],
        out_specs=pl.BlockSpec((rows, LANES), lambda s, *_: (s, 0)),
        scratch_shapes=[pltpu.VMEM((d, de), F32), pltpu.VMEM((d, de), F32), pltpu.VMEM((de, d), F32),
                        pltpu.VMEM((d, de), BF16), pltpu.VMEM((d, de), BF16), pltpu.VMEM((de, d), BF16),
                        pltpu.SemaphoreType.DMA((3,))],
    )
    return pl.pallas_call(
        _moe_kernel,
        grid_spec=grid_spec,
        out_shape=jax.ShapeDtypeStruct((r, LANES), U32),
        compiler_params=_cparams(("arbitrary",), vmem=60 * 1024 * 1024),
        name="moe",
    )(blk_e, used, first, nxt, xbuf, w_gate, w_up, w_down)


FIN_TM = 256


def _final_kernel(dest_ref, dnext_ref, x_ref, cw_ref, mod_ref, g_ref, y_hbm, o_ref, ybuf, sems):
    i = pl.program_id(0)
    n = pl.num_programs(0)
    tm = x_ref.shape[0]
    slot = i % 2

    def gather(d_ref, s):
        for j in range(tm):
            for k in range(TOP_K):
                d = d_ref[0, 0, TOP_K * j + k]
                pltpu.make_async_copy(y_hbm.at[_row_slab(d)], ybuf.at[s, k, _row_slab(j)],
                                      sems.at[s]).start(priority=k)

    @pl.when(i == 0)
    def _():
        gather(dest_ref, 0)

    @pl.when(i + 1 < n)
    def _():
        gather(dnext_ref, 1 - slot)

    for k in range(TOP_K):
        pltpu.make_async_copy(y_hbm.at[pl.ds(0, tm * ROW_SUB)], ybuf.at[slot, k], sems.at[slot]).wait()

    cw = cw_ref[...]
    c0, c1 = cw[:, 0:1], cw[:, 1:2]
    lo0, hi0 = _unpack_pair(_load_rows(ybuf.at[slot, 0], 0, tm))
    lo1, hi1 = _unpack_pair(_load_rows(ybuf.at[slot, 1], 0, tm))
    g2 = mod_ref[0, 5:6, :]
    xlo = x_ref[:, :HALF] + g2[:, :HALF] * (lo0 * c0 + lo1 * c1)
    xhi = x_ref[:, HALF:] + g2[:, HALF:] * (hi0 * c0 + hi1 * c1)
    ms = (jnp.sum(xlo * xlo, axis=-1, keepdims=True)
          + jnp.sum(xhi * xhi, axis=-1, keepdims=True)) * (1.0 / D_MODEL)
    r = lax.rsqrt(ms + EPS)
    o_ref[:, :HALF] = xlo * r * g_ref[:, :HALF]
    o_ref[:, HALF:] = xhi * r * g_ref[:, HALF:]


def _final(dest3, x1, cw, mod3, g, ybuf, seq):
    t, d = x1.shape
    tm = FIN_TM
    nt = t // tm
    per_b = seq // tm

    def dspec(shift):
        return pl.BlockSpec((1, 1, TOP_K * tm), lambda i: (jnp.minimum(i + shift, nt - 1), 0, 0),
                            memory_space=pltpu.SMEM)

    return pl.pallas_call(
        _final_kernel,
        grid=(nt,),
        in_specs=[dspec(0), dspec(1),
                  pl.BlockSpec((tm, d), lambda i: (i, 0)),
                  pl.BlockSpec((tm, LANES), lambda i: (i, 0)),
                  pl.BlockSpec((1, 6, d), lambda i: (i // per_b, 0, 0)),
                  pl.BlockSpec((1, d), lambda i: (0, 0)),
                  pl.BlockSpec(memory_space=pl.ANY)],
        out_specs=pl.BlockSpec((tm, d), lambda i: (i, 0)),
        out_shape=jax.ShapeDtypeStruct((t, d), F32),
        scratch_shapes=[pltpu.VMEM((2, TOP_K, tm * ROW_SUB, LANES), U32),
                        pltpu.SemaphoreType.DMA((2,))],
        compiler_params=_cparams(("arbitrary",)),
        name="final",
    )(dest3, dest3, x1, cw, mod3, g, ybuf)


def kernel(x, c, w_ada, b_ada, norm1_g, w_in, conv_w, w_conv_out, w_ret_out, w_o, norm2_g,
           w_router_group, b_router_group, w_router_expert, b_router_expert,
           w_gate, w_up, w_down, norm_f_g):
    batch, seq, d = x.shape
    t = batch * seq
    depth = w_ada.shape[0]
    assert d == D_MODEL and w_in.shape[2] == IN_COLS and w_gate.shape[1] == N_EXPERTS
    assert depth == 1, "the final kernel fuses the last rmsnorm into the single layer"
    n_rows = t * 2 + N_EXPERTS * EXPERT_BLOCK
    n_blk = n_rows // EXPERT_BLOCK
    assert n_blk <= BLK_ROWS

    c_pad = jnp.pad(c, ((0, 8 - batch), (0, 0)))
    xs = x.reshape(t, d)
    for l in range(depth):
        mod = _ada(c_pad, w_ada[l], b_ada[l].reshape(1, -1))[:batch]
        mod3 = mod.reshape(batch, 6, d)

        u = _in_proj(xs, mod3, norm1_g[l].reshape(1, d), w_in[l], seq)
        z = _retention(u, batch, seq)
        x1 = _mix(u, z, conv_w[l], w_conv_out[l].astype(BF16), w_ret_out[l].astype(BF16),
                  w_o[l].astype(BF16), xs, mod3, seq)

        w_r = jnp.pad(jnp.concatenate([w_router_group[l], w_router_expert[l]], axis=1),
                      ((0, 0), (0, LANES - N_GROUPS - N_EXPERTS)))
        b_r = jnp.pad(jnp.concatenate([b_router_group[l], b_router_expert[l]]),
                      (0, LANES - N_GROUPS - N_EXPERTS)).reshape(1, LANES)
        h2, ef, cw = _route(x1, mod3, norm2_g[l].reshape(1, d), w_r, b_r, seq)

        dest, blk = _meta(ef)
        blk_e = blk[:n_blk, 0]
        used = blk[:n_blk, 1]
        prev_e = jnp.concatenate([jnp.full((1,), -1, I32), blk_e[:-1]])
        first = ((blk_e != prev_e) & (used == 1)).astype(I32)
        idx = jnp.arange(n_blk, dtype=I32)
        run_start = jnp.where(first == 1, idx, n_blk)
        nxt_start = lax.cummin(jnp.concatenate([run_start[1:], jnp.full((1,), n_blk, I32)]),
                               axis=0, reverse=True)
        nxt = jnp.where(nxt_start < n_blk, blk_e[jnp.minimum(nxt_start, n_blk - 1)], -1).astype(I32)

        dest2 = dest[:, :TOP_K]
        xbuf = _dispatch(blk[:n_blk, 2], dest2.reshape(t // DISP_TM, 1, TOP_K * DISP_TM), h2, n_rows)
        ybuf = _moe(blk_e, used, first, nxt, xbuf, w_gate[l], w_up[l], w_down[l])
        xs = _final(dest2.reshape(t // FIN_TM, 1, TOP_K * FIN_TM), x1, cw, mod3,
                    norm_f_g.reshape(1, d), ybuf, seq)
    return xs.reshape(batch, seq, d)
```

```python
import functools

import numpy as np
import jax
import jax.numpy as jnp
from jax import lax
from jax.experimental import pallas as pl
from jax.experimental.pallas import tpu as pltpu

F32 = jnp.float32
BF16 = jnp.bfloat16
I32 = jnp.int32

D_MODEL = 2048
CONV_CH = 1024
CONV_K = 3
RET_HEADS = 8
RET_DK = 128
RET_DV = 256
CHUNK = 128
ROPE_BASE = 10000.0
N_GROUPS = 4
EXPERTS_PER_GROUP = 8
N_EXPERTS = 32
D_EXPERT = 1024
EXPERT_BLOCK = 128
EPS = 1e-6
IN_COLS = 13312
LANES = 128
COL_BLK = 1024
CB_BLK, CC_BLK, CX_BLK, Q_BLK, K_BLK, V_BLK, RG_BLK, GA_BLK, GB_BLK = 0, 1, 2, 3, 4, 5, 7, 9, 11

VMEM_LIMIT = 56 * 1024 * 1024


def _cparams(sem, vmem=VMEM_LIMIT):
    return pltpu.CompilerParams(dimension_semantics=sem, vmem_limit_bytes=vmem)


def _sigmoid(v):
    return 1.0 / (1.0 + jnp.exp(-v))


def _silu(v):
    return v * _sigmoid(v)


def _ada_kernel(c_ref, w_ref, b_ref, o_ref):
    c = c_ref[...]
    o_ref[...] = jnp.dot(_silu(c), w_ref[...], precision=lax.Precision.HIGHEST,
                         preferred_element_type=F32) + b_ref[...]


def _ada(c_pad, w, b):
    rows, d = c_pad.shape
    n = w.shape[1]
    tn = 512
    return pl.pallas_call(
        _ada_kernel,
        grid=(n // tn,),
        in_specs=[pl.BlockSpec((rows, d), lambda j: (0, 0)),
                  pl.BlockSpec((d, tn), lambda j: (0, j)),
                  pl.BlockSpec((1, tn), lambda j: (0, j))],
        out_specs=pl.BlockSpec((rows, tn), lambda j: (0, j)),
        out_shape=jax.ShapeDtypeStruct((rows, n), F32),
        compiler_params=_cparams(("arbitrary",)),
        name="ada",
    )(c_pad, w, b)


def _rms_mod(x, g, shift, scale):
    ms = jnp.mean(x * x, axis=-1, keepdims=True)
    xn = x * lax.rsqrt(ms + EPS) * g
    return xn * (1.0 + scale) + shift


def _in_kernel(x_ref, mod_ref, g_ref, w_ref, o_ref, h_ref):
    @pl.when(pl.program_id(1) == 0)
    def _():
        h = _rms_mod(x_ref[...], g_ref[...], mod_ref[0, 0:1, :], mod_ref[0, 1:2, :])
        h_ref[...] = h.astype(BF16)

    o_ref[...] = jnp.dot(h_ref[...], w_ref[...].astype(BF16),
                         preferred_element_type=F32).astype(o_ref.dtype)


def _in_proj(x2, mod3, g, w, seq):
    t, d = x2.shape
    n = w.shape[1]
    tm, tn = 1024, 1024
    per_b = seq // tm
    return pl.pallas_call(
        _in_kernel,
        grid=(t // tm, n // tn),
        in_specs=[pl.BlockSpec((tm, d), lambda i, j: (i, 0)),
                  pl.BlockSpec((1, 6, d), lambda i, j: (i // per_b, 0, 0)),
                  pl.BlockSpec((1, d), lambda i, j: (0, 0)),
                  pl.BlockSpec((d, tn), lambda i, j: (0, j))],
        out_specs=pl.BlockSpec((tm, tn), lambda i, j: (i, j)),
        out_shape=jax.ShapeDtypeStruct((t, n), BF16),
        scratch_shapes=[pltpu.VMEM((tm, d), BF16)],
        compiler_params=_cparams(("arbitrary", "arbitrary")),
        name="in_proj",
    )(x2, mod3, g, w)


def _ret_kernel(q_ref, k_ref, v0_ref, v1_ref, rg0_ref, rg1_ref, cos_ref, sin_ref,
                idec_ref, qdec_ref, kdec_ref, o_ref, state_ref, *, chunk_decay):
    @pl.when(pl.program_id(1) == 0)
    def _():
        state_ref[...] = jnp.zeros_like(state_ref)

    cos = cos_ref[...]
    sin = sin_ref[...]

    def rot(t):
        return t * cos + pltpu.roll(t, RET_DK // 2, axis=1) * sin

    for h in range(RET_HEADS):
        v_ref, rg_ref = (v0_ref, rg0_ref) if h < RET_HEADS // 2 else (v1_ref, rg1_ref)
        vc = (h % (RET_HEADS // 2)) * RET_DV
        qh = rot(q_ref[:, h * RET_DK:(h + 1) * RET_DK].astype(F32))
        kh = rot(k_ref[:, h * RET_DK:(h + 1) * RET_DK].astype(F32)) * (RET_DK ** -0.5)
        vh = v_ref[:, vc:vc + RET_DV]
        qb = qh.astype(BF16)
        kb = kh.astype(BF16)
        scores = lax.dot_general(qb, kb, (((1,), (1,)), ((), ())),
                                 preferred_element_type=F32) * idec_ref[h]
        inner = jnp.dot(scores.astype(BF16), vh, preferred_element_type=F32)
        state = state_ref[h]
        cross = jnp.dot((qh * qdec_ref[h]).astype(BF16), state.astype(BF16),
                        preferred_element_type=F32)
        kv = lax.dot_general((kh * kdec_ref[h]).astype(BF16), vh, (((0,), (0,)), ((), ())),
                             preferred_element_type=F32)
        state_ref[h] = state * chunk_decay[h] + kv
        r = inner + cross
        r = r * lax.rsqrt(jnp.mean(r * r, axis=-1, keepdims=True) + EPS)
        gate = _silu(rg_ref[:, vc:vc + RET_DV].astype(F32))
        o_ref[:, h * RET_DV:(h + 1) * RET_DV] = (gate * r).astype(o_ref.dtype)


def _retention(u, batch, seq):
    t = u.shape[0]
    nch = seq // CHUNK
    inv = ROPE_BASE ** (-jnp.arange(0, RET_DK, 2, dtype=F32) / RET_DK)
    ang = jnp.arange(seq, dtype=F32)[:, None] * inv[None, :]
    cos, sin = jnp.cos(ang), jnp.sin(ang)
    cos_t = jnp.concatenate([cos, cos], axis=-1)
    sin_t = jnp.concatenate([-sin, sin], axis=-1)
    log_g = jnp.log1p(-(2.0 ** (-5.0 - jnp.arange(RET_HEADS, dtype=F32))))
    i = jnp.arange(CHUNK, dtype=F32)
    diff = i[:, None] - i[None, :]
    idec = jnp.where(diff >= 0, jnp.exp(log_g[:, None, None] * jnp.maximum(diff, 0.0)), 0.0)
    qdec = jnp.broadcast_to(jnp.exp(log_g[:, None] * (i + 1.0))[:, :, None],
                            (RET_HEADS, CHUNK, RET_DK))
    kdec = jnp.broadcast_to(jnp.exp(log_g[:, None] * (CHUNK - 1.0 - i))[:, :, None],
                            (RET_HEADS, CHUNK, RET_DK))
    log_g_np = np.log1p(-(2.0 ** (-5.0 - np.arange(RET_HEADS, dtype=np.float32)))).astype(np.float32)
    chunk_decay = tuple(float(np.exp(np.float32(lg * np.float32(CHUNK)))) for lg in log_g_np)

    def ublk(c):
        return pl.BlockSpec((CHUNK, COL_BLK), lambda b, n, c=c: (b * nch + n, c))

    def table():
        return pl.BlockSpec((RET_HEADS, CHUNK, RET_DK), lambda b, n: (0, 0, 0))

    return pl.pallas_call(
        functools.partial(_ret_kernel, chunk_decay=chunk_decay),
        grid=(batch, nch),
        in_specs=[ublk(Q_BLK), ublk(K_BLK), ublk(V_BLK), ublk(V_BLK + 1),
                  ublk(RG_BLK), ublk(RG_BLK + 1),
                  pl.BlockSpec((CHUNK, RET_DK), lambda b, n: (n, 0)),
                  pl.BlockSpec((CHUNK, RET_DK), lambda b, n: (n, 0)),
                  table(), table(), table()],
        out_specs=pl.BlockSpec((CHUNK, RET_HEADS * RET_DV), lambda b, n: (b * nch + n, 0)),
        out_shape=jax.ShapeDtypeStruct((t, RET_HEADS * RET_DV), BF16),
        scratch_shapes=[pltpu.VMEM((RET_HEADS, RET_DK, RET_DV), F32)],
        compiler_params=_cparams(("arbitrary", "arbitrary")),
        name="retention",
    )(u, u, u, u, u, u, cos_t, sin_t, idec, qdec, kdec)


HALO = 8


MIX_TN = 512


def _mix_kernel(cb_ref, cc_ref, cx_ref, hcc_ref, hcx_ref, z_ref, ga0_ref, ga1_ref, gb0_ref, gb1_ref,
                cw_ref, wc_ref, wr_ref, wo_ref, x_ref, mod_ref, o_ref, m_ref, mg_ref,
                *, tiles_per_seq):
    tm = cb_ref.shape[0]
    d = o_ref.shape[1]
    m = cc_ref[...].astype(F32) * cx_ref[...].astype(F32)
    halo = hcc_ref[...].astype(F32) * hcx_ref[...].astype(F32)
    first = (pl.program_id(0) % tiles_per_seq) == 0
    m_ref[0:HALO, :] = jnp.where(first, 0.0, halo)
    m_ref[HALO:HALO + tm, :] = m
    conv = (m_ref[HALO - 2:HALO - 2 + tm, :] * cw_ref[0:1, :]
            + m_ref[HALO - 1:HALO - 1 + tm, :] * cw_ref[1:2, :]
            + m * cw_ref[2:3, :])
    p = (cb_ref[...].astype(F32) * conv).astype(BF16)
    z = z_ref[...]
    for n in range(d // MIX_TN):
        cols = slice(n * MIX_TN, (n + 1) * MIX_TN)
        ga_ref, gb_ref = (ga0_ref, gb0_ref) if n < COL_BLK // MIX_TN else (ga1_ref, gb1_ref)
        gcols = slice((n * MIX_TN) % COL_BLK, (n * MIX_TN) % COL_BLK + MIX_TN)
        ya = jnp.dot(p, wc_ref[:, cols], preferred_element_type=F32)
        yb = jnp.dot(z, wr_ref[:, cols], preferred_element_type=F32)
        merged = (_sigmoid(ga_ref[:, gcols].astype(F32)) * ya
                  + _sigmoid(gb_ref[:, gcols].astype(F32)) * yb)
        mg_ref[:, cols] = merged.astype(BF16)
    mg = mg_ref[...]
    for n in range(d // MIX_TN):
        cols = slice(n * MIX_TN, (n + 1) * MIX_TN)
        acc = jnp.dot(mg, wo_ref[:, cols], preferred_element_type=F32)
        o_ref[:, cols] = x_ref[:, cols] + mod_ref[0, 2:3, cols] * acc


def _mix(u, z, conv_w, wc_bf, wr_bf, wo_bf, x2, mod3, seq):
    t, d = x2.shape
    tm = 256
    hb = tm // HALO
    per_b = seq // tm

    def ublk(c):
        return pl.BlockSpec((tm, COL_BLK), lambda i, c=c: (i, c))

    def halo(c):
        return pl.BlockSpec((HALO, COL_BLK), lambda i, c=c: (jnp.maximum(i * hb - 1, 0), c))

    def resident(shape):
        return pl.BlockSpec(shape, lambda i: (0, 0), pipeline_mode=pl.Buffered(1))

    return pl.pallas_call(
        functools.partial(_mix_kernel, tiles_per_seq=per_b),
        grid=(t // tm,),
        in_specs=[ublk(CB_BLK), ublk(CC_BLK), ublk(CX_BLK), halo(CC_BLK), halo(CX_BLK),
                  pl.BlockSpec((tm, z.shape[1]), lambda i: (i, 0)),
                  ublk(GA_BLK), ublk(GA_BLK + 1), ublk(GB_BLK), ublk(GB_BLK + 1),
                  pl.BlockSpec((CONV_K, CONV_CH), lambda i: (0, 0)),
                  resident(wc_bf.shape), resident(wr_bf.shape), resident(wo_bf.shape),
                  pl.BlockSpec((tm, d), lambda i: (i, 0)),
                  pl.BlockSpec((1, 6, d), lambda i: (i // per_b, 0, 0))],
        out_specs=pl.BlockSpec((tm, d), lambda i: (i, 0)),
        out_shape=jax.ShapeDtypeStruct((t, d), F32),
        scratch_shapes=[pltpu.VMEM((tm + HALO, CONV_CH), F32),
                        pltpu.VMEM((tm, d), BF16)],
        compiler_params=_cparams(("arbitrary",)),
        name="mix",
    )(u, u, u, u, u, z, u, u, u, u, conv_w, wc_bf, wr_bf, wo_bf, x2, mod3)


ROUTE_E0 = N_GROUPS


def _split_bf16(v):
    hi = v.astype(BF16)
    lo = (v - hi.astype(F32)).astype(BF16)
    return hi, lo


U32 = jnp.uint32
HALF = D_MODEL // 2


def _pack_pair(lo, hi):
    ulo = lax.bitcast_convert_type(lo.astype(BF16).astype(F32), U32)
    uhi = lax.bitcast_convert_type(hi.astype(BF16).astype(F32), U32)
    return (ulo >> 16) | uhi


def _unpack_pair(w):
    lo = lax.bitcast_convert_type(w << 16, F32)
    hi = lax.bitcast_convert_type(w & jnp.uint32(0xFFFF0000), F32)
    return lo, hi


ROW_SUB = HALF // LANES


def _row_slab(r):
    if isinstance(r, int):
        return pl.ds(r * ROW_SUB, ROW_SUB)
    return pl.ds(pl.multiple_of(r * ROW_SUB, ROW_SUB), ROW_SUB)


def _store_rows(ref, row0, v):
    for c in range(ROW_SUB):
        ref[pl.ds(row0 * ROW_SUB + c, v.shape[0], stride=ROW_SUB), :] = v[:, c * LANES:(c + 1) * LANES]


def _load_rows(ref, row0, n):
    return jnp.concatenate(
        [ref[pl.ds(row0 * ROW_SUB + c, n, stride=ROW_SUB), :] for c in range(ROW_SUB)], axis=1)


def _route_kernel(x_ref, mod_ref, g_ref, wr_ref, br_ref, h_ref, ef_ref, cw_ref):
    h2 = _rms_mod(x_ref[...], g_ref[...], mod_ref[0, 3:4, :], mod_ref[0, 4:5, :])
    h_ref[...] = _pack_pair(h2[:, :HALF], h2[:, HALF:])
    hh, hl = _split_bf16(h2)
    wh, wl = _split_bf16(wr_ref[...])
    logits = (jnp.dot(hh, wh, preferred_element_type=F32)
              + (jnp.dot(hh, wl, preferred_element_type=F32)
                 + jnp.dot(hl, wh, preferred_element_type=F32))) + br_ref[...]
    lane = lax.broadcasted_iota(I32, logits.shape, 1).astype(F32)
    neg = -jnp.inf
    big = float(LANES)
    gl = jnp.where(lane < N_GROUPS, logits, neg)
    gmax = jnp.max(gl, axis=-1, keepdims=True)
    gsel = jnp.min(jnp.where(gl == gmax, lane, big), axis=-1, keepdims=True)
    gsum = jnp.sum(jnp.exp(gl - gmax), axis=-1, keepdims=True)
    g_w = 1.0 / gsum
    lo_lane = ROUTE_E0 + EXPERTS_PER_GROUP * gsel
    emask = (lane >= lo_lane) & (lane < lo_lane + EXPERTS_PER_GROUP)
    el = jnp.where(emask, logits, neg)
    m1 = jnp.max(el, axis=-1, keepdims=True)
    i1 = jnp.min(jnp.where(el == m1, lane, big), axis=-1, keepdims=True)
    el2 = jnp.where(lane == i1, neg, el)
    m2 = jnp.max(el2, axis=-1, keepdims=True)
    i2 = jnp.min(jnp.where(el2 == m2, lane, big), axis=-1, keepdims=True)
    esum = jnp.sum(jnp.exp(el - m1), axis=-1, keepdims=True)
    p1 = 1.0 / esum
    p2 = jnp.exp(m2 - m1) / esum
    tot = p1 + p2
    c1 = g_w * (p1 / tot)
    c2 = g_w * (p2 / tot)
    ef_ref[...] = jnp.where(lane == 0.0, i1 - ROUTE_E0, jnp.where(lane == 1.0, i2 - ROUTE_E0, 0.0))
    cw_ref[...] = jnp.where(lane == 0.0, c1, jnp.where(lane == 1.0, c2, 0.0))


def _route(x1, mod3, g, w_r, b_r, seq):
    t, d = x1.shape
    tm = 512
    per_b = seq // tm
    return pl.pallas_call(
        _route_kernel,
        grid=(t // tm,),
        in_specs=[pl.BlockSpec((tm, d), lambda i: (i, 0)),
                  pl.BlockSpec((1, 6, d), lambda i: (i // per_b, 0, 0)),
                  pl.BlockSpec((1, d), lambda i: (0, 0)),
                  pl.BlockSpec((d, LANES), lambda i: (0, 0)),
                  pl.BlockSpec((1, LANES), lambda i: (0, 0))],
        out_specs=[pl.BlockSpec((tm, HALF), lambda i: (i, 0)),
                   pl.BlockSpec((tm, LANES), lambda i: (i, 0)),
                   pl.BlockSpec((tm, LANES), lambda i: (i, 0))],
        out_shape=[jax.ShapeDtypeStruct((t, HALF), U32),
                   jax.ShapeDtypeStruct((t, LANES), F32),
                   jax.ShapeDtypeStruct((t, LANES), F32)],
        compiler_params=_cparams(("arbitrary",)),
        name="route",
    )(x1, mod3, g, w_r, b_r)


META_TB = 512
BLK_ROWS = 256


def _meta_kernel(ef_ref, dest_ref, blk_ref, pref_ref):
    t = ef_ref.shape[0]
    nb = t // META_TB
    lane = lax.broadcasted_iota(I32, (META_TB, LANES), 1).astype(F32)
    r_i = lax.broadcasted_iota(I32, (META_TB, META_TB), 0)
    c_i = lax.broadcasted_iota(I32, (META_TB, META_TB), 1)
    tril = jnp.where(r_i > c_i, 1.0, 0.0).astype(BF16)

    def onehots(i):
        ef = ef_ref[pl.ds(i * META_TB, META_TB), :]
        oh1 = jnp.where(lane == ef[:, 0:1], 1.0, 0.0)
        oh2 = jnp.where(lane == ef[:, 1:2], 1.0, 0.0)
        return oh1, oh2

    def pass1(i, carry):
        oh1, oh2 = onehots(i)
        cnt = oh1 + oh2
        pref = jnp.dot(tril, cnt.astype(BF16), preferred_element_type=F32) + carry
        pref_ref[pl.ds(i * META_TB, META_TB), :] = pref
        return carry + jnp.sum(cnt, axis=0, keepdims=True)

    counts = lax.fori_loop(0, nb, pass1, jnp.zeros((1, LANES), F32))
    nblk = jnp.floor((counts + (EXPERT_BLOCK - 1)) * (1.0 / EXPERT_BLOCK))
    u_r = lax.broadcasted_iota(I32, (LANES, LANES), 0)
    u_c = lax.broadcasted_iota(I32, (LANES, LANES), 1)
    upper = jnp.where(u_r <= u_c, 1.0, 0.0).astype(BF16)
    nblk8 = jnp.broadcast_to(nblk, (8, LANES))
    pend_blk = jnp.dot(nblk8.astype(BF16), upper, preferred_element_type=F32)[0:1, :]
    pstart_blk = pend_blk - nblk
    pstart = pstart_blk * float(EXPERT_BLOCK)

    def pass2(i, _):
        oh1, oh2 = onehots(i)
        base = pref_ref[pl.ds(i * META_TB, META_TB), :] + pstart
        d1 = jnp.sum(oh1 * base, axis=-1, keepdims=True)
        d2 = jnp.sum(oh2 * base, axis=-1, keepdims=True)
        dest = jnp.where(lane == 0.0, d1, jnp.where(lane == 1.0, d2, 0.0))
        dest_ref[pl.ds(i * META_TB, META_TB), :] = dest.astype(I32)
        return 0

    lax.fori_loop(0, nb, pass2, 0)

    blane = lax.broadcasted_iota(I32, (BLK_ROWS, LANES), 1).astype(F32)
    brow = lax.broadcasted_iota(I32, (BLK_ROWS, LANES), 0).astype(F32)
    emask = blane < N_EXPERTS
    owner = jnp.sum(jnp.where(emask & (pend_blk <= brow), 1.0, 0.0), axis=-1, keepdims=True)
    owner = jnp.minimum(owner, float(N_EXPERTS - 1))
    total = jnp.sum(jnp.where(emask, nblk, 0.0), axis=-1, keepdims=True)
    used = jnp.where(brow < total, 1.0, 0.0)
    mine = emask & (blane == owner)
    cnt_b = jnp.sum(jnp.where(mine, counts, 0.0), axis=-1, keepdims=True)
    ps_b = jnp.sum(jnp.where(mine, pstart_blk, 0.0), axis=-1, keepdims=True)
    nvalid = jnp.clip(cnt_b - (brow - ps_b) * float(EXPERT_BLOCK), 0.0, float(EXPERT_BLOCK))
    pad_lo = nvalid * used
    blk_ref[...] = jnp.where(blane == 0.0, owner,
                             jnp.where(blane == 1.0, used,
                                       jnp.where(blane == 2.0, pad_lo, 0.0))).astype(I32)


def _meta(ef):
    t = ef.shape[0]
    return pl.pallas_call(
        _meta_kernel,
        out_shape=[jax.ShapeDtypeStruct((t, LANES), I32),
                   jax.ShapeDtypeStruct((BLK_ROWS, LANES), I32)],
        scratch_shapes=[pltpu.VMEM((t, LANES), F32)],
        compiler_params=pltpu.CompilerParams(vmem_limit_bytes=VMEM_LIMIT),
        name="meta",
    )(ef)


DISP_TM = 256
TOP_K = 2


def _dispatch_kernel(pad_lo_ref, dest_ref, h_ref, xbuf_ref, stage, zrow, sems, zsem):
    i = pl.program_id(0)
    n = pl.num_programs(0)
    tm = h_ref.shape[0]
    n_blk = pad_lo_ref.shape[0]
    slot = i % 2

    @pl.when(i == 0)
    def _():
        zrow[...] = jnp.zeros_like(zrow)

        def pad_rows(action):
            def blk(b, _):
                lo = pad_lo_ref[b]
                base = pl.multiple_of(b * EXPERT_BLOCK, EXPERT_BLOCK)

                @pl.when(lo == 0)
                def _():
                    whole = pl.ds(pl.multiple_of(base * ROW_SUB, EXPERT_BLOCK * ROW_SUB),
                                  EXPERT_BLOCK * ROW_SUB)
                    action(pltpu.make_async_copy(zrow, xbuf_ref.at[whole], zsem))

                @pl.when(lo > 0)
                def _():
                    def row(r, _):
                        action(pltpu.make_async_copy(zrow.at[_row_slab(0)],
                                                     xbuf_ref.at[_row_slab(base + r)], zsem))
                        return 0
                    lax.fori_loop(lo, EXPERT_BLOCK, row, 0)
                return 0
            lax.fori_loop(0, n_blk, blk, 0)

        pad_rows(lambda cp: cp.start())
        pad_rows(lambda cp: cp.wait())

    _store_rows(stage.at[slot], 0, h_ref[...])
    for j in range(tm):
        for k in range(TOP_K):
            d = dest_ref[0, 0, TOP_K * j + k]
            pltpu.make_async_copy(stage.at[slot, _row_slab(j)], xbuf_ref.at[_row_slab(d)],
                                  sems.at[slot]).start(priority=k)

    def wait_tile(s):
        for _ in range(TOP_K):
            pltpu.make_async_copy(stage.at[s], xbuf_ref.at[pl.ds(0, tm * ROW_SUB)], sems.at[s]).wait()

    @pl.when(i > 0)
    def _():
        wait_tile(1 - slot)

    @pl.when(i == n - 1)
    def _():
        wait_tile(slot)


def _dispatch(pad_lo, dest3, hp, n_rows):
    t, w = hp.shape
    tm = DISP_TM
    grid_spec = pltpu.PrefetchScalarGridSpec(
        num_scalar_prefetch=1,
        grid=(t // tm,),
        in_specs=[pl.BlockSpec((1, 1, TOP_K * tm), lambda i, *_: (i, 0, 0), memory_space=pltpu.SMEM),
                  pl.BlockSpec((tm, w), lambda i, *_: (i, 0))],
        out_specs=pl.BlockSpec(memory_space=pl.ANY),
        scratch_shapes=[pltpu.VMEM((2, tm * ROW_SUB, LANES), U32),
                        pltpu.VMEM((EXPERT_BLOCK * ROW_SUB, LANES), U32),
                        pltpu.SemaphoreType.DMA((2,)), pltpu.SemaphoreType.DMA(())],
    )
    return pl.pallas_call(
        _dispatch_kernel,
        grid_spec=grid_spec,
        out_shape=jax.ShapeDtypeStruct((n_rows * ROW_SUB, LANES), U32),
        compiler_params=_cparams(("arbitrary",)),
        name="dispatch",
    )(pad_lo, dest3, hp)


MOE_SUB = 2
W_CHUNK = 1024
N_CHUNKS = 6
N_STAGE = 3
W_RING = 3
CAST_ROWS = 256
LOOKAHEAD_PER_BLOCK = 3


def _moe_kernel(run_e_ref, run_id_ref, used_ref, nruns_ref, x_ref, wg_hbm, wu_hbm, wd_hbm, o_ref,
                st, wg_bf, wu_bf, wd_bf, sems, qnext):
    step = pl.program_id(0)
    total = nruns_ref[0] * N_CHUNKS

    def parts(q):
        run = lax.div(q, N_CHUNKS)
        c = lax.rem(q, N_CHUNKS)
        off = pl.multiple_of(lax.rem(c, 2) * W_CHUNK, W_CHUNK)
        return run, c, off, lax.rem(q, N_STAGE)

    def start_chunk(q):
        run, c, off, stage = parts(q)
        e = run_e_ref[run]
        dst, sem = st.at[stage], sems.at[stage]

        @pl.when(c < 2)
        def _():
            pltpu.make_async_copy(wg_hbm.at[e, pl.ds(off, W_CHUNK), :], dst, sem).start(priority=1)

        @pl.when((c >= 2) & (c < 4))
        def _():
            pltpu.make_async_copy(wu_hbm.at[e, pl.ds(off, W_CHUNK), :], dst, sem).start(priority=1)

        for half in range(2):
            @pl.when(c == 4 + half)
            def _():
                cols = pl.ds(half * W_CHUNK, W_CHUNK)
                pltpu.make_async_copy(wd_hbm.at[e, :, cols], dst, sem).start(priority=1)

    def finish_chunk(q):
        run, c, off, stage = parts(q)
        ring = lax.rem(run, W_RING)
        pltpu.make_async_copy(wg_hbm.at[0, pl.ds(0, W_CHUNK), :], st.at[stage], sems.at[stage]).wait()

        def cast_rows(store):
            def body(i, _):
                r0 = pl.multiple_of(i * CAST_ROWS, CAST_ROWS)
                store(r0, st[stage, pl.ds(r0, CAST_ROWS), :].astype(BF16))
                return 0
            lax.fori_loop(0, W_CHUNK // CAST_ROWS, body, 0)

        def row_store(bf):
            def store(r0, v):
                bf[ring, pl.ds(pl.multiple_of(off + r0, CAST_ROWS), CAST_ROWS), :] = v
            return store

        pl.when(c < 2)(lambda: cast_rows(row_store(wg_bf)))
        pl.when((c >= 2) & (c < 4))(lambda: cast_rows(row_store(wu_bf)))
        for half in range(2):
            def store(r0, v, half=half):
                wd_bf[ring, pl.ds(r0, CAST_ROWS), half * W_CHUNK:(half + 1) * W_CHUNK] = v
            pl.when(c == 4 + half)(functools.partial(cast_rows, store))

    def advance(_, carry):
        q = qnext[0]
        finish_chunk(q)

        @pl.when(q + N_STAGE < total)
        def _():
            start_chunk(q + N_STAGE)

        qnext[0] = q + 1
        return carry

    @pl.when(step == 0)
    def _():
        qnext[0] = 0
        for q in range(N_STAGE):
            start_chunk(jnp.int32(q))

    def run_block(b, row0):
        run = run_id_ref[b]
        lax.fori_loop(0, jnp.maximum((run + 1) * N_CHUNKS - qnext[0], 0), advance, 0)
        limit = jnp.minimum((run + W_RING) * N_CHUNKS, total)
        lax.fori_loop(0, jnp.clip(limit - qnext[0], 0, LOOKAHEAD_PER_BLOCK), advance, 0)

        ring = lax.rem(run, W_RING)
        xlo, xhi = _unpack_pair(_load_rows(x_ref, row0, EXPERT_BLOCK))
        xb = jnp.concatenate([xlo.astype(BF16), xhi.astype(BF16)], axis=1)
        g = jnp.dot(xb, wg_bf[ring], preferred_element_type=F32)
        u = jnp.dot(xb, wu_bf[ring], preferred_element_type=F32)
        hmid = (_silu(g) * u).astype(BF16)
        y = jnp.dot(hmid, wd_bf[ring], preferred_element_type=F32)
        _store_rows(o_ref, row0, _pack_pair(y[:, :HALF], y[:, HALF:]))

    for j in range(MOE_SUB):
        b = step * MOE_SUB + j
        row0 = j * EXPERT_BLOCK
        pl.when(used_ref[b] == 1)(functools.partial(run_block, b, row0))

        @pl.when(used_ref[b] == 0)
        def _():
            o_ref[pl.ds(row0 * ROW_SUB, EXPERT_BLOCK * ROW_SUB), :] = jnp.zeros(
                (EXPERT_BLOCK * ROW_SUB, LANES), U32)


def _moe(run_e, run_id, used, n_runs, xbuf, w_gate, w_up, w_down):
    r = xbuf.shape[0]
    rows = MOE_SUB * EXPERT_BLOCK * ROW_SUB
    d, de = w_gate.shape[1:]
    assert d == 2 * W_CHUNK and de == W_CHUNK

    def xmap(s, run_e_ref, run_id_ref, used_ref, nruns_ref):
        del run_e_ref, run_id_ref, nruns_ref
        return (jnp.where(used_ref[s * MOE_SUB] == 1, s, 0), 0)

    grid_spec = pltpu.PrefetchScalarGridSpec(
        num_scalar_prefetch=4,
        grid=(r // rows,),
        in_specs=[pl.BlockSpec((rows, LANES), xmap),
                  pl.BlockSpec(memory_space=pl.ANY),
                  pl.BlockSpec(memory_space=pl.ANY),
                  pl.BlockSpec(memory_space=pl.ANY)],
        out_specs=pl.BlockSpec((rows, LANES), lambda s, *_: (s, 0)),
        scratch_shapes=[pltpu.VMEM((N_STAGE, W_CHUNK, W_CHUNK), F32),
                        pltpu.VMEM((W_RING, d, de), BF16), pltpu.VMEM((W_RING, d, de), BF16),
                        pltpu.VMEM((W_RING, de, d), BF16),
                        pltpu.SemaphoreType.DMA((N_STAGE,)), pltpu.SMEM((1,), I32)],
    )
    return pl.pallas_call(
        _moe_kernel,
        grid_spec=grid_spec,
        out_shape=jax.ShapeDtypeStruct((r, LANES), U32),
        compiler_params=_cparams(("arbitrary",), vmem=60 * 1024 * 1024),
        name="moe",
    )(run_e, run_id, used, n_runs, xbuf, w_gate, w_up, w_down)


FIN_TM = 256


def _final_kernel(dest_ref, dnext_ref, x_ref, cw_ref, mod_ref, g_ref, y_hbm, o_ref, ybuf, sems):
    i = pl.program_id(0)
    n = pl.num_programs(0)
    tm = x_ref.shape[0]
    slot = i % 2

    def gather(d_ref, s):
        for j in range(tm):
            for k in range(TOP_K):
                d = d_ref[0, 0, TOP_K * j + k]
                pltpu.make_async_copy(y_hbm.at[_row_slab(d)], ybuf.at[s, k, _row_slab(j)],
                                      sems.at[s]).start(priority=k)

    @pl.when(i == 0)
    def _():
        gather(dest_ref, 0)

    @pl.when(i + 1 < n)
    def _():
        gather(dnext_ref, 1 - slot)

    for k in range(TOP_K):
        pltpu.make_async_copy(y_hbm.at[pl.ds(0, tm * ROW_SUB)], ybuf.at[slot, k], sems.at[slot]).wait()

    cw = cw_ref[...]
    c0, c1 = cw[:, 0:1], cw[:, 1:2]
    lo0, hi0 = _unpack_pair(_load_rows(ybuf.at[slot, 0], 0, tm))
    lo1, hi1 = _unpack_pair(_load_rows(ybuf.at[slot, 1], 0, tm))
    g2 = mod_ref[0, 5:6, :]
    xlo = x_ref[:, :HALF] + g2[:, :HALF] * (lo0 * c0 + lo1 * c1)
    xhi = x_ref[:, HALF:] + g2[:, HALF:] * (hi0 * c0 + hi1 * c1)
    ms = (jnp.sum(xlo * xlo, axis=-1, keepdims=True)
          + jnp.sum(xhi * xhi, axis=-1, keepdims=True)) * (1.0 / D_MODEL)
    r = lax.rsqrt(ms + EPS)
    o_ref[:, :HALF] = xlo * r * g_ref[:, :HALF]
    o_ref[:, HALF:] = xhi * r * g_ref[:, HALF:]


def _final(dest3, x1, cw, mod3, g, ybuf, seq):
    t, d = x1.shape
    tm = FIN_TM
    nt = t // tm
    per_b = seq // tm

    def dspec(shift):
        return pl.BlockSpec((1, 1, TOP_K * tm), lambda i: (jnp.minimum(i + shift, nt - 1), 0, 0),
                            memory_space=pltpu.SMEM)

    return pl.pallas_call(
        _final_kernel,
        grid=(nt,),
        in_specs=[dspec(0), dspec(1),
                  pl.BlockSpec((tm, d), lambda i: (i, 0)),
                  pl.BlockSpec((tm, LANES), lambda i: (i, 0)),
                  pl.BlockSpec((1, 6, d), lambda i: (i // per_b, 0, 0)),
                  pl.BlockSpec((1, d), lambda i: (0, 0)),
                  pl.BlockSpec(memory_space=pl.ANY)],
        out_specs=pl.BlockSpec((tm, d), lambda i: (i, 0)),
        out_shape=jax.ShapeDtypeStruct((t, d), F32),
        scratch_shapes=[pltpu.VMEM((2, TOP_K, tm * ROW_SUB, LANES), U32),
                        pltpu.SemaphoreType.DMA((2,))],
        compiler_params=_cparams(("arbitrary",)),
        name="final",
    )(dest3, dest3, x1, cw, mod3, g, ybuf)


def kernel(x, c, w_ada, b_ada, norm1_g, w_in, conv_w, w_conv_out, w_ret_out, w_o, norm2_g,
           w_router_group, b_router_group, w_router_expert, b_router_expert,
           w_gate, w_up, w_down, norm_f_g):
    batch, seq, d = x.shape
    t = batch * seq
    depth = w_ada.shape[0]
    assert d == D_MODEL and w_in.shape[2] == IN_COLS and w_gate.shape[1] == N_EXPERTS
    assert depth == 1, "the final kernel fuses the last rmsnorm into the single layer"
    n_rows = t * 2 + N_EXPERTS * EXPERT_BLOCK
    n_blk = n_rows // EXPERT_BLOCK
    assert n_blk <= BLK_ROWS

    c_pad = jnp.pad(c, ((0, 8 - batch), (0, 0)))
    xs = x.reshape(t, d)
    for l in range(depth):
        mod = _ada(c_pad, w_ada[l], b_ada[l].reshape(1, -1))[:batch]
        mod3 = mod.reshape(batch, 6, d)

        u = _in_proj(xs, mod3, norm1_g[l].reshape(1, d), w_in[l], seq)
        z = _retention(u, batch, seq)
        x1 = _mix(u, z, conv_w[l], w_conv_out[l].astype(BF16), w_ret_out[l].astype(BF16),
                  w_o[l].astype(BF16), xs, mod3, seq)

        w_r = jnp.pad(jnp.concatenate([w_router_group[l], w_router_expert[l]], axis=1),
                      ((0, 0), (0, LANES - N_GROUPS - N_EXPERTS)))
        b_r = jnp.pad(jnp.concatenate([b_router_group[l], b_router_expert[l]]),
                      (0, LANES - N_GROUPS - N_EXPERTS)).reshape(1, LANES)
        h2, ef, cw = _route(x1, mod3, norm2_g[l].reshape(1, d), w_r, b_r, seq)

        dest, blk = _meta(ef)
        blk_e = blk[:n_blk, 0]
        used = blk[:n_blk, 1]
        prev_e = jnp.concatenate([jnp.full((1,), -1, I32), blk_e[:-1]])
        first = ((blk_e != prev_e) & (used == 1)).astype(I32)
        run_id = jnp.maximum(jnp.cumsum(first) - 1, 0).astype(I32)
        n_runs = jnp.sum(first).astype(I32).reshape(1)
        present = jnp.any((blk_e[None, :] == jnp.arange(N_EXPERTS, dtype=I32)[:, None])
                          & (used[None, :] == 1), axis=1)
        eids = jnp.arange(N_EXPERTS, dtype=I32)
        rank = jnp.cumsum(present.astype(I32)) - 1
        run_e = jnp.sum(jnp.where(present[None, :] & (rank[None, :] == eids[:, None]),
                                  eids[None, :], 0), axis=1).astype(I32)

        dest2 = dest[:, :TOP_K]
        xbuf = _dispatch(blk[:n_blk, 2], dest2.reshape(t // DISP_TM, 1, TOP_K * DISP_TM), h2, n_rows)
        ybuf = _moe(run_e, run_id, used, n_runs, xbuf, w_gate[l], w_up[l], w_down[l])
        xs = _final(dest2.reshape(t // FIN_TM, 1, TOP_K * FIN_TM), x1, cw, mod3,
                    norm_f_g.reshape(1, d), ybuf, seq)
    return xs.reshape(batch, seq, d)
```

```python
import functools

import numpy as np
import jax
import jax.numpy as jnp
from jax import lax
from jax.experimental import pallas as pl
from jax.experimental.pallas import tpu as pltpu

F32 = jnp.float32
BF16 = jnp.bfloat16
I32 = jnp.int32

D_MODEL = 2048
CONV_CH = 1024
CONV_K = 3
RET_HEADS = 8
RET_DK = 128
RET_DV = 256
CHUNK = 128
ROPE_BASE = 10000.0
N_GROUPS = 4
EXPERTS_PER_GROUP = 8
N_EXPERTS = 32
D_EXPERT = 1024
EXPERT_BLOCK = 128
EPS = 1e-6
IN_COLS = 13312
LANES = 128
COL_BLK = 1024
CB_BLK, CC_BLK, CX_BLK, Q_BLK, K_BLK, V_BLK, RG_BLK, GA_BLK, GB_BLK = 0, 1, 2, 3, 4, 5, 7, 9, 11

VMEM_LIMIT = 56 * 1024 * 1024


def _cparams(sem, vmem=VMEM_LIMIT):
    return pltpu.CompilerParams(dimension_semantics=sem, vmem_limit_bytes=vmem)


def _sigmoid(v):
    return 1.0 / (1.0 + jnp.exp(-v))


def _silu(v):
    return v * _sigmoid(v)


def _ada_kernel(c_ref, w_ref, b_ref, o_ref):
    c = c_ref[...]
    o_ref[...] = jnp.dot(_silu(c), w_ref[...], precision=lax.Precision.HIGHEST,
                         preferred_element_type=F32) + b_ref[...]


def _ada(c_pad, w, b):
    rows, d = c_pad.shape
    n = w.shape[1]
    tn = 512
    return pl.pallas_call(
        _ada_kernel,
        grid=(n // tn,),
        in_specs=[pl.BlockSpec((rows, d), lambda j: (0, 0)),
                  pl.BlockSpec((d, tn), lambda j: (0, j)),
                  pl.BlockSpec((1, tn), lambda j: (0, j))],
        out_specs=pl.BlockSpec((rows, tn), lambda j: (0, j)),
        out_shape=jax.ShapeDtypeStruct((rows, n), F32),
        compiler_params=_cparams(("arbitrary",)),
        name="ada",
    )(c_pad, w, b)


def _rms_mod(x, g, shift, scale):
    ms = jnp.mean(x * x, axis=-1, keepdims=True)
    xn = x * lax.rsqrt(ms + EPS) * g
    return xn * (1.0 + scale) + shift


def _in_kernel(x_ref, mod_ref, g_ref, w_ref, o_ref, h_ref):
    @pl.when(pl.program_id(1) == 0)
    def _():
        h = _rms_mod(x_ref[...], g_ref[...], mod_ref[0, 0:1, :], mod_ref[0, 1:2, :])
        h_ref[...] = h.astype(BF16)

    o_ref[...] = jnp.dot(h_ref[...], w_ref[...].astype(BF16),
                         preferred_element_type=F32).astype(o_ref.dtype)


def _in_proj(x2, mod3, g, w, seq):
    t, d = x2.shape
    n = w.shape[1]
    tm, tn = 1024, 1024
    per_b = seq // tm
    return pl.pallas_call(
        _in_kernel,
        grid=(t // tm, n // tn),
        in_specs=[pl.BlockSpec((tm, d), lambda i, j: (i, 0)),
                  pl.BlockSpec((1, 6, d), lambda i, j: (i // per_b, 0, 0)),
                  pl.BlockSpec((1, d), lambda i, j: (0, 0)),
                  pl.BlockSpec((d, tn), lambda i, j: (0, j))],
        out_specs=pl.BlockSpec((tm, tn), lambda i, j: (i, j)),
        out_shape=jax.ShapeDtypeStruct((t, n), BF16),
        scratch_shapes=[pltpu.VMEM((tm, d), BF16)],
        compiler_params=_cparams(("arbitrary", "arbitrary")),
        name="in_proj",
    )(x2, mod3, g, w)


def _ret_kernel(q_ref, k_ref, v0_ref, v1_ref, rg0_ref, rg1_ref, cos_ref, sin_ref,
                idec_ref, qdec_ref, kdec_ref, o_ref, state_ref, *, chunk_decay):
    @pl.when(pl.program_id(1) == 0)
    def _():
        state_ref[...] = jnp.zeros_like(state_ref)

    cos = cos_ref[...]
    sin = sin_ref[...]

    def rot(t):
        return t * cos + pltpu.roll(t, RET_DK // 2, axis=1) * sin

    for h in range(RET_HEADS):
        v_ref, rg_ref = (v0_ref, rg0_ref) if h < RET_HEADS // 2 else (v1_ref, rg1_ref)
        vc = (h % (RET_HEADS // 2)) * RET_DV
        qh = rot(q_ref[:, h * RET_DK:(h + 1) * RET_DK].astype(F32))
        kh = rot(k_ref[:, h * RET_DK:(h + 1) * RET_DK].astype(F32)) * (RET_DK ** -0.5)
        vh = v_ref[:, vc:vc + RET_DV]
        qb = qh.astype(BF16)
        kb = kh.astype(BF16)
        scores = lax.dot_general(qb, kb, (((1,), (1,)), ((), ())),
                                 preferred_element_type=F32) * idec_ref[h]
        inner = jnp.dot(scores.astype(BF16), vh, preferred_element_type=F32)
        state = state_ref[h]
        cross = jnp.dot((qh * qdec_ref[h]).astype(BF16), state.astype(BF16),
                        preferred_element_type=F32)
        kv = lax.dot_general((kh * kdec_ref[h]).astype(BF16), vh, (((0,), (0,)), ((), ())),
                             preferred_element_type=F32)
        state_ref[h] = state * chunk_decay[h] + kv
        r = inner + cross
        r = r * lax.rsqrt(jnp.mean(r * r, axis=-1, keepdims=True) + EPS)
        gate = _silu(rg_ref[:, vc:vc + RET_DV].astype(F32))
        o_ref[:, h * RET_DV:(h + 1) * RET_DV] = (gate * r).astype(o_ref.dtype)


def _retention(u, batch, seq):
    t = u.shape[0]
    nch = seq // CHUNK
    inv = ROPE_BASE ** (-jnp.arange(0, RET_DK, 2, dtype=F32) / RET_DK)
    ang = jnp.arange(seq, dtype=F32)[:, None] * inv[None, :]
    cos, sin = jnp.cos(ang), jnp.sin(ang)
    cos_t = jnp.concatenate([cos, cos], axis=-1)
    sin_t = jnp.concatenate([-sin, sin], axis=-1)
    log_g = jnp.log1p(-(2.0 ** (-5.0 - jnp.arange(RET_HEADS, dtype=F32))))
    i = jnp.arange(CHUNK, dtype=F32)
    diff = i[:, None] - i[None, :]
    idec = jnp.where(diff >= 0, jnp.exp(log_g[:, None, None] * jnp.maximum(diff, 0.0)), 0.0)
    qdec = jnp.broadcast_to(jnp.exp(log_g[:, None] * (i + 1.0))[:, :, None],
                            (RET_HEADS, CHUNK, RET_DK))
    kdec = jnp.broadcast_to(jnp.exp(log_g[:, None] * (CHUNK - 1.0 - i))[:, :, None],
                            (RET_HEADS, CHUNK, RET_DK))
    log_g_np = np.log1p(-(2.0 ** (-5.0 - np.arange(RET_HEADS, dtype=np.float32)))).astype(np.float32)
    chunk_decay = tuple(float(np.exp(np.float32(lg * np.float32(CHUNK)))) for lg in log_g_np)

    def ublk(c):
        return pl.BlockSpec((CHUNK, COL_BLK), lambda b, n, c=c: (b * nch + n, c))

    def table():
        return pl.BlockSpec((RET_HEADS, CHUNK, RET_DK), lambda b, n: (0, 0, 0))

    return pl.pallas_call(
        functools.partial(_ret_kernel, chunk_decay=chunk_decay),
        grid=(batch, nch),
        in_specs=[ublk(Q_BLK), ublk(K_BLK), ublk(V_BLK), ublk(V_BLK + 1),
                  ublk(RG_BLK), ublk(RG_BLK + 1),
                  pl.BlockSpec((CHUNK, RET_DK), lambda b, n: (n, 0)),
                  pl.BlockSpec((CHUNK, RET_DK), lambda b, n: (n, 0)),
                  table(), table(), table()],
        out_specs=pl.BlockSpec((CHUNK, RET_HEADS * RET_DV), lambda b, n: (b * nch + n, 0)),
        out_shape=jax.ShapeDtypeStruct((t, RET_HEADS * RET_DV), BF16),
        scratch_shapes=[pltpu.VMEM((RET_HEADS, RET_DK, RET_DV), F32)],
        compiler_params=_cparams(("arbitrary", "arbitrary")),
        name="retention",
    )(u, u, u, u, u, u, cos_t, sin_t, idec, qdec, kdec)


HALO = 8


MIX_TN = 512


def _mix_kernel(cb_ref, cc_ref, cx_ref, hcc_ref, hcx_ref, z_ref, ga0_ref, ga1_ref, gb0_ref, gb1_ref,
                cw_ref, wc_ref, wr_ref, wo_ref, x_ref, mod_ref, o_ref, m_ref, mg_ref,
                *, tiles_per_seq):
    tm = cb_ref.shape[0]
    d = o_ref.shape[1]
    m = cc_ref[...].astype(F32) * cx_ref[...].astype(F32)
    halo = hcc_ref[...].astype(F32) * hcx_ref[...].astype(F32)
    first = (pl.program_id(0) % tiles_per_seq) == 0
    m_ref[0:HALO, :] = jnp.where(first, 0.0, halo)
    m_ref[HALO:HALO + tm, :] = m
    conv = (m_ref[HALO - 2:HALO - 2 + tm, :] * cw_ref[0:1, :]
            + m_ref[HALO - 1:HALO - 1 + tm, :] * cw_ref[1:2, :]
            + m * cw_ref[2:3, :])
    p = (cb_ref[...].astype(F32) * conv).astype(BF16)
    z = z_ref[...]
    for n in range(d // MIX_TN):
        cols = slice(n * MIX_TN, (n + 1) * MIX_TN)
        ga_ref, gb_ref = (ga0_ref, gb0_ref) if n < COL_BLK // MIX_TN else (ga1_ref, gb1_ref)
        gcols = slice((n * MIX_TN) % COL_BLK, (n * MIX_TN) % COL_BLK + MIX_TN)
        ya = jnp.dot(p, wc_ref[:, cols], preferred_element_type=F32)
        yb = jnp.dot(z, wr_ref[:, cols], preferred_element_type=F32)
        merged = (_sigmoid(ga_ref[:, gcols].astype(F32)) * ya
                  + _sigmoid(gb_ref[:, gcols].astype(F32)) * yb)
        mg_ref[:, cols] = merged.astype(BF16)
    mg = mg_ref[...]
    for n in range(d // MIX_TN):
        cols = slice(n * MIX_TN, (n + 1) * MIX_TN)
        acc = jnp.dot(mg, wo_ref[:, cols], preferred_element_type=F32)
        o_ref[:, cols] = x_ref[:, cols] + mod_ref[0, 2:3, cols] * acc


def _mix(u, z, conv_w, wc_bf, wr_bf, wo_bf, x2, mod3, seq):
    t, d = x2.shape
    tm = 256
    hb = tm // HALO
    per_b = seq // tm

    def ublk(c):
        return pl.BlockSpec((tm, COL_BLK), lambda i, c=c: (i, c))

    def halo(c):
        return pl.BlockSpec((HALO, COL_BLK), lambda i, c=c: (jnp.maximum(i * hb - 1, 0), c))

    def resident(shape):
        return pl.BlockSpec(shape, lambda i: (0, 0), pipeline_mode=pl.Buffered(1))

    return pl.pallas_call(
        functools.partial(_mix_kernel, tiles_per_seq=per_b),
        grid=(t // tm,),
        in_specs=[ublk(CB_BLK), ublk(CC_BLK), ublk(CX_BLK), halo(CC_BLK), halo(CX_BLK),
                  pl.BlockSpec((tm, z.shape[1]), lambda i: (i, 0)),
                  ublk(GA_BLK), ublk(GA_BLK + 1), ublk(GB_BLK), ublk(GB_BLK + 1),
                  pl.BlockSpec((CONV_K, CONV_CH), lambda i: (0, 0)),
                  resident(wc_bf.shape), resident(wr_bf.shape), resident(wo_bf.shape),
                  pl.BlockSpec((tm, d), lambda i: (i, 0)),
                  pl.BlockSpec((1, 6, d), lambda i: (i // per_b, 0, 0))],
        out_specs=pl.BlockSpec((tm, d), lambda i: (i, 0)),
        out_shape=jax.ShapeDtypeStruct((t, d), F32),
        scratch_shapes=[pltpu.VMEM((tm + HALO, CONV_CH), F32),
                        pltpu.VMEM((tm, d), BF16)],
        compiler_params=_cparams(("arbitrary",)),
        name="mix",
    )(u, u, u, u, u, z, u, u, u, u, conv_w, wc_bf, wr_bf, wo_bf, x2, mod3)


ROUTE_E0 = N_GROUPS


def _split_bf16(v):
    hi = v.astype(BF16)
    lo = (v - hi.astype(F32)).astype(BF16)
    return hi, lo


U32 = jnp.uint32
HALF = D_MODEL // 2


def _pack_pair(lo, hi):
    ulo = lax.bitcast_convert_type(lo.astype(BF16).astype(F32), U32)
    uhi = lax.bitcast_convert_type(hi.astype(BF16).astype(F32), U32)
    return (ulo >> 16) | uhi


def _unpack_pair(w):
    lo = lax.bitcast_convert_type(w << 16, F32)
    hi = lax.bitcast_convert_type(w & jnp.uint32(0xFFFF0000), F32)
    return lo, hi


ROW_SUB = HALF // LANES


def _row_slab(r):
    if isinstance(r, int):
        return pl.ds(r * ROW_SUB, ROW_SUB)
    return pl.ds(pl.multiple_of(r * ROW_SUB, ROW_SUB), ROW_SUB)


def _store_rows(ref, row0, v):
    for c in range(ROW_SUB):
        ref[pl.ds(row0 * ROW_SUB + c, v.shape[0], stride=ROW_SUB), :] = v[:, c * LANES:(c + 1) * LANES]


def _load_rows(ref, row0, n):
    return jnp.concatenate(
        [ref[pl.ds(row0 * ROW_SUB + c, n, stride=ROW_SUB), :] for c in range(ROW_SUB)], axis=1)


def _route_kernel(x_ref, mod_ref, g_ref, wr_ref, br_ref, h_ref, ef_ref, cw_ref):
    h2 = _rms_mod(x_ref[...], g_ref[...], mod_ref[0, 3:4, :], mod_ref[0, 4:5, :])
    h_ref[...] = _pack_pair(h2[:, :HALF], h2[:, HALF:])
    hh, hl = _split_bf16(h2)
    wh, wl = _split_bf16(wr_ref[...])
    logits = (jnp.dot(hh, wh, preferred_element_type=F32)
              + (jnp.dot(hh, wl, preferred_element_type=F32)
                 + jnp.dot(hl, wh, preferred_element_type=F32))) + br_ref[...]
    lane = lax.broadcasted_iota(I32, logits.shape, 1).astype(F32)
    neg = -jnp.inf
    big = float(LANES)
    gl = jnp.where(lane < N_GROUPS, logits, neg)
    gmax = jnp.max(gl, axis=-1, keepdims=True)
    gsel = jnp.min(jnp.where(gl == gmax, lane, big), axis=-1, keepdims=True)
    gsum = jnp.sum(jnp.exp(gl - gmax), axis=-1, keepdims=True)
    g_w = 1.0 / gsum
    lo_lane = ROUTE_E0 + EXPERTS_PER_GROUP * gsel
    emask = (lane >= lo_lane) & (lane < lo_lane + EXPERTS_PER_GROUP)
    el = jnp.where(emask, logits, neg)
    m1 = jnp.max(el, axis=-1, keepdims=True)
    i1 = jnp.min(jnp.where(el == m1, lane, big), axis=-1, keepdims=True)
    el2 = jnp.where(lane == i1, neg, el)
    m2 = jnp.max(el2, axis=-1, keepdims=True)
    i2 = jnp.min(jnp.where(el2 == m2, lane, big), axis=-1, keepdims=True)
    esum = jnp.sum(jnp.exp(el - m1), axis=-1, keepdims=True)
    p1 = 1.0 / esum
    p2 = jnp.exp(m2 - m1) / esum
    tot = p1 + p2
    c1 = g_w * (p1 / tot)
    c2 = g_w * (p2 / tot)
    ef_ref[...] = jnp.where(lane == 0.0, i1 - ROUTE_E0, jnp.where(lane == 1.0, i2 - ROUTE_E0, 0.0))
    cw_ref[...] = jnp.where(lane == 0.0, c1, jnp.where(lane == 1.0, c2, 0.0))


def _route(x1, mod3, g, w_r, b_r, seq):
    t, d = x1.shape
    tm = 512
    per_b = seq // tm
    return pl.pallas_call(
        _route_kernel,
        grid=(t // tm,),
        in_specs=[pl.BlockSpec((tm, d), lambda i: (i, 0)),
                  pl.BlockSpec((1, 6, d), lambda i: (i // per_b, 0, 0)),
                  pl.BlockSpec((1, d), lambda i: (0, 0)),
                  pl.BlockSpec((d, LANES), lambda i: (0, 0)),
                  pl.BlockSpec((1, LANES), lambda i: (0, 0))],
        out_specs=[pl.BlockSpec((tm, HALF), lambda i: (i, 0)),
                   pl.BlockSpec((tm, LANES), lambda i: (i, 0)),
                   pl.BlockSpec((tm, LANES), lambda i: (i, 0))],
        out_shape=[jax.ShapeDtypeStruct((t, HALF), U32),
                   jax.ShapeDtypeStruct((t, LANES), F32),
                   jax.ShapeDtypeStruct((t, LANES), F32)],
        compiler_params=_cparams(("arbitrary",)),
        name="route",
    )(x1, mod3, g, w_r, b_r)


META_TB = 512
BLK_ROWS = 256


def _meta_kernel(ef_ref, dest_ref, blk_ref, pref_ref):
    t = ef_ref.shape[0]
    nb = t // META_TB
    lane = lax.broadcasted_iota(I32, (META_TB, LANES), 1).astype(F32)
    r_i = lax.broadcasted_iota(I32, (META_TB, META_TB), 0)
    c_i = lax.broadcasted_iota(I32, (META_TB, META_TB), 1)
    tril = jnp.where(r_i > c_i, 1.0, 0.0).astype(BF16)

    def onehots(i):
        ef = ef_ref[pl.ds(i * META_TB, META_TB), :]
        oh1 = jnp.where(lane == ef[:, 0:1], 1.0, 0.0)
        oh2 = jnp.where(lane == ef[:, 1:2], 1.0, 0.0)
        return oh1, oh2

    def pass1(i, carry):
        oh1, oh2 = onehots(i)
        cnt = oh1 + oh2
        pref = jnp.dot(tril, cnt.astype(BF16), preferred_element_type=F32) + carry
        pref_ref[pl.ds(i * META_TB, META_TB), :] = pref
        return carry + jnp.sum(cnt, axis=0, keepdims=True)

    counts = lax.fori_loop(0, nb, pass1, jnp.zeros((1, LANES), F32))
    nblk = jnp.floor((counts + (EXPERT_BLOCK - 1)) * (1.0 / EXPERT_BLOCK))
    u_r = lax.broadcasted_iota(I32, (LANES, LANES), 0)
    u_c = lax.broadcasted_iota(I32, (LANES, LANES), 1)
    upper = jnp.where(u_r <= u_c, 1.0, 0.0).astype(BF16)
    nblk8 = jnp.broadcast_to(nblk, (8, LANES))
    pend_blk = jnp.dot(nblk8.astype(BF16), upper, preferred_element_type=F32)[0:1, :]
    pstart_blk = pend_blk - nblk
    pstart = pstart_blk * float(EXPERT_BLOCK)

    def pass2(i, _):
        oh1, oh2 = onehots(i)
        base = pref_ref[pl.ds(i * META_TB, META_TB), :] + pstart
        d1 = jnp.sum(oh1 * base, axis=-1, keepdims=True)
        d2 = jnp.sum(oh2 * base, axis=-1, keepdims=True)
        dest = jnp.where(lane == 0.0, d1, jnp.where(lane == 1.0, d2, 0.0))
        dest_ref[pl.ds(i * META_TB, META_TB), :] = dest.astype(I32)
        return 0

    lax.fori_loop(0, nb, pass2, 0)

    blane = lax.broadcasted_iota(I32, (BLK_ROWS, LANES), 1).astype(F32)
    brow = lax.broadcasted_iota(I32, (BLK_ROWS, LANES), 0).astype(F32)
    emask = blane < N_EXPERTS
    owner = jnp.sum(jnp.where(emask & (pend_blk <= brow), 1.0, 0.0), axis=-1, keepdims=True)
    owner = jnp.minimum(owner, float(N_EXPERTS - 1))
    total = jnp.sum(jnp.where(emask, nblk, 0.0), axis=-1, keepdims=True)
    used = jnp.where(brow < total, 1.0, 0.0)
    mine = emask & (blane == owner)
    cnt_b = jnp.sum(jnp.where(mine, counts, 0.0), axis=-1, keepdims=True)
    ps_b = jnp.sum(jnp.where(mine, pstart_blk, 0.0), axis=-1, keepdims=True)
    nvalid = jnp.clip(cnt_b - (brow - ps_b) * float(EXPERT_BLOCK), 0.0, float(EXPERT_BLOCK))
    pad_lo = nvalid * used
    blk_ref[...] = jnp.where(blane == 0.0, owner,
                             jnp.where(blane == 1.0, used,
                                       jnp.where(blane == 2.0, pad_lo, 0.0))).astype(I32)


def _meta(ef):
    t = ef.shape[0]
    return pl.pallas_call(
        _meta_kernel,
        out_shape=[jax.ShapeDtypeStruct((t, LANES), I32),
                   jax.ShapeDtypeStruct((BLK_ROWS, LANES), I32)],
        scratch_shapes=[pltpu.VMEM((t, LANES), F32)],
        compiler_params=pltpu.CompilerParams(vmem_limit_bytes=VMEM_LIMIT),
        name="meta",
    )(ef)


DISP_TM = 256
TOP_K = 2


def _dispatch_kernel(pad_lo_ref, dest_ref, h_ref, xbuf_ref, stage, zrow, sems, zsem):
    i = pl.program_id(0)
    n = pl.num_programs(0)
    tm = h_ref.shape[0]
    n_blk = pad_lo_ref.shape[0]
    slot = i % 2

    @pl.when(i == 0)
    def _():
        zrow[...] = jnp.zeros_like(zrow)

        def pad_rows(action):
            def blk(b, _):
                lo = pad_lo_ref[b]
                base = pl.multiple_of(b * EXPERT_BLOCK, EXPERT_BLOCK)

                @pl.when(lo == 0)
                def _():
                    whole = pl.ds(pl.multiple_of(base * ROW_SUB, EXPERT_BLOCK * ROW_SUB),
                                  EXPERT_BLOCK * ROW_SUB)
                    action(pltpu.make_async_copy(zrow, xbuf_ref.at[whole], zsem))

                @pl.when(lo > 0)
                def _():
                    def row(r, _):
                        action(pltpu.make_async_copy(zrow.at[_row_slab(0)],
                                                     xbuf_ref.at[_row_slab(base + r)], zsem))
                        return 0
                    lax.fori_loop(lo, EXPERT_BLOCK, row, 0)
                return 0
            lax.fori_loop(0, n_blk, blk, 0)

        pad_rows(lambda cp: cp.start())
        pad_rows(lambda cp: cp.wait())

    _store_rows(stage.at[slot], 0, h_ref[...])
    for j in range(tm):
        for k in range(TOP_K):
            d = dest_ref[0, 0, TOP_K * j + k]
            pltpu.make_async_copy(stage.at[slot, _row_slab(j)], xbuf_ref.at[_row_slab(d)],
                                  sems.at[slot]).start(priority=k)

    def wait_tile(s):
        for _ in range(TOP_K):
            pltpu.make_async_copy(stage.at[s], xbuf_ref.at[pl.ds(0, tm * ROW_SUB)], sems.at[s]).wait()

    @pl.when(i > 0)
    def _():
        wait_tile(1 - slot)

    @pl.when(i == n - 1)
    def _():
        wait_tile(slot)


def _dispatch(pad_lo, dest3, hp, n_rows):
    t, w = hp.shape
    tm = DISP_TM
    grid_spec = pltpu.PrefetchScalarGridSpec(
        num_scalar_prefetch=1,
        grid=(t // tm,),
        in_specs=[pl.BlockSpec((1, 1, TOP_K * tm), lambda i, *_: (i, 0, 0), memory_space=pltpu.SMEM),
                  pl.BlockSpec((tm, w), lambda i, *_: (i, 0))],
        out_specs=pl.BlockSpec(memory_space=pl.ANY),
        scratch_shapes=[pltpu.VMEM((2, tm * ROW_SUB, LANES), U32),
                        pltpu.VMEM((EXPERT_BLOCK * ROW_SUB, LANES), U32),
                        pltpu.SemaphoreType.DMA((2,)), pltpu.SemaphoreType.DMA(())],
    )
    return pl.pallas_call(
        _dispatch_kernel,
        grid_spec=grid_spec,
        out_shape=jax.ShapeDtypeStruct((n_rows * ROW_SUB, LANES), U32),
        compiler_params=_cparams(("arbitrary",)),
        name="dispatch",
    )(pad_lo, dest3, hp)


MOE_SUB = 2
W_RING = 2
CAST_ROWS = 256
SWITCH_AFTER = 4


def _moe_kernel(run_e_ref, run_id_ref, used_ref, nruns_ref, x_ref, wg_hbm, wu_hbm, wd_hbm, o_ref,
                wg_st, wu_st, wd_st, wg_bf, wu_bf, wd_bf, sems, state):
    step = pl.program_id(0)
    n_runs = nruns_ref[0]
    streams = ((wg_hbm, wg_st, wg_bf), (wu_hbm, wu_st, wu_bf), (wd_hbm, wd_st, wd_bf))

    def weight_copy(m, run):
        hbm, stage, _ = streams[m]
        return pltpu.make_async_copy(hbm.at[run_e_ref[run]], stage, sems.at[m])

    @pl.when(step == 0)
    def _():
        state[0] = 0
        state[1] = 0
        for m in range(len(streams)):
            weight_copy(m, 0).start(priority=1)

    def switch_to(run, b):
        ring = lax.rem(run, W_RING)
        for m, (_, stage, bf) in enumerate(streams):
            weight_copy(m, run).wait()

            def body(i, _, stage=stage, bf=bf):
                rows = pl.ds(pl.multiple_of(i * CAST_ROWS, CAST_ROWS), CAST_ROWS)
                bf[ring, rows, :] = stage[rows, :].astype(BF16)
                return 0
            lax.fori_loop(0, stage.shape[0] // CAST_ROWS, body, 0)

            @pl.when(run + 1 < n_runs)
            def _(m=m):
                weight_copy(m, run + 1).start(priority=1)
        state[0] = run + 1
        state[1] = b + SWITCH_AFTER

    def run_block(b, row0):
        run = run_id_ref[b]
        pl.when(state[0] == run)(functools.partial(switch_to, run, b))
        pl.when((state[0] == run + 1) & (run + 1 < n_runs) & (b >= state[1]))(
            functools.partial(switch_to, run + 1, b))

        ring = lax.rem(run, W_RING)
        xlo, xhi = _unpack_pair(_load_rows(x_ref, row0, EXPERT_BLOCK))
        xb = jnp.concatenate([xlo.astype(BF16), xhi.astype(BF16)], axis=1)
        g = jnp.dot(xb, wg_bf[ring], preferred_element_type=F32)
        u = jnp.dot(xb, wu_bf[ring], preferred_element_type=F32)
        hmid = (_silu(g) * u).astype(BF16)
        y = jnp.dot(hmid, wd_bf[ring], preferred_element_type=F32)
        _store_rows(o_ref, row0, _pack_pair(y[:, :HALF], y[:, HALF:]))

    for j in range(MOE_SUB):
        b = step * MOE_SUB + j
        row0 = j * EXPERT_BLOCK
        pl.when(used_ref[b] == 1)(functools.partial(run_block, b, row0))

        @pl.when(used_ref[b] == 0)
        def _():
            o_ref[pl.ds(row0 * ROW_SUB, EXPERT_BLOCK * ROW_SUB), :] = jnp.zeros(
                (EXPERT_BLOCK * ROW_SUB, LANES), U32)


def _moe(run_e, run_id, used, n_runs, xbuf, w_gate, w_up, w_down):
    r = xbuf.shape[0]
    rows = MOE_SUB * EXPERT_BLOCK * ROW_SUB
    d, de = w_gate.shape[1:]

    def xmap(s, run_e_ref, run_id_ref, used_ref, nruns_ref):
        del run_e_ref, run_id_ref, nruns_ref
        return (jnp.where(used_ref[s * MOE_SUB] == 1, s, 0), 0)

    grid_spec = pltpu.PrefetchScalarGridSpec(
        num_scalar_prefetch=4,
        grid=(r // rows,),
        in_specs=[pl.BlockSpec((rows, LANES), xmap),
                  pl.BlockSpec(memory_space=pl.ANY),
                  pl.BlockSpec(memory_space=pl.ANY),
                  pl.BlockSpec(memory_space=pl.ANY)],
        out_specs=pl.BlockSpec((rows, LANES), lambda s, *_: (s, 0)),
        scratch_shapes=[pltpu.VMEM((d, de), F32), pltpu.VMEM((d, de), F32), pltpu.VMEM((de, d), F32),
                        pltpu.VMEM((W_RING, d, de), BF16), pltpu.VMEM((W_RING, d, de), BF16),
                        pltpu.VMEM((W_RING, de, d), BF16),
                        pltpu.SemaphoreType.DMA((3,)), pltpu.SMEM((2,), I32)],
    )
    return pl.pallas_call(
        _moe_kernel,
        grid_spec=grid_spec,
        out_shape=jax.ShapeDtypeStruct((r, LANES), U32),
        compiler_params=_cparams(("arbitrary",), vmem=60 * 1024 * 1024),
        name="moe",
    )(run_e, run_id, used, n_runs, xbuf, w_gate, w_up, w_down)


FIN_TM = 256


def _final_kernel(dest_ref, dnext_ref, x_ref, cw_ref, mod_ref, g_ref, y_hbm, o_ref, ybuf, sems):
    i = pl.program_id(0)
    n = pl.num_programs(0)
    tm = x_ref.shape[0]
    slot = i % 2

    def gather(d_ref, s):
        for j in range(tm):
            for k in range(TOP_K):
                d = d_ref[0, 0, TOP_K * j + k]
                pltpu.make_async_copy(y_hbm.at[_row_slab(d)], ybuf.at[s, k, _row_slab(j)],
                                      sems.at[s]).start(priority=k)

    @pl.when(i == 0)
    def _():
        gather(dest_ref, 0)

    @pl.when(i + 1 < n)
    def _():
        gather(dnext_ref, 1 - slot)

    for k in range(TOP_K):
        pltpu.make_async_copy(y_hbm.at[pl.ds(0, tm * ROW_SUB)], ybuf.at[slot, k], sems.at[slot]).wait()

    cw = cw_ref[...]
    c0, c1 = cw[:, 0:1], cw[:, 1:2]
    lo0, hi0 = _unpack_pair(_load_rows(ybuf.at[slot, 0], 0, tm))
    lo1, hi1 = _unpack_pair(_load_rows(ybuf.at[slot, 1], 0, tm))
    g2 = mod_ref[0, 5:6, :]
    xlo = x_ref[:, :HALF] + g2[:, :HALF] * (lo0 * c0 + lo1 * c1)
    xhi = x_ref[:, HALF:] + g2[:, HALF:] * (hi0 * c0 + hi1 * c1)
    ms = (jnp.sum(xlo * xlo, axis=-1, keepdims=True)
          + jnp.sum(xhi * xhi, axis=-1, keepdims=True)) * (1.0 / D_MODEL)
    r = lax.rsqrt(ms + EPS)
    o_ref[:, :HALF] = xlo * r * g_ref[:, :HALF]
    o_ref[:, HALF:] = xhi * r * g_ref[:, HALF:]


def _final(dest3, x1, cw, mod3, g, ybuf, seq):
    t, d = x1.shape
    tm = FIN_TM
    nt = t // tm
    per_b = seq // tm

    def dspec(shift):
        return pl.BlockSpec((1, 1, TOP_K * tm), lambda i: (jnp.minimum(i + shift, nt - 1), 0, 0),
                            memory_space=pltpu.SMEM)

    return pl.pallas_call(
        _final_kernel,
        grid=(nt,),
        in_specs=[dspec(0), dspec(1),
                  pl.BlockSpec((tm, d), lambda i: (i, 0)),
                  pl.BlockSpec((tm, LANES), lambda i: (i, 0)),
                  pl.BlockSpec((1, 6, d), lambda i: (i // per_b, 0, 0)),
                  pl.BlockSpec((1, d), lambda i: (0, 0)),
                  pl.BlockSpec(memory_space=pl.ANY)],
        out_specs=pl.BlockSpec((tm, d), lambda i: (i, 0)),
        out_shape=jax.ShapeDtypeStruct((t, d), F32),
        scratch_shapes=[pltpu.VMEM((2, TOP_K, tm * ROW_SUB, LANES), U32),
                        pltpu.SemaphoreType.DMA((2,))],
        compiler_params=_cparams(("arbitrary",)),
        name="final",
    )(dest3, dest3, x1, cw, mod3, g, ybuf)


def kernel(x, c, w_ada, b_ada, norm1_g, w_in, conv_w, w_conv_out, w_ret_out, w_o, norm2_g,
           w_router_group, b_router_group, w_router_expert, b_router_expert,
           w_gate, w_up, w_down, norm_f_g):
    batch, seq, d = x.shape
    t = batch * seq
    depth = w_ada.shape[0]
    assert d == D_MODEL and w_in.shape[2] == IN_COLS and w_gate.shape[1] == N_EXPERTS
    assert depth == 1, "the final kernel fuses the last rmsnorm into the single layer"
    n_rows = t * 2 + N_EXPERTS * EXPERT_BLOCK
    n_blk = n_rows // EXPERT_BLOCK
    assert n_blk <= BLK_ROWS

    c_pad = jnp.pad(c, ((0, 8 - batch), (0, 0)))
    xs = x.reshape(t, d)
    for l in range(depth):
        mod = _ada(c_pad, w_ada[l], b_ada[l].reshape(1, -1))[:batch]
        mod3 = mod.reshape(batch, 6, d)

        u = _in_proj(xs, mod3, norm1_g[l].reshape(1, d), w_in[l], seq)
        z = _retention(u, batch, seq)
        x1 = _mix(u, z, conv_w[l], w_conv_out[l].astype(BF16), w_ret_out[l].astype(BF16),
                  w_o[l].astype(BF16), xs, mod3, seq)

        w_r = jnp.pad(jnp.concatenate([w_router_group[l], w_router_expert[l]], axis=1),
                      ((0, 0), (0, LANES - N_GROUPS - N_EXPERTS)))
        b_r = jnp.pad(jnp.concatenate([b_router_group[l], b_router_expert[l]]),
                      (0, LANES - N_GROUPS - N_EXPERTS)).reshape(1, LANES)
        h2, ef, cw = _route(x1, mod3, norm2_g[l].reshape(1, d), w_r, b_r, seq)

        dest, blk = _meta(ef)
        blk_e = blk[:n_blk, 0]
        used = blk[:n_blk, 1]
        prev_e = jnp.concatenate([jnp.full((1,), -1, I32), blk_e[:-1]])
        first = ((blk_e != prev_e) & (used == 1)).astype(I32)
        run_id = jnp.maximum(jnp.cumsum(first) - 1, 0).astype(I32)
        n_runs = jnp.sum(first).astype(I32).reshape(1)
        present = jnp.any((blk_e[None, :] == jnp.arange(N_EXPERTS, dtype=I32)[:, None])
                          & (used[None, :] == 1), axis=1)
        eids = jnp.arange(N_EXPERTS, dtype=I32)
        rank = jnp.cumsum(present.astype(I32)) - 1
        run_e = jnp.sum(jnp.where(present[None, :] & (rank[None, :] == eids[:, None]),
                                  eids[None, :], 0), axis=1).astype(I32)

        dest2 = dest[:, :TOP_K]
        xbuf = _dispatch(blk[:n_blk, 2], dest2.reshape(t // DISP_TM, 1, TOP_K * DISP_TM), h2, n_rows)
        ybuf = _moe(run_e, run_id, used, n_runs, xbuf, w_gate[l], w_up[l], w_down[l])
        xs = _final(dest2.reshape(t // FIN_TM, 1, TOP_K * FIN_TM), x1, cw, mod3,
                    norm_f_g.reshape(1, d), ybuf, seq)
    return xs.reshape(batch, seq, d)
```

```python
import functools

import numpy as np
import jax
import jax.numpy as jnp
from jax import lax
from jax.experimental import pallas as pl
from jax.experimental.pallas import tpu as pltpu

F32 = jnp.float32
BF16 = jnp.bfloat16
I32 = jnp.int32

D_MODEL = 2048
CONV_CH = 1024
CONV_K = 3
RET_HEADS = 8
RET_DK = 128
RET_DV = 256
CHUNK = 128
ROPE_BASE = 10000.0
N_GROUPS = 4
EXPERTS_PER_GROUP = 8
N_EXPERTS = 32
D_EXPERT = 1024
EXPERT_BLOCK = 128
EPS = 1e-6
IN_COLS = 13312
LANES = 128
COL_BLK = 1024
CB_BLK, CC_BLK, CX_BLK, Q_BLK, K_BLK, V_BLK, RG_BLK, GA_BLK, GB_BLK = 0, 1, 2, 3, 4, 5, 7, 9, 11

VMEM_LIMIT = 56 * 1024 * 1024


def _cparams(sem, vmem=VMEM_LIMIT):
    return pltpu.CompilerParams(dimension_semantics=sem, vmem_limit_bytes=vmem)


def _sigmoid(v):
    return 1.0 / (1.0 + jnp.exp(-v))


def _silu(v):
    return v * _sigmoid(v)


def _split_bf16(v):
    hi = v.astype(BF16)
    lo = (v - hi.astype(F32)).astype(BF16)
    return hi, lo


def _ada_kernel(c_ref, w_ref, b_ref, o_ref):
    rows = c_ref.shape[0]
    ch, cl = _split_bf16(_silu(c_ref[...]))
    wh, wl = _split_bf16(w_ref[...])
    p = jnp.dot(jnp.concatenate([ch, cl], axis=0), wh, preferred_element_type=F32)
    o_ref[...] = (p[:rows] + (p[rows:] + jnp.dot(ch, wl, preferred_element_type=F32))) + b_ref[...]


def _ada(c_pad, w, b):
    rows, d = c_pad.shape
    n = w.shape[1]
    tn = 512
    return pl.pallas_call(
        _ada_kernel,
        grid=(n // tn,),
        in_specs=[pl.BlockSpec((rows, d), lambda j: (0, 0)),
                  pl.BlockSpec((d, tn), lambda j: (0, j)),
                  pl.BlockSpec((1, tn), lambda j: (0, j))],
        out_specs=pl.BlockSpec((rows, tn), lambda j: (0, j)),
        out_shape=jax.ShapeDtypeStruct((rows, n), F32),
        compiler_params=_cparams(("arbitrary",)),
        name="ada",
    )(c_pad, w, b)


def _rms_mod(x, g, shift, scale):
    ms = jnp.mean(x * x, axis=-1, keepdims=True)
    xn = x * lax.rsqrt(ms + EPS) * g
    return xn * (1.0 + scale) + shift


NORM_ROWS = 64


def _in_kernel(x_ref, mod_ref, g_ref, w_ref, o_ref, h_ref):
    @pl.when(pl.program_id(1) == 0)
    def _():
        gain = g_ref[...] * (1.0 + mod_ref[0, 1:2, :])
        shift = mod_ref[0, 0:1, :]

        def body(i, _):
            rows = pl.ds(pl.multiple_of(i * NORM_ROWS, NORM_ROWS), NORM_ROWS)
            x = x_ref[rows, :]
            r = lax.rsqrt(jnp.mean(x * x, axis=-1, keepdims=True) + EPS)
            h_ref[rows, :] = (x * r * gain + shift).astype(BF16)
            return 0
        lax.fori_loop(0, x_ref.shape[0] // NORM_ROWS, body, 0)

    o_ref[...] = jnp.dot(h_ref[...], w_ref[...].astype(BF16),
                         preferred_element_type=F32).astype(o_ref.dtype)


def _in_proj(x2, mod3, g, w, seq):
    t, d = x2.shape
    n = w.shape[1]
    tm, tn = 1024, 1024
    per_b = seq // tm
    return pl.pallas_call(
        _in_kernel,
        grid=(t // tm, n // tn),
        in_specs=[pl.BlockSpec((tm, d), lambda i, j: (i, 0)),
                  pl.BlockSpec((1, 6, d), lambda i, j: (i // per_b, 0, 0)),
                  pl.BlockSpec((1, d), lambda i, j: (0, 0)),
                  pl.BlockSpec((d, tn), lambda i, j: (0, j))],
        out_specs=pl.BlockSpec((tm, tn), lambda i, j: (i, j)),
        out_shape=jax.ShapeDtypeStruct((t, n), BF16),
        scratch_shapes=[pltpu.VMEM((tm, d), BF16)],
        compiler_params=_cparams(("arbitrary", "arbitrary")),
        name="in_proj",
    )(x2, mod3, g, w)


RET_CPS = 4


def _ret_kernel(q_ref, k_ref, v0_ref, v1_ref, rg0_ref, rg1_ref, cos_ref, sin_ref,
                idec_ref, qdec_ref, kdec_ref, o_ref, state_ref, *, chunk_decay):
    @pl.when(pl.program_id(1) == 0)
    def _():
        state_ref[...] = jnp.zeros_like(state_ref)

    def rot(t, cos, sin):
        return t * cos + pltpu.roll(t, RET_DK // 2, axis=1) * sin

    for ci in range(RET_CPS):
        rows = slice(ci * CHUNK, (ci + 1) * CHUNK)
        cos = cos_ref[rows, :]
        sin = sin_ref[rows, :]
        for h in range(RET_HEADS):
            v_ref, rg_ref = (v0_ref, rg0_ref) if h < RET_HEADS // 2 else (v1_ref, rg1_ref)
            vc = (h % (RET_HEADS // 2)) * RET_DV
            qh = rot(q_ref[rows, h * RET_DK:(h + 1) * RET_DK].astype(F32), cos, sin)
            kh = rot(k_ref[rows, h * RET_DK:(h + 1) * RET_DK].astype(F32), cos, sin) * (RET_DK ** -0.5)
            vh = v_ref[rows, vc:vc + RET_DV]
            qb = qh.astype(BF16)
            kb = kh.astype(BF16)
            scores = lax.dot_general(qb, kb, (((1,), (1,)), ((), ())),
                                     preferred_element_type=F32) * idec_ref[h]
            inner = jnp.dot(scores.astype(BF16), vh, preferred_element_type=F32)
            state = state_ref[h]
            cross = jnp.dot((qh * qdec_ref[h]).astype(BF16), state.astype(BF16),
                            preferred_element_type=F32)
            kv = lax.dot_general((kh * kdec_ref[h]).astype(BF16), vh, (((0,), (0,)), ((), ())),
                                 preferred_element_type=F32)
            state_ref[h] = state * chunk_decay[h] + kv
            r = inner + cross
            r = r * lax.rsqrt(jnp.mean(r * r, axis=-1, keepdims=True) + EPS)
            gate = _silu(rg_ref[rows, vc:vc + RET_DV].astype(F32))
            o_ref[rows, h * RET_DV:(h + 1) * RET_DV] = (gate * r).astype(o_ref.dtype)


def _retention(u, batch, seq):
    t = u.shape[0]
    nch = seq // CHUNK
    inv = ROPE_BASE ** (-jnp.arange(0, RET_DK, 2, dtype=F32) / RET_DK)
    ang = jnp.arange(seq, dtype=F32)[:, None] * inv[None, :]
    cos, sin = jnp.cos(ang), jnp.sin(ang)
    cos_t = jnp.concatenate([cos, cos], axis=-1)
    sin_t = jnp.concatenate([-sin, sin], axis=-1)
    log_g = jnp.log1p(-(2.0 ** (-5.0 - jnp.arange(RET_HEADS, dtype=F32))))
    i = jnp.arange(CHUNK, dtype=F32)
    diff = i[:, None] - i[None, :]
    idec = jnp.where(diff >= 0, jnp.exp(log_g[:, None, None] * jnp.maximum(diff, 0.0)), 0.0)
    qdec = jnp.broadcast_to(jnp.exp(log_g[:, None] * (i + 1.0))[:, :, None],
                            (RET_HEADS, CHUNK, RET_DK))
    kdec = jnp.broadcast_to(jnp.exp(log_g[:, None] * (CHUNK - 1.0 - i))[:, :, None],
                            (RET_HEADS, CHUNK, RET_DK))
    log_g_np = np.log1p(-(2.0 ** (-5.0 - np.arange(RET_HEADS, dtype=np.float32)))).astype(np.float32)
    chunk_decay = tuple(float(np.exp(np.float32(lg * np.float32(CHUNK)))) for lg in log_g_np)

    rows = RET_CPS * CHUNK
    nst = seq // rows

    def ublk(c):
        return pl.BlockSpec((rows, COL_BLK), lambda b, n, c=c: (b * nst + n, c))

    def table():
        return pl.BlockSpec((RET_HEADS, CHUNK, RET_DK), lambda b, n: (0, 0, 0))

    return pl.pallas_call(
        functools.partial(_ret_kernel, chunk_decay=chunk_decay),
        grid=(batch, nst),
        in_specs=[ublk(Q_BLK), ublk(K_BLK), ublk(V_BLK), ublk(V_BLK + 1),
                  ublk(RG_BLK), ublk(RG_BLK + 1),
                  pl.BlockSpec((rows, RET_DK), lambda b, n: (n, 0)),
                  pl.BlockSpec((rows, RET_DK), lambda b, n: (n, 0)),
                  table(), table(), table()],
        out_specs=pl.BlockSpec((rows, RET_HEADS * RET_DV), lambda b, n: (b * nst + n, 0)),
        out_shape=jax.ShapeDtypeStruct((t, RET_HEADS * RET_DV), BF16),
        scratch_shapes=[pltpu.VMEM((RET_HEADS, RET_DK, RET_DV), F32)],
        compiler_params=_cparams(("arbitrary", "arbitrary")),
        name="retention",
    )(u, u, u, u, u, u, cos_t, sin_t, idec, qdec, kdec)


HALO = 8


MIX_TN = 512


def _mix_kernel(cb_ref, cc_ref, cx_ref, hcc_ref, hcx_ref, z_ref, ga0_ref, ga1_ref, gb0_ref, gb1_ref,
                cw_ref, wc_ref, wr_ref, wo_ref, x_ref, mod_ref, o_ref, m_ref, mg_ref,
                *, tiles_per_seq):
    tm = cb_ref.shape[0]
    d = o_ref.shape[1]
    m = cc_ref[...].astype(F32) * cx_ref[...].astype(F32)
    halo = hcc_ref[...].astype(F32) * hcx_ref[...].astype(F32)
    first = (pl.program_id(0) % tiles_per_seq) == 0
    m_ref[0:HALO, :] = jnp.where(first, 0.0, halo)
    m_ref[HALO:HALO + tm, :] = m
    conv = (m_ref[HALO - 2:HALO - 2 + tm, :] * cw_ref[0:1, :]
            + m_ref[HALO - 1:HALO - 1 + tm, :] * cw_ref[1:2, :]
            + m * cw_ref[2:3, :])
    p = (cb_ref[...].astype(F32) * conv).astype(BF16)
    z = z_ref[...]
    for n in range(d // MIX_TN):
        cols = slice(n * MIX_TN, (n + 1) * MIX_TN)
        ga_ref, gb_ref = (ga0_ref, gb0_ref) if n < COL_BLK // MIX_TN else (ga1_ref, gb1_ref)
        gcols = slice((n * MIX_TN) % COL_BLK, (n * MIX_TN) % COL_BLK + MIX_TN)
        ya = jnp.dot(p, wc_ref[:, cols], preferred_element_type=F32)
        yb = jnp.dot(z, wr_ref[:, cols], preferred_element_type=F32)
        merged = (_sigmoid(ga_ref[:, gcols].astype(F32)) * ya
                  + _sigmoid(gb_ref[:, gcols].astype(F32)) * yb)
        mg_ref[:, cols] = merged.astype(BF16)
    mg = mg_ref[...]
    for n in range(d // MIX_TN):
        cols = slice(n * MIX_TN, (n + 1) * MIX_TN)
        acc = jnp.dot(mg, wo_ref[:, cols], preferred_element_type=F32)
        o_ref[:, cols] = x_ref[:, cols] + mod_ref[0, 2:3, cols] * acc


def _mix(u, z, conv_w, wc_bf, wr_bf, wo_bf, x2, mod3, seq):
    t, d = x2.shape
    tm = 256
    hb = tm // HALO
    per_b = seq // tm

    def ublk(c):
        return pl.BlockSpec((tm, COL_BLK), lambda i, c=c: (i, c))

    def halo(c):
        return pl.BlockSpec((HALO, COL_BLK), lambda i, c=c: (jnp.maximum(i * hb - 1, 0), c))

    def resident(shape):
        return pl.BlockSpec(shape, lambda i: (0, 0), pipeline_mode=pl.Buffered(1))

    return pl.pallas_call(
        functools.partial(_mix_kernel, tiles_per_seq=per_b),
        grid=(t // tm,),
        in_specs=[ublk(CB_BLK), ublk(CC_BLK), ublk(CX_BLK), halo(CC_BLK), halo(CX_BLK),
                  pl.BlockSpec((tm, z.shape[1]), lambda i: (i, 0)),
                  ublk(GA_BLK), ublk(GA_BLK + 1), ublk(GB_BLK), ublk(GB_BLK + 1),
                  pl.BlockSpec((CONV_K, CONV_CH), lambda i: (0, 0)),
                  resident(wc_bf.shape), resident(wr_bf.shape), resident(wo_bf.shape),
                  pl.BlockSpec((tm, d), lambda i: (i, 0)),
                  pl.BlockSpec((1, 6, d), lambda i: (i // per_b, 0, 0))],
        out_specs=pl.BlockSpec((tm, d), lambda i: (i, 0)),
        out_shape=jax.ShapeDtypeStruct((t, d), F32),
        scratch_shapes=[pltpu.VMEM((tm + HALO, CONV_CH), F32),
                        pltpu.VMEM((tm, d), BF16)],
        compiler_params=_cparams(("arbitrary",)),
        name="mix",
    )(u, u, u, u, u, z, u, u, u, u, conv_w, wc_bf, wr_bf, wo_bf, x2, mod3)


ROUTE_E0 = N_GROUPS


U32 = jnp.uint32
HALF = D_MODEL // 2


def _pack_pair(lo, hi):
    ulo = lax.bitcast_convert_type(lo.astype(BF16).astype(F32), U32)
    uhi = lax.bitcast_convert_type(hi.astype(BF16).astype(F32), U32)
    return (ulo >> 16) | uhi


def _unpack_pair(w):
    lo = lax.bitcast_convert_type(w << 16, F32)
    hi = lax.bitcast_convert_type(w & jnp.uint32(0xFFFF0000), F32)
    return lo, hi


ROW_SUB = HALF // LANES


def _row_slab(r):
    if isinstance(r, int):
        return pl.ds(r * ROW_SUB, ROW_SUB)
    return pl.ds(pl.multiple_of(r * ROW_SUB, ROW_SUB), ROW_SUB)


def _store_rows(ref, row0, v):
    for c in range(ROW_SUB):
        ref[pl.ds(row0 * ROW_SUB + c, v.shape[0], stride=ROW_SUB), :] = v[:, c * LANES:(c + 1) * LANES]


def _load_rows(ref, row0, n):
    return jnp.concatenate(
        [ref[pl.ds(row0 * ROW_SUB + c, n, stride=ROW_SUB), :] for c in range(ROW_SUB)], axis=1)


def _route_kernel(x_ref, mod_ref, g_ref, wr_ref, br_ref, h_ref, ef_ref, cw_ref):
    h2 = _rms_mod(x_ref[...], g_ref[...], mod_ref[0, 3:4, :], mod_ref[0, 4:5, :])
    h_ref[...] = _pack_pair(h2[:, :HALF], h2[:, HALF:])
    hh, hl = _split_bf16(h2)
    wh, wl = _split_bf16(wr_ref[...])
    logits = (jnp.dot(hh, wh, preferred_element_type=F32)
              + (jnp.dot(hh, wl, preferred_element_type=F32)
                 + jnp.dot(hl, wh, preferred_element_type=F32))) + br_ref[...]
    lane = lax.broadcasted_iota(I32, logits.shape, 1).astype(F32)
    neg = -jnp.inf
    big = float(LANES)
    gl = jnp.where(lane < N_GROUPS, logits, neg)
    gmax = jnp.max(gl, axis=-1, keepdims=True)
    gsel = jnp.min(jnp.where(gl == gmax, lane, big), axis=-1, keepdims=True)
    gsum = jnp.sum(jnp.exp(gl - gmax), axis=-1, keepdims=True)
    g_w = 1.0 / gsum
    lo_lane = ROUTE_E0 + EXPERTS_PER_GROUP * gsel
    emask = (lane >= lo_lane) & (lane < lo_lane + EXPERTS_PER_GROUP)
    el = jnp.where(emask, logits, neg)
    m1 = jnp.max(el, axis=-1, keepdims=True)
    i1 = jnp.min(jnp.where(el == m1, lane, big), axis=-1, keepdims=True)
    el2 = jnp.where(lane == i1, neg, el)
    m2 = jnp.max(el2, axis=-1, keepdims=True)
    i2 = jnp.min(jnp.where(el2 == m2, lane, big), axis=-1, keepdims=True)
    esum = jnp.sum(jnp.exp(el - m1), axis=-1, keepdims=True)
    p1 = 1.0 / esum
    p2 = jnp.exp(m2 - m1) / esum
    tot = p1 + p2
    c1 = g_w * (p1 / tot)
    c2 = g_w * (p2 / tot)
    ef_ref[...] = jnp.where(lane == 0.0, i1 - ROUTE_E0, jnp.where(lane == 1.0, i2 - ROUTE_E0, 0.0))
    cw_ref[...] = jnp.where(lane == 0.0, c1, jnp.where(lane == 1.0, c2, 0.0))


def _route(x1, mod3, g, w_r, b_r, seq):
    t, d = x1.shape
    tm = 512
    per_b = seq // tm
    return pl.pallas_call(
        _route_kernel,
        grid=(t // tm,),
        in_specs=[pl.BlockSpec((tm, d), lambda i: (i, 0)),
                  pl.BlockSpec((1, 6, d), lambda i: (i // per_b, 0, 0)),
                  pl.BlockSpec((1, d), lambda i: (0, 0)),
                  pl.BlockSpec((d, LANES), lambda i: (0, 0)),
                  pl.BlockSpec((1, LANES), lambda i: (0, 0))],
        out_specs=[pl.BlockSpec((tm, HALF), lambda i: (i, 0)),
                   pl.BlockSpec((tm, LANES), lambda i: (i, 0)),
                   pl.BlockSpec((tm, LANES), lambda i: (i, 0))],
        out_shape=[jax.ShapeDtypeStruct((t, HALF), U32),
                   jax.ShapeDtypeStruct((t, LANES), F32),
                   jax.ShapeDtypeStruct((t, LANES), F32)],
        compiler_params=_cparams(("arbitrary",)),
        name="route",
    )(x1, mod3, g, w_r, b_r)


META_TB = 512
BLK_ROWS = 256


def _meta_kernel(ef_ref, dest_ref, blk_ref, pref_ref):
    t = ef_ref.shape[0]
    nb = t // META_TB
    lane = lax.broadcasted_iota(I32, (META_TB, LANES), 1).astype(F32)
    r_i = lax.broadcasted_iota(I32, (META_TB, META_TB), 0)
    c_i = lax.broadcasted_iota(I32, (META_TB, META_TB), 1)
    tril = jnp.where(r_i > c_i, 1.0, 0.0).astype(BF16)

    def onehots(i):
        ef = ef_ref[pl.ds(i * META_TB, META_TB), :]
        oh1 = jnp.where(lane == ef[:, 0:1], 1.0, 0.0)
        oh2 = jnp.where(lane == ef[:, 1:2], 1.0, 0.0)
        return oh1, oh2

    def pass1(i, carry):
        oh1, oh2 = onehots(i)
        cnt = oh1 + oh2
        pref = jnp.dot(tril, cnt.astype(BF16), preferred_element_type=F32) + carry
        pref_ref[pl.ds(i * META_TB, META_TB), :] = pref
        return carry + jnp.sum(cnt, axis=0, keepdims=True)

    counts = lax.fori_loop(0, nb, pass1, jnp.zeros((1, LANES), F32))
    nblk = jnp.floor((counts + (EXPERT_BLOCK - 1)) * (1.0 / EXPERT_BLOCK))
    u_r = lax.broadcasted_iota(I32, (LANES, LANES), 0)
    u_c = lax.broadcasted_iota(I32, (LANES, LANES), 1)
    upper = jnp.where(u_r <= u_c, 1.0, 0.0).astype(BF16)
    nblk8 = jnp.broadcast_to(nblk, (8, LANES))
    pend_blk = jnp.dot(nblk8.astype(BF16), upper, preferred_element_type=F32)[0:1, :]
    pstart_blk = pend_blk - nblk
    pstart = pstart_blk * float(EXPERT_BLOCK)

    def pass2(i, _):
        oh1, oh2 = onehots(i)
        base = pref_ref[pl.ds(i * META_TB, META_TB), :] + pstart
        d1 = jnp.sum(oh1 * base, axis=-1, keepdims=True)
        d2 = jnp.sum(oh2 * base, axis=-1, keepdims=True)
        dest = jnp.where(lane == 0.0, d1, jnp.where(lane == 1.0, d2, 0.0))
        dest_ref[pl.ds(i * META_TB, META_TB), :] = dest.astype(I32)
        return 0

    lax.fori_loop(0, nb, pass2, 0)

    blane = lax.broadcasted_iota(I32, (BLK_ROWS, LANES), 1).astype(F32)
    brow = lax.broadcasted_iota(I32, (BLK_ROWS, LANES), 0).astype(F32)
    emask = blane < N_EXPERTS
    owner = jnp.sum(jnp.where(emask & (pend_blk <= brow), 1.0, 0.0), axis=-1, keepdims=True)
    owner = jnp.minimum(owner, float(N_EXPERTS - 1))
    total = jnp.sum(jnp.where(emask, nblk, 0.0), axis=-1, keepdims=True)
    used = jnp.where(brow < total, 1.0, 0.0)
    mine = emask & (blane == owner)
    cnt_b = jnp.sum(jnp.where(mine, counts, 0.0), axis=-1, keepdims=True)
    ps_b = jnp.sum(jnp.where(mine, pstart_blk, 0.0), axis=-1, keepdims=True)
    nvalid = jnp.clip(cnt_b - (brow - ps_b) * float(EXPERT_BLOCK), 0.0, float(EXPERT_BLOCK))
    pad_lo = nvalid * used
    blk_ref[...] = jnp.where(blane == 0.0, owner,
                             jnp.where(blane == 1.0, used,
                                       jnp.where(blane == 2.0, pad_lo, 0.0))).astype(I32)


def _meta(ef):
    t = ef.shape[0]
    return pl.pallas_call(
        _meta_kernel,
        out_shape=[jax.ShapeDtypeStruct((t, LANES), I32),
                   jax.ShapeDtypeStruct((BLK_ROWS, LANES), I32)],
        scratch_shapes=[pltpu.VMEM((t, LANES), F32)],
        compiler_params=pltpu.CompilerParams(vmem_limit_bytes=VMEM_LIMIT),
        name="meta",
    )(ef)


DISP_TM = 256
TOP_K = 2


def _dispatch_kernel(pad_lo_ref, dest_ref, h_ref, xbuf_ref, stage, zrow, sems, zsem):
    i = pl.program_id(0)
    n = pl.num_programs(0)
    tm = h_ref.shape[0]
    n_blk = pad_lo_ref.shape[0]
    slot = i % 2

    @pl.when(i == 0)
    def _():
        zrow[...] = jnp.zeros_like(zrow)

        def pad_rows(action):
            def blk(b, _):
                lo = pad_lo_ref[b]
                base = pl.multiple_of(b * EXPERT_BLOCK, EXPERT_BLOCK)

                @pl.when(lo == 0)
                def _():
                    whole = pl.ds(pl.multiple_of(base * ROW_SUB, EXPERT_BLOCK * ROW_SUB),
                                  EXPERT_BLOCK * ROW_SUB)
                    action(pltpu.make_async_copy(zrow, xbuf_ref.at[whole], zsem))

                @pl.when(lo > 0)
                def _():
                    def row(r, _):
                        action(pltpu.make_async_copy(zrow.at[_row_slab(0)],
                                                     xbuf_ref.at[_row_slab(base + r)], zsem))
                        return 0
                    lax.fori_loop(lo, EXPERT_BLOCK, row, 0)
                return 0
            lax.fori_loop(0, n_blk, blk, 0)

        pad_rows(lambda cp: cp.start())
        pad_rows(lambda cp: cp.wait())

    _store_rows(stage.at[slot], 0, h_ref[...])
    for j in range(tm):
        for k in range(TOP_K):
            d = dest_ref[0, 0, TOP_K * j + k]
            pltpu.make_async_copy(stage.at[slot, _row_slab(j)], xbuf_ref.at[_row_slab(d)],
                                  sems.at[slot]).start(priority=k)

    def wait_tile(s):
        for _ in range(TOP_K):
            pltpu.make_async_copy(stage.at[s], xbuf_ref.at[pl.ds(0, tm * ROW_SUB)], sems.at[s]).wait()

    @pl.when(i > 0)
    def _():
        wait_tile(1 - slot)

    @pl.when(i == n - 1)
    def _():
        wait_tile(slot)


def _dispatch(pad_lo, dest3, hp, n_rows):
    t, w = hp.shape
    tm = DISP_TM
    grid_spec = pltpu.PrefetchScalarGridSpec(
        num_scalar_prefetch=1,
        grid=(t // tm,),
        in_specs=[pl.BlockSpec((1, 1, TOP_K * tm), lambda i, *_: (i, 0, 0), memory_space=pltpu.SMEM),
                  pl.BlockSpec((tm, w), lambda i, *_: (i, 0))],
        out_specs=pl.BlockSpec(memory_space=pl.ANY),
        scratch_shapes=[pltpu.VMEM((2, tm * ROW_SUB, LANES), U32),
                        pltpu.VMEM((EXPERT_BLOCK * ROW_SUB, LANES), U32),
                        pltpu.SemaphoreType.DMA((2,)), pltpu.SemaphoreType.DMA(())],
    )
    return pl.pallas_call(
        _dispatch_kernel,
        grid_spec=grid_spec,
        out_shape=jax.ShapeDtypeStruct((n_rows * ROW_SUB, LANES), U32),
        compiler_params=_cparams(("arbitrary",)),
        name="dispatch",
    )(pad_lo, dest3, hp)


MOE_SUB = 2
W_RING = 2
CAST_ROWS = 256
SWITCH_AFTER = 4


def _moe_kernel(run_e_ref, run_id_ref, used_ref, nruns_ref, x_ref, wg_hbm, wu_hbm, wd_hbm, o_ref,
                wg_st, wu_st, wd_st, wg_bf, wu_bf, wd_bf, sems, state):
    step = pl.program_id(0)
    n_runs = nruns_ref[0]
    streams = ((wg_hbm, wg_st, wg_bf), (wu_hbm, wu_st, wu_bf), (wd_hbm, wd_st, wd_bf))

    def weight_copy(m, run):
        hbm, stage, _ = streams[m]
        return pltpu.make_async_copy(hbm.at[run_e_ref[run]], stage, sems.at[m])

    @pl.when(step == 0)
    def _():
        state[0] = 0
        state[1] = 0
        for m in range(len(streams)):
            weight_copy(m, 0).start(priority=1)

    def switch_to(run, b):
        ring = lax.rem(run, W_RING)
        for m, (_, stage, bf) in enumerate(streams):
            weight_copy(m, run).wait()

            def body(i, _, stage=stage, bf=bf):
                rows = pl.ds(pl.multiple_of(i * CAST_ROWS, CAST_ROWS), CAST_ROWS)
                bf[ring, rows, :] = stage[rows, :].astype(BF16)
                return 0
            lax.fori_loop(0, stage.shape[0] // CAST_ROWS, body, 0)

            @pl.when(run + 1 < n_runs)
            def _(m=m):
                weight_copy(m, run + 1).start(priority=1)
        state[0] = run + 1
        state[1] = b + SWITCH_AFTER

    def run_block(b, row0):
        run = run_id_ref[b]
        pl.when(state[0] == run)(functools.partial(switch_to, run, b))
        pl.when((state[0] == run + 1) & (run + 1 < n_runs) & (b >= state[1]))(
            functools.partial(switch_to, run + 1, b))

        ring = lax.rem(run, W_RING)
        xlo, xhi = _unpack_pair(_load_rows(x_ref, row0, EXPERT_BLOCK))
        xb = jnp.concatenate([xlo.astype(BF16), xhi.astype(BF16)], axis=1)
        g = jnp.dot(xb, wg_bf[ring], preferred_element_type=F32)
        u = jnp.dot(xb, wu_bf[ring], preferred_element_type=F32)
        hmid = (_silu(g) * u).astype(BF16)
        y = jnp.dot(hmid, wd_bf[ring], preferred_element_type=F32)
        _store_rows(o_ref, row0, _pack_pair(y[:, :HALF], y[:, HALF:]))

    for j in range(MOE_SUB):
        b = step * MOE_SUB + j
        row0 = j * EXPERT_BLOCK
        pl.when(used_ref[b] == 1)(functools.partial(run_block, b, row0))

        @pl.when(used_ref[b] == 0)
        def _():
            o_ref[pl.ds(row0 * ROW_SUB, EXPERT_BLOCK * ROW_SUB), :] = jnp.zeros(
                (EXPERT_BLOCK * ROW_SUB, LANES), U32)


def _moe(run_e, run_id, used, n_runs, xbuf, w_gate, w_up, w_down):
    r = xbuf.shape[0]
    rows = MOE_SUB * EXPERT_BLOCK * ROW_SUB
    d, de = w_gate.shape[1:]

    def xmap(s, run_e_ref, run_id_ref, used_ref, nruns_ref):
        del run_e_ref, run_id_ref, nruns_ref
        return (jnp.where(used_ref[s * MOE_SUB] == 1, s, 0), 0)

    grid_spec = pltpu.PrefetchScalarGridSpec(
        num_scalar_prefetch=4,
        grid=(r // rows,),
        in_specs=[pl.BlockSpec((rows, LANES), xmap),
                  pl.BlockSpec(memory_space=pl.ANY),
                  pl.BlockSpec(memory_space=pl.ANY),
                  pl.BlockSpec(memory_space=pl.ANY)],
        out_specs=pl.BlockSpec((rows, LANES), lambda s, *_: (s, 0)),
        scratch_shapes=[pltpu.VMEM((d, de), F32), pltpu.VMEM((d, de), F32), pltpu.VMEM((de, d), F32),
                        pltpu.VMEM((W_RING, d, de), BF16), pltpu.VMEM((W_RING, d, de), BF16),
                        pltpu.VMEM((W_RING, de, d), BF16),
                        pltpu.SemaphoreType.DMA((3,)), pltpu.SMEM((2,), I32)],
    )
    return pl.pallas_call(
        _moe_kernel,
        grid_spec=grid_spec,
        out_shape=jax.ShapeDtypeStruct((r, LANES), U32),
        compiler_params=_cparams(("arbitrary",), vmem=60 * 1024 * 1024),
        name="moe",
    )(run_e, run_id, used, n_runs, xbuf, w_gate, w_up, w_down)


FIN_TM = 256


def _final_kernel(dest_ref, dnext_ref, x_ref, cw_ref, mod_ref, g_ref, y_hbm, o_ref, ybuf, sems):
    i = pl.program_id(0)
    n = pl.num_programs(0)
    tm = x_ref.shape[0]
    slot = i % 2

    def gather(d_ref, s):
        for j in range(tm):
            for k in range(TOP_K):
                d = d_ref[0, 0, TOP_K * j + k]
                pltpu.make_async_copy(y_hbm.at[_row_slab(d)], ybuf.at[s, k, _row_slab(j)],
                                      sems.at[s]).start(priority=k)

    @pl.when(i == 0)
    def _():
        gather(dest_ref, 0)

    @pl.when(i + 1 < n)
    def _():
        gather(dnext_ref, 1 - slot)

    for k in range(TOP_K):
        pltpu.make_async_copy(y_hbm.at[pl.ds(0, tm * ROW_SUB)], ybuf.at[slot, k], sems.at[slot]).wait()

    cw = cw_ref[...]
    c0, c1 = cw[:, 0:1], cw[:, 1:2]
    lo0, hi0 = _unpack_pair(_load_rows(ybuf.at[slot, 0], 0, tm))
    lo1, hi1 = _unpack_pair(_load_rows(ybuf.at[slot, 1], 0, tm))
    g2 = mod_ref[0, 5:6, :]
    xlo = x_ref[:, :HALF] + g2[:, :HALF] * (lo0 * c0 + lo1 * c1)
    xhi = x_ref[:, HALF:] + g2[:, HALF:] * (hi0 * c0 + hi1 * c1)
    ms = (jnp.sum(xlo * xlo, axis=-1, keepdims=True)
          + jnp.sum(xhi * xhi, axis=-1, keepdims=True)) * (1.0 / D_MODEL)
    r = lax.rsqrt(ms + EPS)
    o_ref[:, :HALF] = xlo * r * g_ref[:, :HALF]
    o_ref[:, HALF:] = xhi * r * g_ref[:, HALF:]


def _final(dest3, x1, cw, mod3, g, ybuf, seq):
    t, d = x1.shape
    tm = FIN_TM
    nt = t // tm
    per_b = seq // tm

    def dspec(shift):
        return pl.BlockSpec((1, 1, TOP_K * tm), lambda i: (jnp.minimum(i + shift, nt - 1), 0, 0),
                            memory_space=pltpu.SMEM)

    return pl.pallas_call(
        _final_kernel,
        grid=(nt,),
        in_specs=[dspec(0), dspec(1),
                  pl.BlockSpec((tm, d), lambda i: (i, 0)),
                  pl.BlockSpec((tm, LANES), lambda i: (i, 0)),
                  pl.BlockSpec((1, 6, d), lambda i: (i // per_b, 0, 0)),
                  pl.BlockSpec((1, d), lambda i: (0, 0)),
                  pl.BlockSpec(memory_space=pl.ANY)],
        out_specs=pl.BlockSpec((tm, d), lambda i: (i, 0)),
        out_shape=jax.ShapeDtypeStruct((t, d), F32),
        scratch_shapes=[pltpu.VMEM((2, TOP_K, tm * ROW_SUB, LANES), U32),
                        pltpu.SemaphoreType.DMA((2,))],
        compiler_params=_cparams(("arbitrary",)),
        name="final",
    )(dest3, dest3, x1, cw, mod3, g, ybuf)


def kernel(x, c, w_ada, b_ada, norm1_g, w_in, conv_w, w_conv_out, w_ret_out, w_o, norm2_g,
           w_router_group, b_router_group, w_router_expert, b_router_expert,
           w_gate, w_up, w_down, norm_f_g):
    batch, seq, d = x.shape
    t = batch * seq
    depth = w_ada.shape[0]
    assert d == D_MODEL and w_in.shape[2] == IN_COLS and w_gate.shape[1] == N_EXPERTS
    assert depth == 1, "the final kernel fuses the last rmsnorm into the single layer"
    n_rows = t * 2 + N_EXPERTS * EXPERT_BLOCK
    n_blk = n_rows // EXPERT_BLOCK
    assert n_blk <= BLK_ROWS

    c_pad = jnp.pad(c, ((0, 8 - batch), (0, 0)))
    xs = x.reshape(t, d)
    for l in range(depth):
        mod = _ada(c_pad, w_ada[l], b_ada[l].reshape(1, -1))[:batch]
        mod3 = mod.reshape(batch, 6, d)

        u = _in_proj(xs, mod3, norm1_g[l].reshape(1, d), w_in[l], seq)
        z = _retention(u, batch, seq)
        x1 = _mix(u, z, conv_w[l], w_conv_out[l].astype(BF16), w_ret_out[l].astype(BF16),
                  w_o[l].astype(BF16), xs, mod3, seq)

        w_r = jnp.pad(jnp.concatenate([w_router_group[l], w_router_expert[l]], axis=1),
                      ((0, 0), (0, LANES - N_GROUPS - N_EXPERTS)))
        b_r = jnp.pad(jnp.concatenate([b_router_group[l], b_router_expert[l]]),
                      (0, LANES - N_GROUPS - N_EXPERTS)).reshape(1, LANES)
        h2, ef, cw = _route(x1, mod3, norm2_g[l].reshape(1, d), w_r, b_r, seq)

        dest, blk = _meta(ef)
        blk_e = blk[:n_blk, 0]
        used = blk[:n_blk, 1]
        prev_e = jnp.concatenate([jnp.full((1,), -1, I32), blk_e[:-1]])
        first = ((blk_e != prev_e) & (used == 1)).astype(I32)
        run_id = jnp.maximum(jnp.cumsum(first) - 1, 0).astype(I32)
        n_runs = jnp.sum(first).astype(I32).reshape(1)
        present = jnp.any((blk_e[None, :] == jnp.arange(N_EXPERTS, dtype=I32)[:, None])
                          & (used[None, :] == 1), axis=1)
        eids = jnp.arange(N_EXPERTS, dtype=I32)
        rank = jnp.cumsum(present.astype(I32)) - 1
        run_e = jnp.sum(jnp.where(present[None, :] & (rank[None, :] == eids[:, None]),
                                  eids[None, :], 0), axis=1).astype(I32)

        dest2 = dest[:, :TOP_K]
        xbuf = _dispatch(blk[:n_blk, 2], dest2.reshape(t // DISP_TM, 1, TOP_K * DISP_TM), h2, n_rows)
        ybuf = _moe(run_e, run_id, used, n_runs, xbuf, w_gate[l], w_up[l], w_down[l])
        xs = _final(dest2.reshape(t // FIN_TM, 1, TOP_K * FIN_TM), x1, cw, mod3,
                    norm_f_g.reshape(1, d), ybuf, seq)
    return xs.reshape(batch, seq, d)
```

```python
import functools

import numpy as np
import jax
import jax.numpy as jnp
from jax import lax
from jax.experimental import pallas as pl
from jax.experimental.pallas import tpu as pltpu

F32 = jnp.float32
BF16 = jnp.bfloat16
I32 = jnp.int32

D_MODEL = 2048
CONV_CH = 1024
CONV_K = 3
RET_HEADS = 8
RET_DK = 128
RET_DV = 256
CHUNK = 128
ROPE_BASE = 10000.0
N_GROUPS = 4
EXPERTS_PER_GROUP = 8
N_EXPERTS = 32
D_EXPERT = 1024
EXPERT_BLOCK = 128
EPS = 1e-6
IN_COLS = 13312
LANES = 128
COL_BLK = 1024
CB_BLK, CC_BLK, CX_BLK, Q_BLK, K_BLK, V_BLK, RG_BLK, GA_BLK, GB_BLK = 0, 1, 2, 3, 4, 5, 7, 9, 11

VMEM_LIMIT = 56 * 1024 * 1024


def _cparams(sem, vmem=VMEM_LIMIT):
    return pltpu.CompilerParams(dimension_semantics=sem, vmem_limit_bytes=vmem)


def _sigmoid(v):
    return 1.0 / (1.0 + jnp.exp(-v))


def _silu(v):
    return v * _sigmoid(v)


def _split_bf16(v):
    hi = v.astype(BF16)
    lo = (v - hi.astype(F32)).astype(BF16)
    return hi, lo


def _ada_kernel(c_ref, w_ref, b_ref, o_ref):
    rows = c_ref.shape[0]
    ch, cl = _split_bf16(_silu(c_ref[...]))
    wh, wl = _split_bf16(w_ref[...])
    p = jnp.dot(jnp.concatenate([ch, cl], axis=0), wh, preferred_element_type=F32)
    o_ref[...] = (p[:rows] + (p[rows:] + jnp.dot(ch, wl, preferred_element_type=F32))) + b_ref[...]


def _ada(c_pad, w, b):
    rows, d = c_pad.shape
    n = w.shape[1]
    tn = 512
    return pl.pallas_call(
        _ada_kernel,
        grid=(n // tn,),
        in_specs=[pl.BlockSpec((rows, d), lambda j: (0, 0)),
                  pl.BlockSpec((d, tn), lambda j: (0, j)),
                  pl.BlockSpec((1, tn), lambda j: (0, j))],
        out_specs=pl.BlockSpec((rows, tn), lambda j: (0, j)),
        out_shape=jax.ShapeDtypeStruct((rows, n), F32),
        compiler_params=_cparams(("arbitrary",)),
        name="ada",
    )(c_pad, w, b)


def _rms_mod(x, g, shift, scale):
    ms = jnp.mean(x * x, axis=-1, keepdims=True)
    xn = x * lax.rsqrt(ms + EPS) * g
    return xn * (1.0 + scale) + shift


NORM_ROWS = 64


def _in_kernel(x_ref, mod_ref, g_ref, w_ref, o_ref, h_ref):
    @pl.when(pl.program_id(1) == 0)
    def _():
        gain = g_ref[...] * (1.0 + mod_ref[0, 1:2, :])
        shift = mod_ref[0, 0:1, :]

        def body(i, _):
            rows = pl.ds(pl.multiple_of(i * NORM_ROWS, NORM_ROWS), NORM_ROWS)
            x = x_ref[rows, :]
            r = lax.rsqrt(jnp.mean(x * x, axis=-1, keepdims=True) + EPS)
            h_ref[rows, :] = (x * r * gain + shift).astype(BF16)
            return 0
        lax.fori_loop(0, x_ref.shape[0] // NORM_ROWS, body, 0)

    o_ref[...] = jnp.dot(h_ref[...], w_ref[...].astype(BF16),
                         preferred_element_type=F32).astype(o_ref.dtype)


def _in_proj(x2, mod3, g, w, seq):
    t, d = x2.shape
    n = w.shape[1]
    tm, tn = 1024, 1024
    per_b = seq // tm
    return pl.pallas_call(
        _in_kernel,
        grid=(t // tm, n // tn),
        in_specs=[pl.BlockSpec((tm, d), lambda i, j: (i, 0)),
                  pl.BlockSpec((1, 6, d), lambda i, j: (i // per_b, 0, 0)),
                  pl.BlockSpec((1, d), lambda i, j: (0, 0)),
                  pl.BlockSpec((d, tn), lambda i, j: (0, j))],
        out_specs=pl.BlockSpec((tm, tn), lambda i, j: (i, j)),
        out_shape=jax.ShapeDtypeStruct((t, n), BF16),
        scratch_shapes=[pltpu.VMEM((tm, d), BF16)],
        compiler_params=_cparams(("arbitrary", "arbitrary")),
        name="in_proj",
    )(x2, mod3, g, w)


RET_CPS = 4


def _ret_kernel(q_ref, k_ref, v0_ref, v1_ref, rg0_ref, rg1_ref, cos_ref, sin_ref,
                idec_ref, qdec_ref, kdec_ref, o_ref, state_ref, *, chunk_decay):
    @pl.when(pl.program_id(1) == 0)
    def _():
        state_ref[...] = jnp.zeros_like(state_ref)

    def rot(t, cos, sin):
        return t * cos + pltpu.roll(t, RET_DK // 2, axis=1) * sin

    for ci in range(RET_CPS):
        rows = slice(ci * CHUNK, (ci + 1) * CHUNK)
        cos = cos_ref[rows, :]
        sin = sin_ref[rows, :]
        for h in range(RET_HEADS):
            v_ref, rg_ref = (v0_ref, rg0_ref) if h < RET_HEADS // 2 else (v1_ref, rg1_ref)
            vc = (h % (RET_HEADS // 2)) * RET_DV
            qh = rot(q_ref[rows, h * RET_DK:(h + 1) * RET_DK].astype(F32), cos, sin)
            kh = rot(k_ref[rows, h * RET_DK:(h + 1) * RET_DK].astype(F32), cos, sin) * (RET_DK ** -0.5)
            vh = v_ref[rows, vc:vc + RET_DV]
            qb = qh.astype(BF16)
            kb = kh.astype(BF16)
            scores = lax.dot_general(qb, kb, (((1,), (1,)), ((), ())),
                                     preferred_element_type=F32) * idec_ref[h]
            inner = jnp.dot(scores.astype(BF16), vh, preferred_element_type=F32)
            state = state_ref[h]
            cross = jnp.dot((qh * qdec_ref[h]).astype(BF16), state.astype(BF16),
                            preferred_element_type=F32)
            kv = lax.dot_general((kh * kdec_ref[h]).astype(BF16), vh, (((0,), (0,)), ((), ())),
                                 preferred_element_type=F32)
            state_ref[h] = state * chunk_decay[h] + kv
            r = inner + cross
            r = r * lax.rsqrt(jnp.mean(r * r, axis=-1, keepdims=True) + EPS)
            gate = _silu(rg_ref[rows, vc:vc + RET_DV].astype(F32))
            o_ref[rows, h * RET_DV:(h + 1) * RET_DV] = (gate * r).astype(o_ref.dtype)


def _retention(u, batch, seq):
    t = u.shape[0]
    nch = seq // CHUNK
    inv = ROPE_BASE ** (-jnp.arange(0, RET_DK, 2, dtype=F32) / RET_DK)
    ang = jnp.arange(seq, dtype=F32)[:, None] * inv[None, :]
    cos, sin = jnp.cos(ang), jnp.sin(ang)
    cos_t = jnp.concatenate([cos, cos], axis=-1)
    sin_t = jnp.concatenate([-sin, sin], axis=-1)
    log_g = jnp.log1p(-(2.0 ** (-5.0 - jnp.arange(RET_HEADS, dtype=F32))))
    i = jnp.arange(CHUNK, dtype=F32)
    diff = i[:, None] - i[None, :]
    idec = jnp.where(diff >= 0, jnp.exp(log_g[:, None, None] * jnp.maximum(diff, 0.0)), 0.0)
    qdec = jnp.broadcast_to(jnp.exp(log_g[:, None] * (i + 1.0))[:, :, None],
                            (RET_HEADS, CHUNK, RET_DK))
    kdec = jnp.broadcast_to(jnp.exp(log_g[:, None] * (CHUNK - 1.0 - i))[:, :, None],
                            (RET_HEADS, CHUNK, RET_DK))
    log_g_np = np.log1p(-(2.0 ** (-5.0 - np.arange(RET_HEADS, dtype=np.float32)))).astype(np.float32)
    chunk_decay = tuple(float(np.exp(np.float32(lg * np.float32(CHUNK)))) for lg in log_g_np)

    rows = RET_CPS * CHUNK
    nst = seq // rows

    def ublk(c):
        return pl.BlockSpec((rows, COL_BLK), lambda b, n, c=c: (b * nst + n, c))

    def table():
        return pl.BlockSpec((RET_HEADS, CHUNK, RET_DK), lambda b, n: (0, 0, 0))

    return pl.pallas_call(
        functools.partial(_ret_kernel, chunk_decay=chunk_decay),
        grid=(batch, nst),
        in_specs=[ublk(Q_BLK), ublk(K_BLK), ublk(V_BLK), ublk(V_BLK + 1),
                  ublk(RG_BLK), ublk(RG_BLK + 1),
                  pl.BlockSpec((rows, RET_DK), lambda b, n: (n, 0)),
                  pl.BlockSpec((rows, RET_DK), lambda b, n: (n, 0)),
                  table(), table(), table()],
        out_specs=pl.BlockSpec((rows, RET_HEADS * RET_DV), lambda b, n: (b * nst + n, 0)),
        out_shape=jax.ShapeDtypeStruct((t, RET_HEADS * RET_DV), BF16),
        scratch_shapes=[pltpu.VMEM((RET_HEADS, RET_DK, RET_DV), F32)],
        compiler_params=_cparams(("arbitrary", "arbitrary")),
        name="retention",
    )(u, u, u, u, u, u, cos_t, sin_t, idec, qdec, kdec)


HALO = 8


MIX_TN = 512


def _mix_kernel(cb_ref, cc_ref, cx_ref, hcc_ref, hcx_ref, z_ref, ga0_ref, ga1_ref, gb0_ref, gb1_ref,
                cw_ref, wc_ref, wr_ref, wo_ref, x_ref, mod_ref, modp_ref, g2_ref, wrt_ref, brt_ref,
                o_ref, hp_ref, ef_ref, rw_ref, m_ref, mg_ref, x1p_ref, *, tiles_per_seq, n_tiles):
    tm = cb_ref.shape[0]
    d = o_ref.shape[1]
    step = pl.program_id(0)

    slot = step % 2

    @pl.when(step == 0)
    def _():
        x1p_ref[1] = jnp.zeros(x1p_ref.shape[1:], F32)

    m = cc_ref[...].astype(F32) * cx_ref[...].astype(F32)
    halo = hcc_ref[...].astype(F32) * hcx_ref[...].astype(F32)
    first = (jnp.minimum(step, n_tiles - 1) % tiles_per_seq) == 0
    m_ref[0:HALO, :] = jnp.where(first, 0.0, halo)
    m_ref[HALO:HALO + tm, :] = m
    conv = (m_ref[HALO - 2:HALO - 2 + tm, :] * cw_ref[0:1, :]
            + m_ref[HALO - 1:HALO - 1 + tm, :] * cw_ref[1:2, :]
            + m * cw_ref[2:3, :])
    p = (cb_ref[...].astype(F32) * conv).astype(BF16)
    z = z_ref[...]
    nchunk = d // MIX_TN
    prev = 1 - slot
    ssq = jnp.zeros((tm, 1), F32)
    for n in range(nchunk):
        cols = slice(n * MIX_TN, (n + 1) * MIX_TN)
        ga_ref, gb_ref = (ga0_ref, gb0_ref) if n < COL_BLK // MIX_TN else (ga1_ref, gb1_ref)
        gcols = slice((n * MIX_TN) % COL_BLK, (n * MIX_TN) % COL_BLK + MIX_TN)
        ya = jnp.dot(p, wc_ref[:, cols], preferred_element_type=F32)
        yb = jnp.dot(z, wr_ref[:, cols], preferred_element_type=F32)
        merged = (_sigmoid(ga_ref[:, gcols].astype(F32)) * ya
                  + _sigmoid(gb_ref[:, gcols].astype(F32)) * yb)
        mg_ref[:, cols] = merged.astype(BF16)
        xp = x1p_ref[prev, :, cols]
        ssq = ssq + jnp.sum(xp * xp, axis=-1, keepdims=True)
    rnorm = lax.rsqrt(ssq * (1.0 / d) + EPS)
    gain = g2_ref[...] * (1.0 + modp_ref[0, 4:5, :])
    shift = modp_ref[0, 3:4, :]
    wh, wl = _split_bf16(wrt_ref[...])
    whl = jnp.concatenate([wh, wl], axis=1)
    acc_hw = jnp.zeros((tm, 2 * LANES), F32)
    acc_lw = jnp.zeros((tm, LANES), F32)
    mg = mg_ref[...]
    for n in range(nchunk):
        cols = slice(n * MIX_TN, (n + 1) * MIX_TN)
        acc = jnp.dot(mg, wo_ref[:, cols], preferred_element_type=F32)
        x1 = x_ref[:, cols] + mod_ref[0, 2:3, cols] * acc
        o_ref[:, cols] = x1
        x1p_ref[slot, :, cols] = x1
        if n % 2 == 0:
            c = (n // 2) * MIX_TN
            halves = []
            for c0 in (c, c + HALF):
                hcols = slice(c0, c0 + MIX_TN)
                h2 = x1p_ref[prev, :, hcols] * rnorm * gain[:, hcols] + shift[:, hcols]
                hh, hl = _split_bf16(h2)
                acc_hw = acc_hw + jnp.dot(hh, whl[hcols, :], preferred_element_type=F32)
                acc_lw = acc_lw + jnp.dot(hl, wh[hcols, :], preferred_element_type=F32)
                halves.append(h2)
            hp_ref[:, c:c + MIX_TN] = _pack_pair(halves[0], halves[1])
    logits = (acc_hw[:, :LANES] + (acc_hw[:, LANES:] + acc_lw)) + brt_ref[...]
    ef_ref[...], rw_ref[...] = _route_from_logits(logits)


def _mix(u, z, conv_w, wc_bf, wr_bf, wo_bf, x2, mod3, g2, w_rt, b_rt, seq):
    t, d = x2.shape
    tm = 256
    nt = t // tm
    hb = tm // HALO
    per_b = seq // tm

    def cur(i):
        return jnp.minimum(i, nt - 1)

    def prev(i):
        return jnp.maximum(i - 1, 0)

    def ublk(c):
        return pl.BlockSpec((tm, COL_BLK), lambda i, c=c: (cur(i), c))

    def halo(c):
        return pl.BlockSpec((HALO, COL_BLK), lambda i, c=c: (jnp.maximum(cur(i) * hb - 1, 0), c))

    def resident(shape):
        return pl.BlockSpec(shape, lambda i: (0, 0), pipeline_mode=pl.Buffered(1))

    return pl.pallas_call(
        functools.partial(_mix_kernel, tiles_per_seq=per_b, n_tiles=nt),
        grid=(nt + 1,),
        in_specs=[ublk(CB_BLK), ublk(CC_BLK), ublk(CX_BLK), halo(CC_BLK), halo(CX_BLK),
                  pl.BlockSpec((tm, z.shape[1]), lambda i: (cur(i), 0)),
                  ublk(GA_BLK), ublk(GA_BLK + 1), ublk(GB_BLK), ublk(GB_BLK + 1),
                  pl.BlockSpec((CONV_K, CONV_CH), lambda i: (0, 0)),
                  resident(wc_bf.shape), resident(wr_bf.shape), resident(wo_bf.shape),
                  pl.BlockSpec((tm, d), lambda i: (cur(i), 0)),
                  pl.BlockSpec((1, 6, d), lambda i: (cur(i) // per_b, 0, 0)),
                  pl.BlockSpec((1, 6, d), lambda i: (prev(i) // per_b, 0, 0)),
                  pl.BlockSpec((1, d), lambda i: (0, 0)),
                  pl.BlockSpec((d, LANES), lambda i: (0, 0)),
                  pl.BlockSpec((1, LANES), lambda i: (0, 0))],
        out_specs=[pl.BlockSpec((tm, d), lambda i: (cur(i), 0)),
                   pl.BlockSpec((tm, HALF), lambda i: (prev(i), 0)),
                   pl.BlockSpec((tm, LANES), lambda i: (prev(i), 0)),
                   pl.BlockSpec((tm, LANES), lambda i: (prev(i), 0))],
        out_shape=[jax.ShapeDtypeStruct((t, d), F32),
                   jax.ShapeDtypeStruct((t, HALF), U32),
                   jax.ShapeDtypeStruct((t, LANES), F32),
                   jax.ShapeDtypeStruct((t, LANES), F32)],
        scratch_shapes=[pltpu.VMEM((tm + HALO, CONV_CH), F32),
                        pltpu.VMEM((tm, d), BF16),
                        pltpu.VMEM((2, tm, d), F32)],
        compiler_params=_cparams(("arbitrary",)),
        name="mix",
    )(u, u, u, u, u, z, u, u, u, u, conv_w, wc_bf, wr_bf, wo_bf, x2, mod3, mod3, g2, w_rt, b_rt)


ROUTE_E0 = N_GROUPS


U32 = jnp.uint32
HALF = D_MODEL // 2


def _pack_pair(lo, hi):
    ulo = lax.bitcast_convert_type(lo.astype(BF16).astype(F32), U32)
    uhi = lax.bitcast_convert_type(hi.astype(BF16).astype(F32), U32)
    return (ulo >> 16) | uhi


def _unpack_pair(w):
    lo = lax.bitcast_convert_type(w << 16, F32)
    hi = lax.bitcast_convert_type(w & jnp.uint32(0xFFFF0000), F32)
    return lo, hi


ROW_SUB = HALF // LANES


def _row_slab(r):
    if isinstance(r, int):
        return pl.ds(r * ROW_SUB, ROW_SUB)
    return pl.ds(pl.multiple_of(r * ROW_SUB, ROW_SUB), ROW_SUB)


def _store_rows(ref, row0, v):
    for c in range(ROW_SUB):
        ref[pl.ds(row0 * ROW_SUB + c, v.shape[0], stride=ROW_SUB), :] = v[:, c * LANES:(c + 1) * LANES]


def _load_rows(ref, row0, n):
    return jnp.concatenate(
        [ref[pl.ds(row0 * ROW_SUB + c, n, stride=ROW_SUB), :] for c in range(ROW_SUB)], axis=1)


def _route_from_logits(logits):
    lane = lax.broadcasted_iota(I32, logits.shape, 1).astype(F32)
    neg = -jnp.inf
    big = float(LANES)
    gl = jnp.where(lane < N_GROUPS, logits, neg)
    gmax = jnp.max(gl, axis=-1, keepdims=True)
    gsel = jnp.min(jnp.where(gl == gmax, lane, big), axis=-1, keepdims=True)
    gsum = jnp.sum(jnp.exp(gl - gmax), axis=-1, keepdims=True)
    g_w = 1.0 / gsum
    lo_lane = ROUTE_E0 + EXPERTS_PER_GROUP * gsel
    emask = (lane >= lo_lane) & (lane < lo_lane + EXPERTS_PER_GROUP)
    el = jnp.where(emask, logits, neg)
    m1 = jnp.max(el, axis=-1, keepdims=True)
    i1 = jnp.min(jnp.where(el == m1, lane, big), axis=-1, keepdims=True)
    el2 = jnp.where(lane == i1, neg, el)
    m2 = jnp.max(el2, axis=-1, keepdims=True)
    i2 = jnp.min(jnp.where(el2 == m2, lane, big), axis=-1, keepdims=True)
    esum = jnp.sum(jnp.exp(el - m1), axis=-1, keepdims=True)
    p1 = 1.0 / esum
    p2 = jnp.exp(m2 - m1) / esum
    tot = p1 + p2
    c1 = g_w * (p1 / tot)
    c2 = g_w * (p2 / tot)
    ef = jnp.where(lane == 0.0, i1 - ROUTE_E0, jnp.where(lane == 1.0, i2 - ROUTE_E0, 0.0))
    cw = jnp.where(lane == 0.0, c1, jnp.where(lane == 1.0, c2, 0.0))
    return ef, cw


META_TB = 512
BLK_ROWS = 256


def _meta_kernel(ef_ref, dest_ref, blk_ref, pref_ref):
    t = ef_ref.shape[0]
    nb = t // META_TB
    lane = lax.broadcasted_iota(I32, (META_TB, LANES), 1).astype(F32)
    r_i = lax.broadcasted_iota(I32, (META_TB, META_TB), 0)
    c_i = lax.broadcasted_iota(I32, (META_TB, META_TB), 1)
    tril = jnp.where(r_i > c_i, 1.0, 0.0).astype(BF16)

    def onehots(i):
        ef = ef_ref[pl.ds(i * META_TB, META_TB), :]
        oh1 = jnp.where(lane == ef[:, 0:1], 1.0, 0.0)
        oh2 = jnp.where(lane == ef[:, 1:2], 1.0, 0.0)
        return oh1, oh2

    def pass1(i, carry):
        oh1, oh2 = onehots(i)
        cnt = oh1 + oh2
        pref = jnp.dot(tril, cnt.astype(BF16), preferred_element_type=F32) + carry
        pref_ref[pl.ds(i * META_TB, META_TB), :] = pref
        return carry + jnp.sum(cnt, axis=0, keepdims=True)

    counts = lax.fori_loop(0, nb, pass1, jnp.zeros((1, LANES), F32))
    nblk = jnp.floor((counts + (EXPERT_BLOCK - 1)) * (1.0 / EXPERT_BLOCK))
    u_r = lax.broadcasted_iota(I32, (LANES, LANES), 0)
    u_c = lax.broadcasted_iota(I32, (LANES, LANES), 1)
    upper = jnp.where(u_r <= u_c, 1.0, 0.0).astype(BF16)
    nblk8 = jnp.broadcast_to(nblk, (8, LANES))
    pend_blk = jnp.dot(nblk8.astype(BF16), upper, preferred_element_type=F32)[0:1, :]
    pstart_blk = pend_blk - nblk
    pstart = pstart_blk * float(EXPERT_BLOCK)

    def pass2(i, _):
        oh1, oh2 = onehots(i)
        base = pref_ref[pl.ds(i * META_TB, META_TB), :] + pstart
        d1 = jnp.sum(oh1 * base, axis=-1, keepdims=True)
        d2 = jnp.sum(oh2 * base, axis=-1, keepdims=True)
        dest = jnp.where(lane == 0.0, d1, jnp.where(lane == 1.0, d2, 0.0))
        dest_ref[pl.ds(i * META_TB, META_TB), :] = dest.astype(I32)
        return 0

    lax.fori_loop(0, nb, pass2, 0)

    blane = lax.broadcasted_iota(I32, (BLK_ROWS, LANES), 1).astype(F32)
    brow = lax.broadcasted_iota(I32, (BLK_ROWS, LANES), 0).astype(F32)
    emask = blane < N_EXPERTS
    owner = jnp.sum(jnp.where(emask & (pend_blk <= brow), 1.0, 0.0), axis=-1, keepdims=True)
    owner = jnp.minimum(owner, float(N_EXPERTS - 1))
    total = jnp.sum(jnp.where(emask, nblk, 0.0), axis=-1, keepdims=True)
    used = jnp.where(brow < total, 1.0, 0.0)
    mine = emask & (blane == owner)
    cnt_b = jnp.sum(jnp.where(mine, counts, 0.0), axis=-1, keepdims=True)
    ps_b = jnp.sum(jnp.where(mine, pstart_blk, 0.0), axis=-1, keepdims=True)
    nvalid = jnp.clip(cnt_b - (brow - ps_b) * float(EXPERT_BLOCK), 0.0, float(EXPERT_BLOCK))
    pad_lo = nvalid * used
    blk_ref[...] = jnp.where(blane == 0.0, owner,
                             jnp.where(blane == 1.0, used,
                                       jnp.where(blane == 2.0, pad_lo, 0.0))).astype(I32)


def _meta(ef):
    t = ef.shape[0]
    return pl.pallas_call(
        _meta_kernel,
        out_shape=[jax.ShapeDtypeStruct((t, LANES), I32),
                   jax.ShapeDtypeStruct((BLK_ROWS, LANES), I32)],
        scratch_shapes=[pltpu.VMEM((t, LANES), F32)],
        compiler_params=pltpu.CompilerParams(vmem_limit_bytes=VMEM_LIMIT),
        name="meta",
    )(ef)


DISP_TM = 256
TOP_K = 2


def _dispatch_kernel(pad_lo_ref, dest_ref, h_ref, xbuf_ref, stage, zrow, sems, zsem):
    i = pl.program_id(0)
    n = pl.num_programs(0)
    tm = h_ref.shape[0]
    n_blk = pad_lo_ref.shape[0]
    slot = i % 2

    def pad_rows(action):
        def blk(b, _):
            lo = pad_lo_ref[b]
            base = pl.multiple_of(b * EXPERT_BLOCK, EXPERT_BLOCK)

            @pl.when(lo == 0)
            def _():
                whole = pl.ds(pl.multiple_of(base * ROW_SUB, EXPERT_BLOCK * ROW_SUB),
                              EXPERT_BLOCK * ROW_SUB)
                action(pltpu.make_async_copy(zrow, xbuf_ref.at[whole], zsem))

            @pl.when(lo > 0)
            def _():
                def row(r, _):
                    action(pltpu.make_async_copy(zrow.at[_row_slab(0)],
                                                 xbuf_ref.at[_row_slab(base + r)], zsem))
                    return 0
                lax.fori_loop(lo, EXPERT_BLOCK, row, 0)
            return 0
        lax.fori_loop(0, n_blk, blk, 0)

    @pl.when(i == 0)
    def _():
        zrow[...] = jnp.zeros_like(zrow)
        pad_rows(lambda cp: cp.start())

    _store_rows(stage.at[slot], 0, h_ref[...])
    for j in range(tm):
        for k in range(TOP_K):
            d = dest_ref[0, 0, TOP_K * j + k]
            pltpu.make_async_copy(stage.at[slot, _row_slab(j)], xbuf_ref.at[_row_slab(d)],
                                  sems.at[slot]).start(priority=k)

    def wait_tile(s):
        for _ in range(TOP_K):
            pltpu.make_async_copy(stage.at[s], xbuf_ref.at[pl.ds(0, tm * ROW_SUB)], sems.at[s]).wait()

    @pl.when(i > 0)
    def _():
        wait_tile(1 - slot)

    @pl.when(i == n - 1)
    def _():
        wait_tile(slot)
        pad_rows(lambda cp: cp.wait())


def _dispatch(pad_lo, dest3, hp, n_rows):
    t, w = hp.shape
    tm = DISP_TM
    grid_spec = pltpu.PrefetchScalarGridSpec(
        num_scalar_prefetch=1,
        grid=(t // tm,),
        in_specs=[pl.BlockSpec((1, 1, TOP_K * tm), lambda i, *_: (i, 0, 0), memory_space=pltpu.SMEM),
                  pl.BlockSpec((tm, w), lambda i, *_: (i, 0))],
        out_specs=pl.BlockSpec(memory_space=pl.ANY),
        scratch_shapes=[pltpu.VMEM((2, tm * ROW_SUB, LANES), U32),
                        pltpu.VMEM((EXPERT_BLOCK * ROW_SUB, LANES), U32),
                        pltpu.SemaphoreType.DMA((2,)), pltpu.SemaphoreType.DMA(())],
    )
    return pl.pallas_call(
        _dispatch_kernel,
        grid_spec=grid_spec,
        out_shape=jax.ShapeDtypeStruct((n_rows * ROW_SUB, LANES), U32),
        compiler_params=_cparams(("arbitrary",)),
        name="dispatch",
    )(pad_lo, dest3, hp)


MOE_SUB = 2
W_RING = 2
CAST_ROWS = 256
SWITCH_AFTER = 4


def _moe_kernel(run_e_ref, run_id_ref, used_ref, nruns_ref, x_ref, wg_hbm, wu_hbm, wd_hbm, o_ref,
                wg_st, wu_st, wd_st, wg_bf, wu_bf, wd_bf, sems, state):
    step = pl.program_id(0)
    n_runs = nruns_ref[0]
    streams = ((wg_hbm, wg_st, wg_bf), (wu_hbm, wu_st, wu_bf), (wd_hbm, wd_st, wd_bf))

    def weight_copy(m, run):
        hbm, stage, _ = streams[m]
        return pltpu.make_async_copy(hbm.at[run_e_ref[run]], stage, sems.at[m])

    @pl.when(step == 0)
    def _():
        state[0] = 0
        state[1] = 0
        for m in range(len(streams)):
            weight_copy(m, 0).start(priority=1)

    def switch_to(run, b):
        ring = lax.rem(run, W_RING)
        for m, (_, stage, bf) in enumerate(streams):
            weight_copy(m, run).wait()

            def body(i, _, stage=stage, bf=bf):
                rows = pl.ds(pl.multiple_of(i * CAST_ROWS, CAST_ROWS), CAST_ROWS)
                bf[ring, rows, :] = stage[rows, :].astype(BF16)
                return 0
            lax.fori_loop(0, stage.shape[0] // CAST_ROWS, body, 0)

            @pl.when(run + 1 < n_runs)
            def _(m=m):
                weight_copy(m, run + 1).start(priority=1)
        state[0] = run + 1
        state[1] = b + SWITCH_AFTER

    def run_block(b, row0):
        run = run_id_ref[b]
        pl.when(state[0] == run)(functools.partial(switch_to, run, b))
        pl.when((state[0] == run + 1) & (run + 1 < n_runs) & (b >= state[1]))(
            functools.partial(switch_to, run + 1, b))

        ring = lax.rem(run, W_RING)
        xlo, xhi = _unpack_pair(_load_rows(x_ref, row0, EXPERT_BLOCK))
        xb = jnp.concatenate([xlo.astype(BF16), xhi.astype(BF16)], axis=1)
        g = jnp.dot(xb, wg_bf[ring], preferred_element_type=F32)
        u = jnp.dot(xb, wu_bf[ring], preferred_element_type=F32)
        hmid = (_silu(g) * u).astype(BF16)
        y = jnp.dot(hmid, wd_bf[ring], preferred_element_type=F32)
        _store_rows(o_ref, row0, _pack_pair(y[:, :HALF], y[:, HALF:]))

    for j in range(MOE_SUB):
        b = step * MOE_SUB + j
        row0 = j * EXPERT_BLOCK
        pl.when(used_ref[b] == 1)(functools.partial(run_block, b, row0))

        @pl.when(used_ref[b] == 0)
        def _():
            o_ref[pl.ds(row0 * ROW_SUB, EXPERT_BLOCK * ROW_SUB), :] = jnp.zeros(
                (EXPERT_BLOCK * ROW_SUB, LANES), U32)


def _moe(run_e, run_id, used, n_runs, xbuf, w_gate, w_up, w_down):
    r = xbuf.shape[0]
    rows = MOE_SUB * EXPERT_BLOCK * ROW_SUB
    d, de = w_gate.shape[1:]

    def xmap(s, run_e_ref, run_id_ref, used_ref, nruns_ref):
        del run_e_ref, run_id_ref, nruns_ref
        return (jnp.where(used_ref[s * MOE_SUB] == 1, s, 0), 0)

    grid_spec = pltpu.PrefetchScalarGridSpec(
        num_scalar_prefetch=4,
        grid=(r // rows,),
        in_specs=[pl.BlockSpec((rows, LANES), xmap),
                  pl.BlockSpec(memory_space=pl.ANY),
                  pl.BlockSpec(memory_space=pl.ANY),
                  pl.BlockSpec(memory_space=pl.ANY)],
        out_specs=pl.BlockSpec((rows, LANES), lambda s, *_: (s, 0)),
        scratch_shapes=[pltpu.VMEM((d, de), F32), pltpu.VMEM((d, de), F32), pltpu.VMEM((de, d), F32),
                        pltpu.VMEM((W_RING, d, de), BF16), pltpu.VMEM((W_RING, d, de), BF16),
                        pltpu.VMEM((W_RING, de, d), BF16),
                        pltpu.SemaphoreType.DMA((3,)), pltpu.SMEM((2,), I32)],
    )
    return pl.pallas_call(
        _moe_kernel,
        grid_spec=grid_spec,
        out_shape=jax.ShapeDtypeStruct((r, LANES), U32),
        compiler_params=_cparams(("arbitrary",), vmem=60 * 1024 * 1024),
        name="moe",
    )(run_e, run_id, used, n_runs, xbuf, w_gate, w_up, w_down)


FIN_TM = 256


def _final_kernel(dest_ref, dnext_ref, x_ref, cw_ref, mod_ref, g_ref, y_hbm, o_ref, ybuf, sems):
    i = pl.program_id(0)
    n = pl.num_programs(0)
    tm = x_ref.shape[0]
    slot = i % 2

    def gather(d_ref, s):
        for j in range(tm):
            for k in range(TOP_K):
                d = d_ref[0, 0, TOP_K * j + k]
                pltpu.make_async_copy(y_hbm.at[_row_slab(d)], ybuf.at[s, k, _row_slab(j)],
                                      sems.at[s]).start(priority=k)

    @pl.when(i == 0)
    def _():
        gather(dest_ref, 0)

    @pl.when(i + 1 < n)
    def _():
        gather(dnext_ref, 1 - slot)

    for k in range(TOP_K):
        pltpu.make_async_copy(y_hbm.at[pl.ds(0, tm * ROW_SUB)], ybuf.at[slot, k], sems.at[slot]).wait()

    cw = cw_ref[...]
    c0, c1 = cw[:, 0:1], cw[:, 1:2]
    lo0, hi0 = _unpack_pair(_load_rows(ybuf.at[slot, 0], 0, tm))
    lo1, hi1 = _unpack_pair(_load_rows(ybuf.at[slot, 1], 0, tm))
    g2 = mod_ref[0, 5:6, :]
    xlo = x_ref[:, :HALF] + g2[:, :HALF] * (lo0 * c0 + lo1 * c1)
    xhi = x_ref[:, HALF:] + g2[:, HALF:] * (hi0 * c0 + hi1 * c1)
    ms = (jnp.sum(xlo * xlo, axis=-1, keepdims=True)
          + jnp.sum(xhi * xhi, axis=-1, keepdims=True)) * (1.0 / D_MODEL)
    r = lax.rsqrt(ms + EPS)
    o_ref[:, :HALF] = xlo * r * g_ref[:, :HALF]
    o_ref[:, HALF:] = xhi * r * g_ref[:, HALF:]


def _final(dest3, x1, cw, mod3, g, ybuf, seq):
    t, d = x1.shape
    tm = FIN_TM
    nt = t // tm
    per_b = seq // tm

    def dspec(shift):
        return pl.BlockSpec((1, 1, TOP_K * tm), lambda i: (jnp.minimum(i + shift, nt - 1), 0, 0),
                            memory_space=pltpu.SMEM)

    return pl.pallas_call(
        _final_kernel,
        grid=(nt,),
        in_specs=[dspec(0), dspec(1),
                  pl.BlockSpec((tm, d), lambda i: (i, 0)),
                  pl.BlockSpec((tm, LANES), lambda i: (i, 0)),
                  pl.BlockSpec((1, 6, d), lambda i: (i // per_b, 0, 0)),
                  pl.BlockSpec((1, d), lambda i: (0, 0)),
                  pl.BlockSpec(memory_space=pl.ANY)],
        out_specs=pl.BlockSpec((tm, d), lambda i: (i, 0)),
        out_shape=jax.ShapeDtypeStruct((t, d), F32),
        scratch_shapes=[pltpu.VMEM((2, TOP_K, tm * ROW_SUB, LANES), U32),
                        pltpu.SemaphoreType.DMA((2,))],
        compiler_params=_cparams(("arbitrary",)),
        name="final",
    )(dest3, dest3, x1, cw, mod3, g, ybuf)


def kernel(x, c, w_ada, b_ada, norm1_g, w_in, conv_w, w_conv_out, w_ret_out, w_o, norm2_g,
           w_router_group, b_router_group, w_router_expert, b_router_expert,
           w_gate, w_up, w_down, norm_f_g):
    batch, seq, d = x.shape
    t = batch * seq
    depth = w_ada.shape[0]
    assert d == D_MODEL and w_in.shape[2] == IN_COLS and w_gate.shape[1] == N_EXPERTS
    assert depth == 1, "the final kernel fuses the last rmsnorm into the single layer"
    n_rows = t * 2 + N_EXPERTS * EXPERT_BLOCK
    n_blk = n_rows // EXPERT_BLOCK
    assert n_blk <= BLK_ROWS

    c_pad = jnp.pad(c, ((0, 8 - batch), (0, 0)))
    xs = x.reshape(t, d)
    for l in range(depth):
        mod = _ada(c_pad, w_ada[l], b_ada[l].reshape(1, -1))[:batch]
        mod3 = mod.reshape(batch, 6, d)

        u = _in_proj(xs, mod3, norm1_g[l].reshape(1, d), w_in[l], seq)
        z = _retention(u, batch, seq)
        w_r = jnp.pad(jnp.concatenate([w_router_group[l], w_router_expert[l]], axis=1),
                      ((0, 0), (0, LANES - N_GROUPS - N_EXPERTS)))
        b_r = jnp.pad(jnp.concatenate([b_router_group[l], b_router_expert[l]]),
                      (0, LANES - N_GROUPS - N_EXPERTS)).reshape(1, LANES)
        x1, h2, ef, cw = _mix(u, z, conv_w[l], w_conv_out[l].astype(BF16), w_ret_out[l].astype(BF16),
                              w_o[l].astype(BF16), xs, mod3, norm2_g[l].reshape(1, d), w_r, b_r, seq)

        dest, blk = _meta(ef)
        blk_e = blk[:n_blk, 0]
        used = blk[:n_blk, 1]
        prev_e = jnp.concatenate([jnp.full((1,), -1, I32), blk_e[:-1]])
        first = ((blk_e != prev_e) & (used == 1)).astype(I32)
        run_id = jnp.maximum(jnp.cumsum(first) - 1, 0).astype(I32)
        n_runs = jnp.sum(first).astype(I32).reshape(1)
        present = jnp.any((blk_e[None, :] == jnp.arange(N_EXPERTS, dtype=I32)[:, None])
                          & (used[None, :] == 1), axis=1)
        eids = jnp.arange(N_EXPERTS, dtype=I32)
        rank = jnp.cumsum(present.astype(I32)) - 1
        run_e = jnp.sum(jnp.where(present[None, :] & (rank[None, :] == eids[:, None]),
                                  eids[None, :], 0), axis=1).astype(I32)

        dest2 = dest[:, :TOP_K]
        xbuf = _dispatch(blk[:n_blk, 2], dest2.reshape(t // DISP_TM, 1, TOP_K * DISP_TM), h2, n_rows)
        ybuf = _moe(run_e, run_id, used, n_runs, xbuf, w_gate[l], w_up[l], w_down[l])
        xs = _final(dest2.reshape(t // FIN_TM, 1, TOP_K * FIN_TM), x1, cw, mod3,
                    norm_f_g.reshape(1, d), ybuf, seq)
    return xs.reshape(batch, seq, d)
```

```python
import functools

import numpy as np
import jax
import jax.numpy as jnp
from jax import lax
from jax.experimental import pallas as pl
from jax.experimental.pallas import tpu as pltpu

F32 = jnp.float32
BF16 = jnp.bfloat16
I32 = jnp.int32

D_MODEL = 2048
CONV_CH = 1024
CONV_K = 3
RET_HEADS = 8
RET_DK = 128
RET_DV = 256
CHUNK = 128
ROPE_BASE = 10000.0
N_GROUPS = 4
EXPERTS_PER_GROUP = 8
N_EXPERTS = 32
D_EXPERT = 1024
EXPERT_BLOCK = 128
EPS = 1e-6
IN_COLS = 13312
LANES = 128
COL_BLK = 1024
CB_BLK, CC_BLK, CX_BLK, Q_BLK, K_BLK, V_BLK, RG_BLK, GA_BLK, GB_BLK = 0, 1, 2, 3, 4, 5, 7, 9, 11

VMEM_LIMIT = 56 * 1024 * 1024


def _cparams(sem, vmem=VMEM_LIMIT):
    return pltpu.CompilerParams(dimension_semantics=sem, vmem_limit_bytes=vmem)


def _sigmoid(v):
    return 1.0 / (1.0 + jnp.exp(-v))


def _silu(v):
    return v * _sigmoid(v)


def _split_bf16(v):
    hi = v.astype(BF16)
    lo = (v - hi.astype(F32)).astype(BF16)
    return hi, lo


def _ada_kernel(c_ref, w_ref, b_ref, o_ref):
    rows = c_ref.shape[0]
    ch, cl = _split_bf16(_silu(c_ref[...]))
    wh, wl = _split_bf16(w_ref[...])
    p = jnp.dot(jnp.concatenate([ch, cl], axis=0), wh, preferred_element_type=F32)
    o_ref[...] = (p[:rows] + (p[rows:] + jnp.dot(ch, wl, preferred_element_type=F32))) + b_ref[...]


def _ada(c_pad, w, b):
    rows, d = c_pad.shape
    n = w.shape[1]
    tn = 1024
    return pl.pallas_call(
        _ada_kernel,
        grid=(n // tn,),
        in_specs=[pl.BlockSpec((rows, d), lambda j: (0, 0)),
                  pl.BlockSpec((d, tn), lambda j: (0, j)),
                  pl.BlockSpec((1, tn), lambda j: (0, j))],
        out_specs=pl.BlockSpec((rows, tn), lambda j: (0, j)),
        out_shape=jax.ShapeDtypeStruct((rows, n), F32),
        compiler_params=_cparams(("arbitrary",)),
        name="ada",
    )(c_pad, w, b)


def _rms_mod(x, g, shift, scale):
    ms = jnp.mean(x * x, axis=-1, keepdims=True)
    xn = x * lax.rsqrt(ms + EPS) * g
    return xn * (1.0 + scale) + shift


NORM_ROWS = 64


IN_SPLIT = 2


def _in_kernel(x_ref, mod_ref, g_ref, w_ref, o_ref, h_ref):
    i, j = pl.program_id(0), pl.program_id(1)
    nj = pl.num_programs(1)
    tm, tn = o_ref.shape
    slot = i % 2

    def norm_rows(dst, rows):
        gain = g_ref[...] * (1.0 + mod_ref[0, 1:2, :])
        x = x_ref[rows, :]
        r = lax.rsqrt(jnp.mean(x * x, axis=-1, keepdims=True) + EPS)
        h_ref[dst, rows, :] = (x * r * gain + mod_ref[0, 0:1, :]).astype(BF16)

    @pl.when((i == 0) & (j == 0))
    def _():
        def body(k, _):
            norm_rows(0, pl.ds(pl.multiple_of(k * NORM_ROWS, NORM_ROWS), NORM_ROWS))
            return 0
        lax.fori_loop(0, tm // NORM_ROWS, body, 0)

    @pl.when(j < nj - 1)
    def _():
        o_ref[...] = jnp.dot(h_ref[slot], w_ref[...].astype(BF16),
                             preferred_element_type=F32).astype(o_ref.dtype)

    @pl.when(j == nj - 1)
    def _():
        h = h_ref[slot]
        cw, rw = tn // IN_SPLIT, tm // IN_SPLIT
        for q in range(IN_SPLIT):
            cols = slice(q * cw, (q + 1) * cw)
            o_ref[:, cols] = jnp.dot(h, w_ref[:, cols].astype(BF16),
                                     preferred_element_type=F32).astype(o_ref.dtype)
            for k in range(rw // NORM_ROWS):
                norm_rows(1 - slot, slice(q * rw + k * NORM_ROWS, q * rw + (k + 1) * NORM_ROWS))


def _in_proj(x2, mod3, g, w, seq):
    t, d = x2.shape
    n = w.shape[1]
    tm, tn = 1024, 1024
    nt, nj = t // tm, n // tn
    per_b = seq // tm

    def xtile(i, j):
        return jnp.minimum(i + (j == nj - 1).astype(jnp.int32), nt - 1)

    return pl.pallas_call(
        _in_kernel,
        grid=(nt, nj),
        in_specs=[pl.BlockSpec((tm, d), lambda i, j: (xtile(i, j), 0)),
                  pl.BlockSpec((1, 6, d), lambda i, j: (xtile(i, j) // per_b, 0, 0)),
                  pl.BlockSpec((1, d), lambda i, j: (0, 0)),
                  pl.BlockSpec((d, tn), lambda i, j: (0, j))],
        out_specs=pl.BlockSpec((tm, tn), lambda i, j: (i, j)),
        out_shape=jax.ShapeDtypeStruct((t, n), BF16),
        scratch_shapes=[pltpu.VMEM((2, tm, d), BF16)],
        compiler_params=_cparams(("arbitrary", "arbitrary")),
        name="in_proj",
    )(x2, mod3, g, w)


RET_CPS = 4


def _ret_kernel(q_ref, k_ref, v0_ref, v1_ref, rg0_ref, rg1_ref, cos_ref, sin_ref,
                idec_ref, qdec_ref, kdec_ref, o_ref, state_ref, *, chunk_decay):
    @pl.when(pl.program_id(1) == 0)
    def _():
        state_ref[...] = jnp.zeros_like(state_ref)

    def rot(t, cos, sin):
        return t * cos + pltpu.roll(t, RET_DK // 2, axis=1) * sin

    for ci in range(RET_CPS):
        rows = slice(ci * CHUNK, (ci + 1) * CHUNK)
        cos = cos_ref[rows, :]
        sin = sin_ref[rows, :]
        for h in range(RET_HEADS):
            v_ref, rg_ref = (v0_ref, rg0_ref) if h < RET_HEADS // 2 else (v1_ref, rg1_ref)
            vc = (h % (RET_HEADS // 2)) * RET_DV
            qh = rot(q_ref[rows, h * RET_DK:(h + 1) * RET_DK].astype(F32), cos, sin)
            kh = rot(k_ref[rows, h * RET_DK:(h + 1) * RET_DK].astype(F32), cos, sin) * (RET_DK ** -0.5)
            vh = v_ref[rows, vc:vc + RET_DV]
            qb = qh.astype(BF16)
            kb = kh.astype(BF16)
            scores = lax.dot_general(qb, kb, (((1,), (1,)), ((), ())),
                                     preferred_element_type=F32) * idec_ref[h]
            inner = jnp.dot(scores.astype(BF16), vh, preferred_element_type=F32)
            state = state_ref[h]
            cross = jnp.dot((qh * qdec_ref[h]).astype(BF16), state.astype(BF16),
                            preferred_element_type=F32)
            kv = lax.dot_general((kh * kdec_ref[h]).astype(BF16), vh, (((0,), (0,)), ((), ())),
                                 preferred_element_type=F32)
            state_ref[h] = state * chunk_decay[h] + kv
            r = inner + cross
            r = r * lax.rsqrt(jnp.mean(r * r, axis=-1, keepdims=True) + EPS)
            gate = _silu(rg_ref[rows, vc:vc + RET_DV].astype(F32))
            o_ref[rows, h * RET_DV:(h + 1) * RET_DV] = (gate * r).astype(o_ref.dtype)


def _retention(u, batch, seq):
    t = u.shape[0]
    nch = seq // CHUNK
    inv = ROPE_BASE ** (-jnp.arange(0, RET_DK, 2, dtype=F32) / RET_DK)
    ang = jnp.arange(seq, dtype=F32)[:, None] * inv[None, :]
    cos, sin = jnp.cos(ang), jnp.sin(ang)
    cos_t = jnp.concatenate([cos, cos], axis=-1)
    sin_t = jnp.concatenate([-sin, sin], axis=-1)
    log_g = jnp.log1p(-(2.0 ** (-5.0 - jnp.arange(RET_HEADS, dtype=F32))))
    i = jnp.arange(CHUNK, dtype=F32)
    diff = i[:, None] - i[None, :]
    idec = jnp.where(diff >= 0, jnp.exp(log_g[:, None, None] * jnp.maximum(diff, 0.0)), 0.0)
    qdec = jnp.broadcast_to(jnp.exp(log_g[:, None] * (i + 1.0))[:, :, None],
                            (RET_HEADS, CHUNK, RET_DK))
    kdec = jnp.broadcast_to(jnp.exp(log_g[:, None] * (CHUNK - 1.0 - i))[:, :, None],
                            (RET_HEADS, CHUNK, RET_DK))
    log_g_np = np.log1p(-(2.0 ** (-5.0 - np.arange(RET_HEADS, dtype=np.float32)))).astype(np.float32)
    chunk_decay = tuple(float(np.exp(np.float32(lg * np.float32(CHUNK)))) for lg in log_g_np)

    rows = RET_CPS * CHUNK
    nst = seq // rows

    def ublk(c):
        return pl.BlockSpec((rows, COL_BLK), lambda b, n, c=c: (b * nst + n, c))

    def table():
        return pl.BlockSpec((RET_HEADS, CHUNK, RET_DK), lambda b, n: (0, 0, 0))

    return pl.pallas_call(
        functools.partial(_ret_kernel, chunk_decay=chunk_decay),
        grid=(batch, nst),
        in_specs=[ublk(Q_BLK), ublk(K_BLK), ublk(V_BLK), ublk(V_BLK + 1),
                  ublk(RG_BLK), ublk(RG_BLK + 1),
                  pl.BlockSpec((rows, RET_DK), lambda b, n: (n, 0)),
                  pl.BlockSpec((rows, RET_DK), lambda b, n: (n, 0)),
                  table(), table(), table()],
        out_specs=pl.BlockSpec((rows, RET_HEADS * RET_DV), lambda b, n: (b * nst + n, 0)),
        out_shape=jax.ShapeDtypeStruct((t, RET_HEADS * RET_DV), BF16),
        scratch_shapes=[pltpu.VMEM((RET_HEADS, RET_DK, RET_DV), F32)],
        compiler_params=_cparams(("arbitrary", "arbitrary")),
        name="retention",
    )(u, u, u, u, u, u, cos_t, sin_t, idec, qdec, kdec)


HALO = 8


MIX_TN = 512


def _mix_kernel(cb_ref, cc_ref, cx_ref, hcc_ref, hcx_ref, z_ref, ga0_ref, ga1_ref, gb0_ref, gb1_ref,
                cw_ref, wc_ref, wr_ref, wo_ref, x_ref, mod_ref, modp_ref, g2_ref, wrt_ref, brt_ref,
                o_ref, hp_ref, ef_ref, rw_ref, m_ref, mg_ref, x1p_ref, *, tiles_per_seq, n_tiles):
    tm = cb_ref.shape[0]
    d = o_ref.shape[1]
    step = pl.program_id(0)

    slot = step % 2

    @pl.when(step == 0)
    def _():
        x1p_ref[1] = jnp.zeros(x1p_ref.shape[1:], F32)

    m = cc_ref[...].astype(F32) * cx_ref[...].astype(F32)
    halo = hcc_ref[...].astype(F32) * hcx_ref[...].astype(F32)
    first = (jnp.minimum(step, n_tiles - 1) % tiles_per_seq) == 0
    m_ref[0:HALO, :] = jnp.where(first, 0.0, halo)
    m_ref[HALO:HALO + tm, :] = m
    conv = (m_ref[HALO - 2:HALO - 2 + tm, :] * cw_ref[0:1, :]
            + m_ref[HALO - 1:HALO - 1 + tm, :] * cw_ref[1:2, :]
            + m * cw_ref[2:3, :])
    p = (cb_ref[...].astype(F32) * conv).astype(BF16)
    z = z_ref[...]
    nchunk = d // MIX_TN
    prev = 1 - slot
    ssq = jnp.zeros((tm, 1), F32)
    for n in range(nchunk):
        cols = slice(n * MIX_TN, (n + 1) * MIX_TN)
        ga_ref, gb_ref = (ga0_ref, gb0_ref) if n < COL_BLK // MIX_TN else (ga1_ref, gb1_ref)
        gcols = slice((n * MIX_TN) % COL_BLK, (n * MIX_TN) % COL_BLK + MIX_TN)
        ya = jnp.dot(p, wc_ref[:, cols], preferred_element_type=F32)
        yb = jnp.dot(z, wr_ref[:, cols], preferred_element_type=F32)
        merged = (_sigmoid(ga_ref[:, gcols].astype(F32)) * ya
                  + _sigmoid(gb_ref[:, gcols].astype(F32)) * yb)
        mg_ref[:, cols] = merged.astype(BF16)
        xp = x1p_ref[prev, :, cols]
        ssq = ssq + jnp.sum(xp * xp, axis=-1, keepdims=True)
    rnorm = lax.rsqrt(ssq * (1.0 / d) + EPS)
    gain = g2_ref[...] * (1.0 + modp_ref[0, 4:5, :])
    shift = modp_ref[0, 3:4, :]
    wh, wl = _split_bf16(wrt_ref[...])
    whl = jnp.concatenate([wh, wl], axis=1)
    acc_hw = jnp.zeros((tm, 2 * LANES), F32)
    acc_lw = jnp.zeros((tm, LANES), F32)
    mg = mg_ref[...]
    for n in range(nchunk):
        cols = slice(n * MIX_TN, (n + 1) * MIX_TN)
        acc = jnp.dot(mg, wo_ref[:, cols], preferred_element_type=F32)
        x1 = x_ref[:, cols] + mod_ref[0, 2:3, cols] * acc
        o_ref[:, cols] = x1
        x1p_ref[slot, :, cols] = x1
        if n % 2 == 0:
            c = (n // 2) * MIX_TN
            halves = []
            for c0 in (c, c + HALF):
                hcols = slice(c0, c0 + MIX_TN)
                h2 = x1p_ref[prev, :, hcols] * rnorm * gain[:, hcols] + shift[:, hcols]
                hh, hl = _split_bf16(h2)
                acc_hw = acc_hw + jnp.dot(hh, whl[hcols, :], preferred_element_type=F32)
                acc_lw = acc_lw + jnp.dot(hl, wh[hcols, :], preferred_element_type=F32)
                halves.append(h2)
            hp_ref[:, c:c + MIX_TN] = _pack_pair(halves[0], halves[1])
    logits = (acc_hw[:, :LANES] + (acc_hw[:, LANES:] + acc_lw)) + brt_ref[...]
    ef_ref[...], rw_ref[...] = _route_from_logits(logits)


def _mix(u, z, conv_w, wc_bf, wr_bf, wo_bf, x2, mod3, g2, w_rt, b_rt, seq):
    t, d = x2.shape
    tm = 256
    nt = t // tm
    hb = tm // HALO
    per_b = seq // tm

    def cur(i):
        return jnp.minimum(i, nt - 1)

    def prev(i):
        return jnp.maximum(i - 1, 0)

    def ublk(c):
        return pl.BlockSpec((tm, COL_BLK), lambda i, c=c: (cur(i), c))

    def halo(c):
        return pl.BlockSpec((HALO, COL_BLK), lambda i, c=c: (jnp.maximum(cur(i) * hb - 1, 0), c))

    def resident(shape):
        return pl.BlockSpec(shape, lambda i: (0, 0), pipeline_mode=pl.Buffered(1))

    return pl.pallas_call(
        functools.partial(_mix_kernel, tiles_per_seq=per_b, n_tiles=nt),
        grid=(nt + 1,),
        in_specs=[ublk(CB_BLK), ublk(CC_BLK), ublk(CX_BLK), halo(CC_BLK), halo(CX_BLK),
                  pl.BlockSpec((tm, z.shape[1]), lambda i: (cur(i), 0)),
                  ublk(GA_BLK), ublk(GA_BLK + 1), ublk(GB_BLK), ublk(GB_BLK + 1),
                  pl.BlockSpec((CONV_K, CONV_CH), lambda i: (0, 0)),
                  resident(wc_bf.shape), resident(wr_bf.shape), resident(wo_bf.shape),
                  pl.BlockSpec((tm, d), lambda i: (cur(i), 0)),
                  pl.BlockSpec((1, 6, d), lambda i: (cur(i) // per_b, 0, 0)),
                  pl.BlockSpec((1, 6, d), lambda i: (prev(i) // per_b, 0, 0)),
                  pl.BlockSpec((1, d), lambda i: (0, 0)),
                  pl.BlockSpec((d, LANES), lambda i: (0, 0)),
                  pl.BlockSpec((1, LANES), lambda i: (0, 0))],
        out_specs=[pl.BlockSpec((tm, d), lambda i: (cur(i), 0)),
                   pl.BlockSpec((tm, HALF), lambda i: (prev(i), 0)),
                   pl.BlockSpec((tm, LANES), lambda i: (prev(i), 0)),
                   pl.BlockSpec((tm, LANES), lambda i: (prev(i), 0))],
        out_shape=[jax.ShapeDtypeStruct((t, d), F32),
                   jax.ShapeDtypeStruct((t, HALF), U32),
                   jax.ShapeDtypeStruct((t, LANES), F32),
                   jax.ShapeDtypeStruct((t, LANES), F32)],
        scratch_shapes=[pltpu.VMEM((tm + HALO, CONV_CH), F32),
                        pltpu.VMEM((tm, d), BF16),
                        pltpu.VMEM((2, tm, d), F32)],
        compiler_params=_cparams(("arbitrary",)),
        name="mix",
    )(u, u, u, u, u, z, u, u, u, u, conv_w, wc_bf, wr_bf, wo_bf, x2, mod3, mod3, g2, w_rt, b_rt)


ROUTE_E0 = N_GROUPS


U32 = jnp.uint32
HALF = D_MODEL // 2


def _pack_pair(lo, hi):
    ulo = lax.bitcast_convert_type(lo.astype(BF16).astype(F32), U32)
    uhi = lax.bitcast_convert_type(hi.astype(BF16).astype(F32), U32)
    return (ulo >> 16) | uhi


def _unpack_pair(w):
    lo = lax.bitcast_convert_type(w << 16, F32)
    hi = lax.bitcast_convert_type(w & jnp.uint32(0xFFFF0000), F32)
    return lo, hi


ROW_SUB = HALF // LANES


def _row_slab(r):
    if isinstance(r, int):
        return pl.ds(r * ROW_SUB, ROW_SUB)
    return pl.ds(pl.multiple_of(r * ROW_SUB, ROW_SUB), ROW_SUB)


def _store_rows(ref, row0, v):
    for c in range(ROW_SUB):
        ref[pl.ds(row0 * ROW_SUB + c, v.shape[0], stride=ROW_SUB), :] = v[:, c * LANES:(c + 1) * LANES]


def _load_rows(ref, row0, n):
    return jnp.concatenate(
        [ref[pl.ds(row0 * ROW_SUB + c, n, stride=ROW_SUB), :] for c in range(ROW_SUB)], axis=1)


def _route_from_logits(logits):
    lane = lax.broadcasted_iota(I32, logits.shape, 1).astype(F32)
    neg = -jnp.inf
    big = float(LANES)
    gl = jnp.where(lane < N_GROUPS, logits, neg)
    gmax = jnp.max(gl, axis=-1, keepdims=True)
    gsel = jnp.min(jnp.where(gl == gmax, lane, big), axis=-1, keepdims=True)
    gsum = jnp.sum(jnp.exp(gl - gmax), axis=-1, keepdims=True)
    g_w = 1.0 / gsum
    lo_lane = ROUTE_E0 + EXPERTS_PER_GROUP * gsel
    emask = (lane >= lo_lane) & (lane < lo_lane + EXPERTS_PER_GROUP)
    el = jnp.where(emask, logits, neg)
    m1 = jnp.max(el, axis=-1, keepdims=True)
    i1 = jnp.min(jnp.where(el == m1, lane, big), axis=-1, keepdims=True)
    el2 = jnp.where(lane == i1, neg, el)
    m2 = jnp.max(el2, axis=-1, keepdims=True)
    i2 = jnp.min(jnp.where(el2 == m2, lane, big), axis=-1, keepdims=True)
    esum = jnp.sum(jnp.exp(el - m1), axis=-1, keepdims=True)
    p1 = 1.0 / esum
    p2 = jnp.exp(m2 - m1) / esum
    tot = p1 + p2
    c1 = g_w * (p1 / tot)
    c2 = g_w * (p2 / tot)
    ef = jnp.where(lane == 0.0, i1 - ROUTE_E0, jnp.where(lane == 1.0, i2 - ROUTE_E0, 0.0))
    cw = jnp.where(lane == 0.0, c1, jnp.where(lane == 1.0, c2, 0.0))
    return ef, cw


META_TB = 512
BLK_ROWS = 256


def _meta_kernel(ef_ref, dest_ref, blk_ref, pref_ref):
    t = ef_ref.shape[0]
    nb = t // META_TB
    lane = lax.broadcasted_iota(I32, (META_TB, LANES), 1).astype(F32)
    r_i = lax.broadcasted_iota(I32, (META_TB, META_TB), 0)
    c_i = lax.broadcasted_iota(I32, (META_TB, META_TB), 1)
    tril = jnp.where(r_i > c_i, 1.0, 0.0).astype(BF16)

    def onehots(i):
        ef = ef_ref[pl.ds(i * META_TB, META_TB), :]
        oh1 = jnp.where(lane == ef[:, 0:1], 1.0, 0.0)
        oh2 = jnp.where(lane == ef[:, 1:2], 1.0, 0.0)
        return oh1, oh2

    def pass1(i, carry):
        oh1, oh2 = onehots(i)
        cnt = oh1 + oh2
        pref = jnp.dot(tril, cnt.astype(BF16), preferred_element_type=F32) + carry
        pref_ref[pl.ds(i * META_TB, META_TB), :] = pref
        return carry + jnp.sum(cnt, axis=0, keepdims=True)

    counts = lax.fori_loop(0, nb, pass1, jnp.zeros((1, LANES), F32))
    nblk = jnp.floor((counts + (EXPERT_BLOCK - 1)) * (1.0 / EXPERT_BLOCK))
    u_r = lax.broadcasted_iota(I32, (LANES, LANES), 0)
    u_c = lax.broadcasted_iota(I32, (LANES, LANES), 1)
    upper = jnp.where(u_r <= u_c, 1.0, 0.0).astype(BF16)
    nblk8 = jnp.broadcast_to(nblk, (8, LANES))
    pend_blk = jnp.dot(nblk8.astype(BF16), upper, preferred_element_type=F32)[0:1, :]
    pstart_blk = pend_blk - nblk
    pstart = pstart_blk * float(EXPERT_BLOCK)

    def pass2(i, _):
        oh1, oh2 = onehots(i)
        base = pref_ref[pl.ds(i * META_TB, META_TB), :] + pstart
        d1 = jnp.sum(oh1 * base, axis=-1, keepdims=True)
        d2 = jnp.sum(oh2 * base, axis=-1, keepdims=True)
        dest = jnp.where(lane == 0.0, d1, jnp.where(lane == 1.0, d2, 0.0))
        dest_ref[pl.ds(i * META_TB, META_TB), :] = dest.astype(I32)
        return 0

    lax.fori_loop(0, nb, pass2, 0)

    blane = lax.broadcasted_iota(I32, (BLK_ROWS, LANES), 1).astype(F32)
    brow = lax.broadcasted_iota(I32, (BLK_ROWS, LANES), 0).astype(F32)
    emask = blane < N_EXPERTS
    owner = jnp.sum(jnp.where(emask & (pend_blk <= brow), 1.0, 0.0), axis=-1, keepdims=True)
    owner = jnp.minimum(owner, float(N_EXPERTS - 1))
    total = jnp.sum(jnp.where(emask, nblk, 0.0), axis=-1, keepdims=True)
    used = jnp.where(brow < total, 1.0, 0.0)
    mine = emask & (blane == owner)
    cnt_b = jnp.sum(jnp.where(mine, counts, 0.0), axis=-1, keepdims=True)
    ps_b = jnp.sum(jnp.where(mine, pstart_blk, 0.0), axis=-1, keepdims=True)
    nvalid = jnp.clip(cnt_b - (brow - ps_b) * float(EXPERT_BLOCK), 0.0, float(EXPERT_BLOCK))
    pad_lo = nvalid * used
    blk_ref[...] = jnp.where(blane == 0.0, owner,
                             jnp.where(blane == 1.0, used,
                                       jnp.where(blane == 2.0, pad_lo, 0.0))).astype(I32)


def _meta(ef):
    t = ef.shape[0]
    return pl.pallas_call(
        _meta_kernel,
        out_shape=[jax.ShapeDtypeStruct((t, LANES), I32),
                   jax.ShapeDtypeStruct((BLK_ROWS, LANES), I32)],
        scratch_shapes=[pltpu.VMEM((t, LANES), F32)],
        compiler_params=pltpu.CompilerParams(vmem_limit_bytes=VMEM_LIMIT),
        name="meta",
    )(ef)


DISP_TM = 256
TOP_K = 2


def _dispatch_kernel(pad_lo_ref, dest_ref, h_ref, xbuf_ref, stage, zrow, sems, zsem):
    i = pl.program_id(0)
    n = pl.num_programs(0)
    tm = h_ref.shape[0]
    n_blk = pad_lo_ref.shape[0]
    slot = i % 2

    def pad_rows(action):
        def blk(b, _):
            lo = pad_lo_ref[b]
            base = pl.multiple_of(b * EXPERT_BLOCK, EXPERT_BLOCK)

            @pl.when(lo == 0)
            def _():
                whole = pl.ds(pl.multiple_of(base * ROW_SUB, EXPERT_BLOCK * ROW_SUB),
                              EXPERT_BLOCK * ROW_SUB)
                action(pltpu.make_async_copy(zrow, xbuf_ref.at[whole], zsem))

            @pl.when(lo > 0)
            def _():
                def row(r, _):
                    action(pltpu.make_async_copy(zrow.at[_row_slab(0)],
                                                 xbuf_ref.at[_row_slab(base + r)], zsem))
                    return 0
                lax.fori_loop(lo, EXPERT_BLOCK, row, 0)
            return 0
        lax.fori_loop(0, n_blk, blk, 0)

    @pl.when(i == 0)
    def _():
        zrow[...] = jnp.zeros_like(zrow)
        pad_rows(lambda cp: cp.start())

    _store_rows(stage.at[slot], 0, h_ref[...])
    for j in range(tm):
        for k in range(TOP_K):
            d = dest_ref[0, 0, TOP_K * j + k]
            pltpu.make_async_copy(stage.at[slot, _row_slab(j)], xbuf_ref.at[_row_slab(d)],
                                  sems.at[slot]).start(priority=k)

    def wait_tile(s):
        for _ in range(TOP_K):
            pltpu.make_async_copy(stage.at[s], xbuf_ref.at[pl.ds(0, tm * ROW_SUB)], sems.at[s]).wait()

    @pl.when(i > 0)
    def _():
        wait_tile(1 - slot)

    @pl.when(i == n - 1)
    def _():
        wait_tile(slot)
        pad_rows(lambda cp: cp.wait())


def _dispatch(pad_lo, dest3, hp, n_rows):
    t, w = hp.shape
    tm = DISP_TM
    grid_spec = pltpu.PrefetchScalarGridSpec(
        num_scalar_prefetch=1,
        grid=(t // tm,),
        in_specs=[pl.BlockSpec((1, 1, TOP_K * tm), lambda i, *_: (i, 0, 0), memory_space=pltpu.SMEM),
                  pl.BlockSpec((tm, w), lambda i, *_: (i, 0))],
        out_specs=pl.BlockSpec(memory_space=pl.ANY),
        scratch_shapes=[pltpu.VMEM((2, tm * ROW_SUB, LANES), U32),
                        pltpu.VMEM((EXPERT_BLOCK * ROW_SUB, LANES), U32),
                        pltpu.SemaphoreType.DMA((2,)), pltpu.SemaphoreType.DMA(())],
    )
    return pl.pallas_call(
        _dispatch_kernel,
        grid_spec=grid_spec,
        out_shape=jax.ShapeDtypeStruct((n_rows * ROW_SUB, LANES), U32),
        compiler_params=_cparams(("arbitrary",)),
        name="dispatch",
    )(pad_lo, dest3, hp)


MOE_SUB = 4
W_RING = 2
CAST_ROWS = 256
SWITCH_AFTER = 4


def _moe_kernel(run_e_ref, run_id_ref, used_ref, nruns_ref, x_ref, wg_hbm, wu_hbm, wd_hbm, o_ref,
                wg_st, wu_st, wd_st, wg_bf, wu_bf, wd_bf, sems, state):
    step = pl.program_id(0)
    n_runs = nruns_ref[0]
    streams = ((wg_hbm, wg_st, wg_bf), (wu_hbm, wu_st, wu_bf), (wd_hbm, wd_st, wd_bf))

    def weight_copy(m, run):
        hbm, stage, _ = streams[m]
        return pltpu.make_async_copy(hbm.at[run_e_ref[run]], stage, sems.at[m])

    @pl.when(step == 0)
    def _():
        state[0] = 0
        state[1] = 0
        for m in range(len(streams)):
            weight_copy(m, 0).start(priority=1)

    def switch_to(run, b):
        ring = lax.rem(run, W_RING)
        for m, (_, stage, bf) in enumerate(streams):
            weight_copy(m, run).wait()

            def body(i, _, stage=stage, bf=bf):
                rows = pl.ds(pl.multiple_of(i * CAST_ROWS, CAST_ROWS), CAST_ROWS)
                bf[ring, rows, :] = stage[rows, :].astype(BF16)
                return 0
            lax.fori_loop(0, stage.shape[0] // CAST_ROWS, body, 0)

            @pl.when(run + 1 < n_runs)
            def _(m=m):
                weight_copy(m, run + 1).start(priority=1)
        state[0] = run + 1
        state[1] = b + SWITCH_AFTER

    def run_block(b, row0):
        run = run_id_ref[b]
        pl.when(state[0] == run)(functools.partial(switch_to, run, b))
        pl.when((state[0] == run + 1) & (run + 1 < n_runs) & (b >= state[1]))(
            functools.partial(switch_to, run + 1, b))

        ring = lax.rem(run, W_RING)
        xlo, xhi = _unpack_pair(_load_rows(x_ref, row0, EXPERT_BLOCK))
        xb = jnp.concatenate([xlo.astype(BF16), xhi.astype(BF16)], axis=1)
        g = jnp.dot(xb, wg_bf[ring], preferred_element_type=F32)
        u = jnp.dot(xb, wu_bf[ring], preferred_element_type=F32)
        hmid = (_silu(g) * u).astype(BF16)
        y = jnp.dot(hmid, wd_bf[ring], preferred_element_type=F32)
        _store_rows(o_ref, row0, _pack_pair(y[:, :HALF], y[:, HALF:]))

    for j in range(MOE_SUB):
        b = step * MOE_SUB + j
        row0 = j * EXPERT_BLOCK
        pl.when(used_ref[b] == 1)(functools.partial(run_block, b, row0))

        @pl.when(used_ref[b] == 0)
        def _():
            o_ref[pl.ds(row0 * ROW_SUB, EXPERT_BLOCK * ROW_SUB), :] = jnp.zeros(
                (EXPERT_BLOCK * ROW_SUB, LANES), U32)


def _moe(run_e, run_id, used, n_runs, xbuf, w_gate, w_up, w_down):
    r = xbuf.shape[0]
    rows = MOE_SUB * EXPERT_BLOCK * ROW_SUB
    d, de = w_gate.shape[1:]

    def xmap(s, run_e_ref, run_id_ref, used_ref, nruns_ref):
        del run_e_ref, run_id_ref, nruns_ref
        return (jnp.where(used_ref[s * MOE_SUB] == 1, s, 0), 0)

    grid_spec = pltpu.PrefetchScalarGridSpec(
        num_scalar_prefetch=4,
        grid=(r // rows,),
        in_specs=[pl.BlockSpec((rows, LANES), xmap),
                  pl.BlockSpec(memory_space=pl.ANY),
                  pl.BlockSpec(memory_space=pl.ANY),
                  pl.BlockSpec(memory_space=pl.ANY)],
        out_specs=pl.BlockSpec((rows, LANES), lambda s, *_: (s, 0)),
        scratch_shapes=[pltpu.VMEM((d, de), F32), pltpu.VMEM((d, de), F32), pltpu.VMEM((de, d), F32),
                        pltpu.VMEM((W_RING, d, de), BF16), pltpu.VMEM((W_RING, d, de), BF16),
                        pltpu.VMEM((W_RING, de, d), BF16),
                        pltpu.SemaphoreType.DMA((3,)), pltpu.SMEM((2,), I32)],
    )
    return pl.pallas_call(
        _moe_kernel,
        grid_spec=grid_spec,
        out_shape=jax.ShapeDtypeStruct((r, LANES), U32),
        compiler_params=_cparams(("arbitrary",), vmem=60 * 1024 * 1024),
        name="moe",
    )(run_e, run_id, used, n_runs, xbuf, w_gate, w_up, w_down)


FIN_TM = 256


def _final_kernel(dest_ref, dnext_ref, x_ref, cw_ref, mod_ref, g_ref, y_hbm, o_ref, ybuf, sems):
    i = pl.program_id(0)
    n = pl.num_programs(0)
    tm = x_ref.shape[0]
    slot = i % 2

    def gather(d_ref, s):
        for j in range(tm):
            for k in range(TOP_K):
                d = d_ref[0, 0, TOP_K * j + k]
                pltpu.make_async_copy(y_hbm.at[_row_slab(d)], ybuf.at[s, k, _row_slab(j)],
                                      sems.at[s]).start(priority=k)

    @pl.when(i == 0)
    def _():
        gather(dest_ref, 0)

    @pl.when(i + 1 < n)
    def _():
        gather(dnext_ref, 1 - slot)

    for k in range(TOP_K):
        pltpu.make_async_copy(y_hbm.at[pl.ds(0, tm * ROW_SUB)], ybuf.at[slot, k], sems.at[slot]).wait()

    cw = cw_ref[...]
    c0, c1 = cw[:, 0:1], cw[:, 1:2]
    lo0, hi0 = _unpack_pair(_load_rows(ybuf.at[slot, 0], 0, tm))
    lo1, hi1 = _unpack_pair(_load_rows(ybuf.at[slot, 1], 0, tm))
    g2 = mod_ref[0, 5:6, :]
    xlo = x_ref[:, :HALF] + g2[:, :HALF] * (lo0 * c0 + lo1 * c1)
    xhi = x_ref[:, HALF:] + g2[:, HALF:] * (hi0 * c0 + hi1 * c1)
    ms = (jnp.sum(xlo * xlo, axis=-1, keepdims=True)
          + jnp.sum(xhi * xhi, axis=-1, keepdims=True)) * (1.0 / D_MODEL)
    r = lax.rsqrt(ms + EPS)
    o_ref[:, :HALF] = xlo * r * g_ref[:, :HALF]
    o_ref[:, HALF:] = xhi * r * g_ref[:, HALF:]


def _final(dest3, x1, cw, mod3, g, ybuf, seq):
    t, d = x1.shape
    tm = FIN_TM
    nt = t // tm
    per_b = seq // tm

    def dspec(shift):
        return pl.BlockSpec((1, 1, TOP_K * tm), lambda i: (jnp.minimum(i + shift, nt - 1), 0, 0),
                            memory_space=pltpu.SMEM)

    return pl.pallas_call(
        _final_kernel,
        grid=(nt,),
        in_specs=[dspec(0), dspec(1),
                  pl.BlockSpec((tm, d), lambda i: (i, 0)),
                  pl.BlockSpec((tm, LANES), lambda i: (i, 0)),
                  pl.BlockSpec((1, 6, d), lambda i: (i // per_b, 0, 0)),
                  pl.BlockSpec((1, d), lambda i: (0, 0)),
                  pl.BlockSpec(memory_space=pl.ANY)],
        out_specs=pl.BlockSpec((tm, d), lambda i: (i, 0)),
        out_shape=jax.ShapeDtypeStruct((t, d), F32),
        scratch_shapes=[pltpu.VMEM((2, TOP_K, tm * ROW_SUB, LANES), U32),
                        pltpu.SemaphoreType.DMA((2,))],
        compiler_params=_cparams(("arbitrary",)),
        name="final",
    )(dest3, dest3, x1, cw, mod3, g, ybuf)


def kernel(x, c, w_ada, b_ada, norm1_g, w_in, conv_w, w_conv_out, w_ret_out, w_o, norm2_g,
           w_router_group, b_router_group, w_router_expert, b_router_expert,
           w_gate, w_up, w_down, norm_f_g):
    batch, seq, d = x.shape
    t = batch * seq
    depth = w_ada.shape[0]
    assert d == D_MODEL and w_in.shape[2] == IN_COLS and w_gate.shape[1] == N_EXPERTS
    assert depth == 1, "the final kernel fuses the last rmsnorm into the single layer"
    n_rows = t * 2 + N_EXPERTS * EXPERT_BLOCK
    n_blk = n_rows // EXPERT_BLOCK
    assert n_blk <= BLK_ROWS

    c_pad = jnp.pad(c, ((0, 8 - batch), (0, 0)))
    xs = x.reshape(t, d)
    for l in range(depth):
        mod = _ada(c_pad, w_ada[l], b_ada[l].reshape(1, -1))[:batch]
        mod3 = mod.reshape(batch, 6, d)

        u = _in_proj(xs, mod3, norm1_g[l].reshape(1, d), w_in[l], seq)
        z = _retention(u, batch, seq)
        w_r = jnp.pad(jnp.concatenate([w_router_group[l], w_router_expert[l]], axis=1),
                      ((0, 0), (0, LANES - N_GROUPS - N_EXPERTS)))
        b_r = jnp.pad(jnp.concatenate([b_router_group[l], b_router_expert[l]]),
                      (0, LANES - N_GROUPS - N_EXPERTS)).reshape(1, LANES)
        x1, h2, ef, cw = _mix(u, z, conv_w[l], w_conv_out[l].astype(BF16), w_ret_out[l].astype(BF16),
                              w_o[l].astype(BF16), xs, mod3, norm2_g[l].reshape(1, d), w_r, b_r, seq)

        dest, blk = _meta(ef)
        blk_e = blk[:n_blk, 0]
        used = blk[:n_blk, 1]
        prev_e = jnp.concatenate([jnp.full((1,), -1, I32), blk_e[:-1]])
        first = ((blk_e != prev_e) & (used == 1)).astype(I32)
        run_id = jnp.maximum(jnp.cumsum(first) - 1, 0).astype(I32)
        n_runs = jnp.sum(first).astype(I32).reshape(1)
        present = jnp.any((blk_e[None, :] == jnp.arange(N_EXPERTS, dtype=I32)[:, None])
                          & (used[None, :] == 1), axis=1)
        eids = jnp.arange(N_EXPERTS, dtype=I32)
        rank = jnp.cumsum(present.astype(I32)) - 1
        run_e = jnp.sum(jnp.where(present[None, :] & (rank[None, :] == eids[:, None]),
                                  eids[None, :], 0), axis=1).astype(I32)

        dest2 = dest[:, :TOP_K]
        xbuf = _dispatch(blk[:n_blk, 2], dest2.reshape(t // DISP_TM, 1, TOP_K * DISP_TM), h2, n_rows)
        ybuf = _moe(run_e, run_id, used, n_runs, xbuf, w_gate[l], w_up[l], w_down[l])
        xs = _final(dest2.reshape(t // FIN_TM, 1, TOP_K * FIN_TM), x1, cw, mod3,
                    norm_f_g.reshape(1, d), ybuf, seq)
    return xs.reshape(batch, seq, d)
```

```python
import functools

import numpy as np
import jax
import jax.numpy as jnp
from jax import lax
from jax.experimental import pallas as pl
from jax.experimental.pallas import tpu as pltpu

F32 = jnp.float32
BF16 = jnp.bfloat16
I32 = jnp.int32

D_MODEL = 2048
CONV_CH = 1024
CONV_K = 3
RET_HEADS = 8
RET_DK = 128
RET_DV = 256
CHUNK = 128
ROPE_BASE = 10000.0
N_GROUPS = 4
EXPERTS_PER_GROUP = 8
N_EXPERTS = 32
EXPERT_BLOCK = 128
EPS = 1e-6
IN_COLS = 13312
LANES = 128
COL_BLK = 1024
CB_BLK, CC_BLK, CX_BLK, Q_BLK, K_BLK, V_BLK, RG_BLK, GA_BLK, GB_BLK = 0, 1, 2, 3, 4, 5, 7, 9, 11

V7X_VMEM_BYTES = 64 * 1024 * 1024
VMEM_LIMIT = V7X_VMEM_BYTES - 8 * 1024 * 1024
VMEM_LIMIT_MOE = V7X_VMEM_BYTES - 4 * 1024 * 1024


def _cparams(sem, vmem=VMEM_LIMIT):
    return pltpu.CompilerParams(dimension_semantics=sem, vmem_limit_bytes=vmem)


def _sigmoid(v):
    return 1.0 / (1.0 + jnp.exp(-v))


def _silu(v):
    return v * _sigmoid(v)


def _split_bf16(v):
    hi = v.astype(BF16)
    lo = (v - hi.astype(F32)).astype(BF16)
    return hi, lo


def _ada_kernel(c_ref, w_ref, b_ref, o_ref):
    rows = c_ref.shape[0]
    ch, cl = _split_bf16(_silu(c_ref[...]))
    wh, wl = _split_bf16(w_ref[...])
    p = jnp.dot(jnp.concatenate([ch, cl], axis=0), wh, preferred_element_type=F32)
    o_ref[...] = (p[:rows] + (p[rows:] + jnp.dot(ch, wl, preferred_element_type=F32))) + b_ref[...]


def _ada(c_pad, w, b):
    rows, d = c_pad.shape
    n = w.shape[1]
    tn = 1024
    return pl.pallas_call(
        _ada_kernel,
        grid=(n // tn,),
        in_specs=[pl.BlockSpec((rows, d), lambda j: (0, 0)),
                  pl.BlockSpec((d, tn), lambda j: (0, j)),
                  pl.BlockSpec((1, tn), lambda j: (0, j))],
        out_specs=pl.BlockSpec((rows, tn), lambda j: (0, j)),
        out_shape=jax.ShapeDtypeStruct((rows, n), F32),
        compiler_params=_cparams(("arbitrary",)),
        name="ada",
    )(c_pad, w, b)


NORM_ROWS = 64


IN_SPLIT = 2


def _in_kernel(x_ref, mod_ref, g_ref, w_ref, o_ref, h_ref):
    i, j = pl.program_id(0), pl.program_id(1)
    nj = pl.num_programs(1)
    tm, tn = o_ref.shape
    slot = i % 2

    def norm_rows(dst, rows):
        gain = g_ref[...] * (1.0 + mod_ref[0, 1:2, :])
        x = x_ref[rows, :]
        r = lax.rsqrt(jnp.mean(x * x, axis=-1, keepdims=True) + EPS)
        h_ref[dst, rows, :] = (x * r * gain + mod_ref[0, 0:1, :]).astype(BF16)

    @pl.when((i == 0) & (j == 0))
    def _():
        def body(k, _):
            norm_rows(0, pl.ds(pl.multiple_of(k * NORM_ROWS, NORM_ROWS), NORM_ROWS))
            return 0
        lax.fori_loop(0, tm // NORM_ROWS, body, 0)

    @pl.when(j < nj - 1)
    def _():
        o_ref[...] = jnp.dot(h_ref[slot], w_ref[...].astype(BF16),
                             preferred_element_type=F32).astype(o_ref.dtype)

    @pl.when(j == nj - 1)
    def _():
        h = h_ref[slot]
        cw, rw = tn // IN_SPLIT, tm // IN_SPLIT
        for q in range(IN_SPLIT):
            cols = slice(q * cw, (q + 1) * cw)
            o_ref[:, cols] = jnp.dot(h, w_ref[:, cols].astype(BF16),
                                     preferred_element_type=F32).astype(o_ref.dtype)
            for k in range(rw // NORM_ROWS):
                norm_rows(1 - slot, slice(q * rw + k * NORM_ROWS, q * rw + (k + 1) * NORM_ROWS))


def _in_proj(x2, mod3, g, w, seq):
    t, d = x2.shape
    n = w.shape[1]
    tm, tn = 1024, 1024
    nt, nj = t // tm, n // tn
    per_b = seq // tm

    def xtile(i, j):
        return jnp.minimum(i + (j == nj - 1).astype(jnp.int32), nt - 1)

    return pl.pallas_call(
        _in_kernel,
        grid=(nt, nj),
        in_specs=[pl.BlockSpec((tm, d), lambda i, j: (xtile(i, j), 0)),
                  pl.BlockSpec((1, 6, d), lambda i, j: (xtile(i, j) // per_b, 0, 0)),
                  pl.BlockSpec((1, d), lambda i, j: (0, 0)),
                  pl.BlockSpec((d, tn), lambda i, j: (0, j))],
        out_specs=pl.BlockSpec((tm, tn), lambda i, j: (i, j)),
        out_shape=jax.ShapeDtypeStruct((t, n), BF16),
        scratch_shapes=[pltpu.VMEM((2, tm, d), BF16)],
        compiler_params=_cparams(("arbitrary", "arbitrary")),
        name="in_proj",
    )(x2, mod3, g, w)


RET_CPS = 4


def _ret_kernel(q_ref, k_ref, v0_ref, v1_ref, rg0_ref, rg1_ref, cos_ref, sin_ref,
                idec_ref, qdec_ref, kdec_ref, o_ref, state_ref, *, chunk_decay):
    @pl.when(pl.program_id(1) == 0)
    def _():
        state_ref[...] = jnp.zeros_like(state_ref)

    def rot(t, cos, sin):
        return t * cos + pltpu.roll(t, RET_DK // 2, axis=1) * sin

    for ci in range(RET_CPS):
        rows = slice(ci * CHUNK, (ci + 1) * CHUNK)
        cos = cos_ref[rows, :]
        sin = sin_ref[rows, :]
        for h in range(RET_HEADS):
            v_ref, rg_ref = (v0_ref, rg0_ref) if h < RET_HEADS // 2 else (v1_ref, rg1_ref)
            vc = (h % (RET_HEADS // 2)) * RET_DV
            qh = rot(q_ref[rows, h * RET_DK:(h + 1) * RET_DK].astype(F32), cos, sin)
            kh = rot(k_ref[rows, h * RET_DK:(h + 1) * RET_DK].astype(F32), cos, sin) * (RET_DK ** -0.5)
            vh = v_ref[rows, vc:vc + RET_DV]
            qb = qh.astype(BF16)
            kb = kh.astype(BF16)
            scores = lax.dot_general(qb, kb, (((1,), (1,)), ((), ())),
                                     preferred_element_type=F32) * idec_ref[h]
            inner = jnp.dot(scores.astype(BF16), vh, preferred_element_type=F32)
            state = state_ref[h]
            cross = jnp.dot((qh * qdec_ref[h]).astype(BF16), state.astype(BF16),
                            preferred_element_type=F32)
            kv = lax.dot_general((kh * kdec_ref[h]).astype(BF16), vh, (((0,), (0,)), ((), ())),
                                 preferred_element_type=F32)
            state_ref[h] = state * chunk_decay[h] + kv
            r = inner + cross
            r = r * lax.rsqrt(jnp.mean(r * r, axis=-1, keepdims=True) + EPS)
            gate = _silu(rg_ref[rows, vc:vc + RET_DV].astype(F32))
            o_ref[rows, h * RET_DV:(h + 1) * RET_DV] = (gate * r).astype(o_ref.dtype)


def _retention(u, batch, seq):
    t = u.shape[0]
    inv = ROPE_BASE ** (-jnp.arange(0, RET_DK, 2, dtype=F32) / RET_DK)
    ang = jnp.arange(seq, dtype=F32)[:, None] * inv[None, :]
    cos, sin = jnp.cos(ang), jnp.sin(ang)
    cos_t = jnp.concatenate([cos, cos], axis=-1)
    sin_t = jnp.concatenate([-sin, sin], axis=-1)
    log_g = jnp.log1p(-(2.0 ** (-5.0 - jnp.arange(RET_HEADS, dtype=F32))))
    i = jnp.arange(CHUNK, dtype=F32)
    diff = i[:, None] - i[None, :]
    idec = jnp.where(diff >= 0, jnp.exp(log_g[:, None, None] * jnp.maximum(diff, 0.0)), 0.0)
    qdec = jnp.broadcast_to(jnp.exp(log_g[:, None] * (i + 1.0))[:, :, None],
                            (RET_HEADS, CHUNK, RET_DK))
    kdec = jnp.broadcast_to(jnp.exp(log_g[:, None] * (CHUNK - 1.0 - i))[:, :, None],
                            (RET_HEADS, CHUNK, RET_DK))
    log_g_np = np.log1p(-(2.0 ** (-5.0 - np.arange(RET_HEADS, dtype=np.float32)))).astype(np.float32)
    chunk_decay = tuple(float(np.exp(np.float32(lg * np.float32(CHUNK)))) for lg in log_g_np)

    rows = RET_CPS * CHUNK
    nst = seq // rows

    def ublk(c):
        return pl.BlockSpec((rows, COL_BLK), lambda b, n, c=c: (b * nst + n, c))

    def table():
        return pl.BlockSpec((RET_HEADS, CHUNK, RET_DK), lambda b, n: (0, 0, 0))

    return pl.pallas_call(
        functools.partial(_ret_kernel, chunk_decay=chunk_decay),
        grid=(batch, nst),
        in_specs=[ublk(Q_BLK), ublk(K_BLK), ublk(V_BLK), ublk(V_BLK + 1),
                  ublk(RG_BLK), ublk(RG_BLK + 1),
                  pl.BlockSpec((rows, RET_DK), lambda b, n: (n, 0)),
                  pl.BlockSpec((rows, RET_DK), lambda b, n: (n, 0)),
                  table(), table(), table()],
        out_specs=pl.BlockSpec((rows, RET_HEADS * RET_DV), lambda b, n: (b * nst + n, 0)),
        out_shape=jax.ShapeDtypeStruct((t, RET_HEADS * RET_DV), BF16),
        scratch_shapes=[pltpu.VMEM((RET_HEADS, RET_DK, RET_DV), F32)],
        compiler_params=_cparams(("arbitrary", "arbitrary")),
        name="retention",
    )(u, u, u, u, u, u, cos_t, sin_t, idec, qdec, kdec)


HALO = 8


MIX_TN = 512


def _mix_kernel(cb_ref, cc_ref, cx_ref, hcc_ref, hcx_ref, z_ref, ga0_ref, ga1_ref, gb0_ref, gb1_ref,
                cw_ref, wc_ref, wr_ref, wo_ref, x_ref, mod_ref, modp_ref, g2_ref, wrt_ref, brt_ref,
                o_ref, hp_ref, ef_ref, rw_ref, m_ref, mg_ref, x1p_ref, *, tiles_per_seq, n_tiles):
    tm = cb_ref.shape[0]
    d = o_ref.shape[1]
    step = pl.program_id(0)

    slot = step % 2

    @pl.when(step == 0)
    def _():
        x1p_ref[1] = jnp.zeros(x1p_ref.shape[1:], F32)

    m = cc_ref[...].astype(F32) * cx_ref[...].astype(F32)
    halo = hcc_ref[...].astype(F32) * hcx_ref[...].astype(F32)
    first = (jnp.minimum(step, n_tiles - 1) % tiles_per_seq) == 0
    m_ref[0:HALO, :] = jnp.where(first, 0.0, halo)
    m_ref[HALO:HALO + tm, :] = m
    conv = (m_ref[HALO - 2:HALO - 2 + tm, :] * cw_ref[0:1, :]
            + m_ref[HALO - 1:HALO - 1 + tm, :] * cw_ref[1:2, :]
            + m * cw_ref[2:3, :])
    p = (cb_ref[...].astype(F32) * conv).astype(BF16)
    z = z_ref[...]
    nchunk = d // MIX_TN
    prev = 1 - slot
    ssq = jnp.zeros((tm, 1), F32)
    for n in range(nchunk):
        cols = slice(n * MIX_TN, (n + 1) * MIX_TN)
        ga_ref, gb_ref = (ga0_ref, gb0_ref) if n < COL_BLK // MIX_TN else (ga1_ref, gb1_ref)
        gcols = slice((n * MIX_TN) % COL_BLK, (n * MIX_TN) % COL_BLK + MIX_TN)
        ya = jnp.dot(p, wc_ref[:, cols], preferred_element_type=F32)
        yb = jnp.dot(z, wr_ref[:, cols], preferred_element_type=F32)
        merged = (_sigmoid(ga_ref[:, gcols].astype(F32)) * ya
                  + _sigmoid(gb_ref[:, gcols].astype(F32)) * yb)
        mg_ref[:, cols] = merged.astype(BF16)
        xp = x1p_ref[prev, :, cols]
        ssq = ssq + jnp.sum(xp * xp, axis=-1, keepdims=True)
    rnorm = lax.rsqrt(ssq * (1.0 / d) + EPS)
    gain = g2_ref[...] * (1.0 + modp_ref[0, 4:5, :])
    shift = modp_ref[0, 3:4, :]
    wh, wl = _split_bf16(wrt_ref[...])
    whl = jnp.concatenate([wh, wl], axis=1)
    acc_hw = jnp.zeros((tm, 2 * LANES), F32)
    acc_lw = jnp.zeros((tm, LANES), F32)
    mg = mg_ref[...]
    for n in range(nchunk):
        cols = slice(n * MIX_TN, (n + 1) * MIX_TN)
        acc = jnp.dot(mg, wo_ref[:, cols], preferred_element_type=F32)
        x1 = x_ref[:, cols] + mod_ref[0, 2:3, cols] * acc
        o_ref[:, cols] = x1
        x1p_ref[slot, :, cols] = x1
        if n % 2 == 0:
            c = (n // 2) * MIX_TN
            halves = []
            for c0 in (c, c + HALF):
                hcols = slice(c0, c0 + MIX_TN)
                h2 = x1p_ref[prev, :, hcols] * rnorm * gain[:, hcols] + shift[:, hcols]
                hh, hl = _split_bf16(h2)
                acc_hw = acc_hw + jnp.dot(hh, whl[hcols, :], preferred_element_type=F32)
                acc_lw = acc_lw + jnp.dot(hl, wh[hcols, :], preferred_element_type=F32)
                halves.append(h2)
            hp_ref[:, c:c + MIX_TN] = _pack_pair(halves[0], halves[1])
    logits = (acc_hw[:, :LANES] + (acc_hw[:, LANES:] + acc_lw)) + brt_ref[...]
    ef_ref[...], rw_ref[...] = _route_from_logits(logits)


def _mix(u, z, conv_w, wc_bf, wr_bf, wo_bf, x2, mod3, g2, w_rt, b_rt, seq):
    t, d = x2.shape
    tm = 256
    nt = t // tm
    hb = tm // HALO
    per_b = seq // tm

    def cur(i):
        return jnp.minimum(i, nt - 1)

    def prev(i):
        return jnp.maximum(i - 1, 0)

    def ublk(c):
        return pl.BlockSpec((tm, COL_BLK), lambda i, c=c: (cur(i), c))

    def halo(c):
        return pl.BlockSpec((HALO, COL_BLK), lambda i, c=c: (jnp.maximum(cur(i) * hb - 1, 0), c))

    def resident(shape):
        return pl.BlockSpec(shape, lambda i: (0, 0), pipeline_mode=pl.Buffered(1))

    return pl.pallas_call(
        functools.partial(_mix_kernel, tiles_per_seq=per_b, n_tiles=nt),
        grid=(nt + 1,),
        in_specs=[ublk(CB_BLK), ublk(CC_BLK), ublk(CX_BLK), halo(CC_BLK), halo(CX_BLK),
                  pl.BlockSpec((tm, z.shape[1]), lambda i: (cur(i), 0)),
                  ublk(GA_BLK), ublk(GA_BLK + 1), ublk(GB_BLK), ublk(GB_BLK + 1),
                  pl.BlockSpec((CONV_K, CONV_CH), lambda i: (0, 0)),
                  resident(wc_bf.shape), resident(wr_bf.shape), resident(wo_bf.shape),
                  pl.BlockSpec((tm, d), lambda i: (cur(i), 0)),
                  pl.BlockSpec((1, 6, d), lambda i: (cur(i) // per_b, 0, 0)),
                  pl.BlockSpec((1, 6, d), lambda i: (prev(i) // per_b, 0, 0)),
                  pl.BlockSpec((1, d), lambda i: (0, 0)),
                  pl.BlockSpec((d, LANES), lambda i: (0, 0)),
                  pl.BlockSpec((1, LANES), lambda i: (0, 0))],
        out_specs=[pl.BlockSpec((tm, d), lambda i: (cur(i), 0)),
                   pl.BlockSpec((tm, HALF), lambda i: (prev(i), 0)),
                   pl.BlockSpec((tm, LANES), lambda i: (prev(i), 0)),
                   pl.BlockSpec((tm, LANES), lambda i: (prev(i), 0))],
        out_shape=[jax.ShapeDtypeStruct((t, d), F32),
                   jax.ShapeDtypeStruct((t, HALF), U32),
                   jax.ShapeDtypeStruct((t, LANES), F32),
                   jax.ShapeDtypeStruct((t, LANES), F32)],
        scratch_shapes=[pltpu.VMEM((tm + HALO, CONV_CH), F32),
                        pltpu.VMEM((tm, d), BF16),
                        pltpu.VMEM((2, tm, d), F32)],
        compiler_params=_cparams(("arbitrary",)),
        name="mix",
    )(u, u, u, u, u, z, u, u, u, u, conv_w, wc_bf, wr_bf, wo_bf, x2, mod3, mod3, g2, w_rt, b_rt)


ROUTE_E0 = N_GROUPS


U32 = jnp.uint32
HALF = D_MODEL // 2


def _pack_pair(lo, hi):
    ulo = lax.bitcast_convert_type(lo.astype(BF16).astype(F32), U32)
    uhi = lax.bitcast_convert_type(hi.astype(BF16).astype(F32), U32)
    return (ulo >> 16) | uhi


def _unpack_pair(w):
    lo = lax.bitcast_convert_type(w << 16, F32)
    hi = lax.bitcast_convert_type(w & jnp.uint32(0xFFFF0000), F32)
    return lo, hi


ROW_SUB = HALF // LANES


def _row_slab(r):
    if isinstance(r, int):
        return pl.ds(r * ROW_SUB, ROW_SUB)
    return pl.ds(pl.multiple_of(r * ROW_SUB, ROW_SUB), ROW_SUB)


def _store_rows(ref, row0, v):
    for c in range(ROW_SUB):
        ref[pl.ds(row0 * ROW_SUB + c, v.shape[0], stride=ROW_SUB), :] = v[:, c * LANES:(c + 1) * LANES]


def _load_rows(ref, row0, n):
    return jnp.concatenate(
        [ref[pl.ds(row0 * ROW_SUB + c, n, stride=ROW_SUB), :] for c in range(ROW_SUB)], axis=1)


def _route_from_logits(logits):
    lane = lax.broadcasted_iota(I32, logits.shape, 1).astype(F32)
    neg = -jnp.inf
    big = float(LANES)
    gl = jnp.where(lane < N_GROUPS, logits, neg)
    gmax = jnp.max(gl, axis=-1, keepdims=True)
    gsel = jnp.min(jnp.where(gl == gmax, lane, big), axis=-1, keepdims=True)
    gsum = jnp.sum(jnp.exp(gl - gmax), axis=-1, keepdims=True)
    g_w = 1.0 / gsum
    lo_lane = ROUTE_E0 + EXPERTS_PER_GROUP * gsel
    emask = (lane >= lo_lane) & (lane < lo_lane + EXPERTS_PER_GROUP)
    el = jnp.where(emask, logits, neg)
    m1 = jnp.max(el, axis=-1, keepdims=True)
    i1 = jnp.min(jnp.where(el == m1, lane, big), axis=-1, keepdims=True)
    el2 = jnp.where(lane == i1, neg, el)
    m2 = jnp.max(el2, axis=-1, keepdims=True)
    i2 = jnp.min(jnp.where(el2 == m2, lane, big), axis=-1, keepdims=True)
    esum = jnp.sum(jnp.exp(el - m1), axis=-1, keepdims=True)
    p1 = 1.0 / esum
    p2 = jnp.exp(m2 - m1) / esum
    tot = p1 + p2
    c1 = g_w * (p1 / tot)
    c2 = g_w * (p2 / tot)
    ef = jnp.where(lane == 0.0, i1 - ROUTE_E0, jnp.where(lane == 1.0, i2 - ROUTE_E0, 0.0))
    cw = jnp.where(lane == 0.0, c1, jnp.where(lane == 1.0, c2, 0.0))
    return ef, cw


META_TB = 512
BLK_ROWS = 256


def _meta_kernel(ef_ref, dest_ref, blk_ref, pref_ref):
    t = ef_ref.shape[0]
    nb = t // META_TB
    lane = lax.broadcasted_iota(I32, (META_TB, LANES), 1).astype(F32)
    r_i = lax.broadcasted_iota(I32, (META_TB, META_TB), 0)
    c_i = lax.broadcasted_iota(I32, (META_TB, META_TB), 1)
    tril = jnp.where(r_i > c_i, 1.0, 0.0).astype(BF16)

    def onehots(i):
        ef = ef_ref[pl.ds(i * META_TB, META_TB), :]
        oh1 = jnp.where(lane == ef[:, 0:1], 1.0, 0.0)
        oh2 = jnp.where(lane == ef[:, 1:2], 1.0, 0.0)
        return oh1, oh2

    def pass1(i, carry):
        oh1, oh2 = onehots(i)
        cnt = oh1 + oh2
        pref = jnp.dot(tril, cnt.astype(BF16), preferred_element_type=F32) + carry
        pref_ref[pl.ds(i * META_TB, META_TB), :] = pref
        return carry + jnp.sum(cnt, axis=0, keepdims=True)

    counts = lax.fori_loop(0, nb, pass1, jnp.zeros((1, LANES), F32))
    nblk = jnp.floor((counts + (EXPERT_BLOCK - 1)) * (1.0 / EXPERT_BLOCK))
    u_r = lax.broadcasted_iota(I32, (LANES, LANES), 0)
    u_c = lax.broadcasted_iota(I32, (LANES, LANES), 1)
    upper = jnp.where(u_r <= u_c, 1.0, 0.0).astype(BF16)
    nblk8 = jnp.broadcast_to(nblk, (8, LANES))
    pend_blk = jnp.dot(nblk8.astype(BF16), upper, preferred_element_type=F32)[0:1, :]
    pstart_blk = pend_blk - nblk
    pstart = pstart_blk * float(EXPERT_BLOCK)

    def pass2(i, _):
        oh1, oh2 = onehots(i)
        base = pref_ref[pl.ds(i * META_TB, META_TB), :] + pstart
        d1 = jnp.sum(oh1 * base, axis=-1, keepdims=True)
        d2 = jnp.sum(oh2 * base, axis=-1, keepdims=True)
        dest = jnp.where(lane == 0.0, d1, jnp.where(lane == 1.0, d2, 0.0))
        dest_ref[pl.ds(i * META_TB, META_TB), :] = dest.astype(I32)
        return 0

    lax.fori_loop(0, nb, pass2, 0)

    blane = lax.broadcasted_iota(I32, (BLK_ROWS, LANES), 1).astype(F32)
    brow = lax.broadcasted_iota(I32, (BLK_ROWS, LANES), 0).astype(F32)
    emask = blane < N_EXPERTS
    owner = jnp.sum(jnp.where(emask & (pend_blk <= brow), 1.0, 0.0), axis=-1, keepdims=True)
    owner = jnp.minimum(owner, float(N_EXPERTS - 1))
    total = jnp.sum(jnp.where(emask, nblk, 0.0), axis=-1, keepdims=True)
    used = jnp.where(brow < total, 1.0, 0.0)
    mine = emask & (blane == owner)
    cnt_b = jnp.sum(jnp.where(mine, counts, 0.0), axis=-1, keepdims=True)
    ps_b = jnp.sum(jnp.where(mine, pstart_blk, 0.0), axis=-1, keepdims=True)
    nvalid = jnp.clip(cnt_b - (brow - ps_b) * float(EXPERT_BLOCK), 0.0, float(EXPERT_BLOCK))
    pad_lo = nvalid * used
    blk_ref[...] = jnp.where(blane == 0.0, owner,
                             jnp.where(blane == 1.0, used,
                                       jnp.where(blane == 2.0, pad_lo, 0.0))).astype(I32)


def _meta(ef):
    t = ef.shape[0]
    return pl.pallas_call(
        _meta_kernel,
        out_shape=[jax.ShapeDtypeStruct((t, LANES), I32),
                   jax.ShapeDtypeStruct((BLK_ROWS, LANES), I32)],
        scratch_shapes=[pltpu.VMEM((t, LANES), F32)],
        compiler_params=pltpu.CompilerParams(vmem_limit_bytes=VMEM_LIMIT),
        name="meta",
    )(ef)


DISP_TM = 256
TOP_K = 2


def _dispatch_kernel(pad_lo_ref, dest_ref, h_ref, xbuf_ref, stage, zrow, sems, zsem):
    i = pl.program_id(0)
    n = pl.num_programs(0)
    tm = h_ref.shape[0]
    n_blk = pad_lo_ref.shape[0]
    slot = i % 2

    def pad_rows(action):
        def blk(b, _):
            lo = pad_lo_ref[b]
            base = pl.multiple_of(b * EXPERT_BLOCK, EXPERT_BLOCK)

            @pl.when(lo == 0)
            def _():
                whole = pl.ds(pl.multiple_of(base * ROW_SUB, EXPERT_BLOCK * ROW_SUB),
                              EXPERT_BLOCK * ROW_SUB)
                action(pltpu.make_async_copy(zrow, xbuf_ref.at[whole], zsem))

            @pl.when(lo > 0)
            def _():
                def row(r, _):
                    action(pltpu.make_async_copy(zrow.at[_row_slab(0)],
                                                 xbuf_ref.at[_row_slab(base + r)], zsem))
                    return 0
                lax.fori_loop(lo, EXPERT_BLOCK, row, 0)
            return 0
        lax.fori_loop(0, n_blk, blk, 0)

    @pl.when(i == 0)
    def _():
        zrow[...] = jnp.zeros_like(zrow)
        pad_rows(lambda cp: cp.start())

    _store_rows(stage.at[slot], 0, h_ref[...])
    for j in range(tm):
        for k in range(TOP_K):
            d = dest_ref[0, 0, TOP_K * j + k]
            pltpu.make_async_copy(stage.at[slot, _row_slab(j)], xbuf_ref.at[_row_slab(d)],
                                  sems.at[slot]).start(priority=k)

    def wait_tile(s):
        for _ in range(TOP_K):
            pltpu.make_async_copy(stage.at[s], xbuf_ref.at[pl.ds(0, tm * ROW_SUB)], sems.at[s]).wait()

    @pl.when(i > 0)
    def _():
        wait_tile(1 - slot)

    @pl.when(i == n - 1)
    def _():
        wait_tile(slot)
        pad_rows(lambda cp: cp.wait())


def _dispatch(pad_lo, dest3, hp, n_rows):
    t, w = hp.shape
    tm = DISP_TM
    grid_spec = pltpu.PrefetchScalarGridSpec(
        num_scalar_prefetch=1,
        grid=(t // tm,),
        in_specs=[pl.BlockSpec((1, 1, TOP_K * tm), lambda i, *_: (i, 0, 0), memory_space=pltpu.SMEM),
                  pl.BlockSpec((tm, w), lambda i, *_: (i, 0))],
        out_specs=pl.BlockSpec(memory_space=pl.ANY),
        scratch_shapes=[pltpu.VMEM((2, tm * ROW_SUB, LANES), U32),
                        pltpu.VMEM((EXPERT_BLOCK * ROW_SUB, LANES), U32),
                        pltpu.SemaphoreType.DMA((2,)), pltpu.SemaphoreType.DMA(())],
    )
    return pl.pallas_call(
        _dispatch_kernel,
        grid_spec=grid_spec,
        out_shape=jax.ShapeDtypeStruct((n_rows * ROW_SUB, LANES), U32),
        compiler_params=_cparams(("arbitrary",)),
        name="dispatch",
    )(pad_lo, dest3, hp)


MOE_SUB = 4
W_RING = 2
CAST_ROWS = 256
SWITCH_AFTER = 4


def _moe_kernel(run_e_ref, run_id_ref, used_ref, nruns_ref, x_ref, wg_hbm, wu_hbm, wd_hbm, o_ref,
                wg_st, wu_st, wd_st, wg_bf, wu_bf, wd_bf, sems, state):
    step = pl.program_id(0)
    n_runs = nruns_ref[0]
    streams = ((wg_hbm, wg_st, wg_bf), (wu_hbm, wu_st, wu_bf), (wd_hbm, wd_st, wd_bf))

    def weight_copy(m, run):
        hbm, stage, _ = streams[m]
        return pltpu.make_async_copy(hbm.at[run_e_ref[run]], stage, sems.at[m])

    @pl.when(step == 0)
    def _():
        state[0] = 0
        state[1] = 0
        for m in range(len(streams)):
            weight_copy(m, 0).start(priority=1)

    def switch_to(run, b):
        ring = lax.rem(run, W_RING)
        for m, (_, stage, bf) in enumerate(streams):
            weight_copy(m, run).wait()

            def body(i, _, stage=stage, bf=bf):
                rows = pl.ds(pl.multiple_of(i * CAST_ROWS, CAST_ROWS), CAST_ROWS)
                bf[ring, rows, :] = stage[rows, :].astype(BF16)
                return 0
            lax.fori_loop(0, stage.shape[0] // CAST_ROWS, body, 0)

            @pl.when(run + 1 < n_runs)
            def _(m=m):
                weight_copy(m, run + 1).start(priority=1)
        state[0] = run + 1
        state[1] = b + SWITCH_AFTER

    def run_block(b, row0):
        run = run_id_ref[b]
        pl.when(state[0] == run)(functools.partial(switch_to, run, b))
        pl.when((state[0] == run + 1) & (run + 1 < n_runs) & (b >= state[1]))(
            functools.partial(switch_to, run + 1, b))

        ring = lax.rem(run, W_RING)
        xlo, xhi = _unpack_pair(_load_rows(x_ref, row0, EXPERT_BLOCK))
        xb = jnp.concatenate([xlo.astype(BF16), xhi.astype(BF16)], axis=1)
        g = jnp.dot(xb, wg_bf[ring], preferred_element_type=F32)
        u = jnp.dot(xb, wu_bf[ring], preferred_element_type=F32)
        hmid = (_silu(g) * u).astype(BF16)
        y = jnp.dot(hmid, wd_bf[ring], preferred_element_type=F32)
        _store_rows(o_ref, row0, _pack_pair(y[:, :HALF], y[:, HALF:]))

    for j in range(MOE_SUB):
        b = step * MOE_SUB + j
        row0 = j * EXPERT_BLOCK
        pl.when(used_ref[b] == 1)(functools.partial(run_block, b, row0))

        @pl.when(used_ref[b] == 0)
        def _():
            o_ref[pl.ds(row0 * ROW_SUB, EXPERT_BLOCK * ROW_SUB), :] = jnp.zeros(
                (EXPERT_BLOCK * ROW_SUB, LANES), U32)


def _moe(run_e, run_id, used, n_runs, xbuf, w_gate, w_up, w_down):
    r = xbuf.shape[0]
    rows = MOE_SUB * EXPERT_BLOCK * ROW_SUB
    d, de = w_gate.shape[1:]

    def xmap(s, run_e_ref, run_id_ref, used_ref, nruns_ref):
        del run_e_ref, run_id_ref, nruns_ref
        return (jnp.where(used_ref[s * MOE_SUB] == 1, s, 0), 0)

    grid_spec = pltpu.PrefetchScalarGridSpec(
        num_scalar_prefetch=4,
        grid=(r // rows,),
        in_specs=[pl.BlockSpec((rows, LANES), xmap),
                  pl.BlockSpec(memory_space=pl.ANY),
                  pl.BlockSpec(memory_space=pl.ANY),
                  pl.BlockSpec(memory_space=pl.ANY)],
        out_specs=pl.BlockSpec((rows, LANES), lambda s, *_: (s, 0)),
        scratch_shapes=[pltpu.VMEM((d, de), F32), pltpu.VMEM((d, de), F32), pltpu.VMEM((de, d), F32),
                        pltpu.VMEM((W_RING, d, de), BF16), pltpu.VMEM((W_RING, d, de), BF16),
                        pltpu.VMEM((W_RING, de, d), BF16),
                        pltpu.SemaphoreType.DMA((3,)), pltpu.SMEM((2,), I32)],
    )
    return pl.pallas_call(
        _moe_kernel,
        grid_spec=grid_spec,
        out_shape=jax.ShapeDtypeStruct((r, LANES), U32),
        compiler_params=_cparams(("arbitrary",), vmem=VMEM_LIMIT_MOE),
        name="moe",
    )(run_e, run_id, used, n_runs, xbuf, w_gate, w_up, w_down)


FIN_TM = 256


def _final_kernel(dest_ref, dnext_ref, x_ref, cw_ref, mod_ref, g_ref, y_hbm, o_ref, ybuf, sems):
    i = pl.program_id(0)
    n = pl.num_programs(0)
    tm = x_ref.shape[0]
    slot = i % 2

    def gather(d_ref, s):
        for j in range(tm):
            for k in range(TOP_K):
                d = d_ref[0, 0, TOP_K * j + k]
                pltpu.make_async_copy(y_hbm.at[_row_slab(d)], ybuf.at[s, k, _row_slab(j)],
                                      sems.at[s]).start(priority=k)

    @pl.when(i == 0)
    def _():
        gather(dest_ref, 0)

    @pl.when(i + 1 < n)
    def _():
        gather(dnext_ref, 1 - slot)

    for k in range(TOP_K):
        pltpu.make_async_copy(y_hbm.at[pl.ds(0, tm * ROW_SUB)], ybuf.at[slot, k], sems.at[slot]).wait()

    cw = cw_ref[...]
    c0, c1 = cw[:, 0:1], cw[:, 1:2]
    lo0, hi0 = _unpack_pair(_load_rows(ybuf.at[slot, 0], 0, tm))
    lo1, hi1 = _unpack_pair(_load_rows(ybuf.at[slot, 1], 0, tm))
    g2 = mod_ref[0, 5:6, :]
    xlo = x_ref[:, :HALF] + g2[:, :HALF] * (lo0 * c0 + lo1 * c1)
    xhi = x_ref[:, HALF:] + g2[:, HALF:] * (hi0 * c0 + hi1 * c1)
    ms = (jnp.sum(xlo * xlo, axis=-1, keepdims=True)
          + jnp.sum(xhi * xhi, axis=-1, keepdims=True)) * (1.0 / D_MODEL)
    r = lax.rsqrt(ms + EPS)
    o_ref[:, :HALF] = xlo * r * g_ref[:, :HALF]
    o_ref[:, HALF:] = xhi * r * g_ref[:, HALF:]


def _final(dest3, x1, cw, mod3, g, ybuf, seq):
    t, d = x1.shape
    tm = FIN_TM
    nt = t // tm
    per_b = seq // tm

    def dspec(shift):
        return pl.BlockSpec((1, 1, TOP_K * tm), lambda i: (jnp.minimum(i + shift, nt - 1), 0, 0),
                            memory_space=pltpu.SMEM)

    return pl.pallas_call(
        _final_kernel,
        grid=(nt,),
        in_specs=[dspec(0), dspec(1),
                  pl.BlockSpec((tm, d), lambda i: (i, 0)),
                  pl.BlockSpec((tm, LANES), lambda i: (i, 0)),
                  pl.BlockSpec((1, 6, d), lambda i: (i // per_b, 0, 0)),
                  pl.BlockSpec((1, d), lambda i: (0, 0)),
                  pl.BlockSpec(memory_space=pl.ANY)],
        out_specs=pl.BlockSpec((tm, d), lambda i: (i, 0)),
        out_shape=jax.ShapeDtypeStruct((t, d), F32),
        scratch_shapes=[pltpu.VMEM((2, TOP_K, tm * ROW_SUB, LANES), U32),
                        pltpu.SemaphoreType.DMA((2,))],
        compiler_params=_cparams(("arbitrary",)),
        name="final",
    )(dest3, dest3, x1, cw, mod3, g, ybuf)


def kernel(x, c, w_ada, b_ada, norm1_g, w_in, conv_w, w_conv_out, w_ret_out, w_o, norm2_g,
           w_router_group, b_router_group, w_router_expert, b_router_expert,
           w_gate, w_up, w_down, norm_f_g):
    batch, seq, d = x.shape
    t = batch * seq
    depth = w_ada.shape[0]
    assert d == D_MODEL and w_in.shape[2] == IN_COLS and w_gate.shape[1] == N_EXPERTS
    assert depth == 1, "the final kernel fuses the last rmsnorm into the single layer"
    n_rows = t * 2 + N_EXPERTS * EXPERT_BLOCK
    n_blk = n_rows // EXPERT_BLOCK
    assert n_blk <= BLK_ROWS

    c_pad = jnp.pad(c, ((0, 8 - batch), (0, 0)))
    xs = x.reshape(t, d)
    for l in range(depth):
        mod = _ada(c_pad, w_ada[l], b_ada[l].reshape(1, -1))[:batch]
        mod3 = mod.reshape(batch, 6, d)

        u = _in_proj(xs, mod3, norm1_g[l].reshape(1, d), w_in[l], seq)
        z = _retention(u, batch, seq)
        w_r = jnp.pad(jnp.concatenate([w_router_group[l], w_router_expert[l]], axis=1),
                      ((0, 0), (0, LANES - N_GROUPS - N_EXPERTS)))
        b_r = jnp.pad(jnp.concatenate([b_router_group[l], b_router_expert[l]]),
                      (0, LANES - N_GROUPS - N_EXPERTS)).reshape(1, LANES)
        x1, h2, ef, cw = _mix(u, z, conv_w[l], w_conv_out[l].astype(BF16), w_ret_out[l].astype(BF16),
                              w_o[l].astype(BF16), xs, mod3, norm2_g[l].reshape(1, d), w_r, b_r, seq)

        dest, blk = _meta(ef)
        blk_e = blk[:n_blk, 0]
        used = blk[:n_blk, 1]
        prev_e = jnp.concatenate([jnp.full((1,), -1, I32), blk_e[:-1]])
        first = ((blk_e != prev_e) & (used == 1)).astype(I32)
        run_id = jnp.maximum(jnp.cumsum(first) - 1, 0).astype(I32)
        n_runs = jnp.sum(first).astype(I32).reshape(1)
        present = jnp.any((blk_e[None, :] == jnp.arange(N_EXPERTS, dtype=I32)[:, None])
                          & (used[None, :] == 1), axis=1)
        eids = jnp.arange(N_EXPERTS, dtype=I32)
        rank = jnp.cumsum(present.astype(I32)) - 1
        run_e = jnp.sum(jnp.where(present[None, :] & (rank[None, :] == eids[:, None]),
                                  eids[None, :], 0), axis=1).astype(I32)

        dest2 = dest[:, :TOP_K]
        xbuf = _dispatch(blk[:n_blk, 2], dest2.reshape(t // DISP_TM, 1, TOP_K * DISP_TM), h2, n_rows)
        ybuf = _moe(run_e, run_id, used, n_runs, xbuf, w_gate[l], w_up[l], w_down[l])
        xs = _final(dest2.reshape(t // FIN_TM, 1, TOP_K * FIN_TM), x1, cw, mod3,
                    norm_f_g.reshape(1, d), ybuf, seq)
    return xs.reshape(batch, seq, d)
```

```python
import functools

import numpy as np
import jax
import jax.numpy as jnp
from jax import lax
from jax.experimental import pallas as pl
from jax.experimental.pallas import tpu as pltpu

F32 = jnp.float32
BF16 = jnp.bfloat16
I32 = jnp.int32

D_MODEL = 2048
CONV_CH = 1024
CONV_K = 3
RET_HEADS = 8
RET_DK = 128
RET_DV = 256
CHUNK = 128
ROPE_BASE = 10000.0
N_GROUPS = 4
EXPERTS_PER_GROUP = 8
N_EXPERTS = 32
EXPERT_BLOCK = 128
EPS = 1e-6
IN_COLS = 13312
LANES = 128
COL_BLK = 1024
CB_BLK, CC_BLK, CX_BLK, Q_BLK, K_BLK, V_BLK, RG_BLK, GA_BLK, GB_BLK = 0, 1, 2, 3, 4, 5, 7, 9, 11

V7X_VMEM_BYTES = 64 * 1024 * 1024
VMEM_LIMIT = V7X_VMEM_BYTES - 8 * 1024 * 1024
VMEM_LIMIT_MOE = V7X_VMEM_BYTES - 4 * 1024 * 1024


def _cparams(sem, vmem=VMEM_LIMIT):
    return pltpu.CompilerParams(dimension_semantics=sem, vmem_limit_bytes=vmem)


def _sigmoid(v):
    return 1.0 / (1.0 + jnp.exp(-v))


def _silu(v):
    return v * _sigmoid(v)


def _split_bf16(v):
    hi = v.astype(BF16)
    lo = (v - hi.astype(F32)).astype(BF16)
    return hi, lo


def _ada_kernel(c_ref, w_ref, b_ref, o_ref):
    rows = c_ref.shape[0]
    ch, cl = _split_bf16(_silu(c_ref[...]))
    wh, wl = _split_bf16(w_ref[...])
    p = jnp.dot(jnp.concatenate([ch, cl], axis=0), wh, preferred_element_type=F32)
    o_ref[...] = (p[:rows] + (p[rows:] + jnp.dot(ch, wl, preferred_element_type=F32))) + b_ref[...]


def _ada(c_pad, w, b):
    rows, d = c_pad.shape
    n = w.shape[1]
    tn = 1024
    return pl.pallas_call(
        _ada_kernel,
        grid=(n // tn,),
        in_specs=[pl.BlockSpec((rows, d), lambda j: (0, 0)),
                  pl.BlockSpec((d, tn), lambda j: (0, j)),
                  pl.BlockSpec((1, tn), lambda j: (0, j))],
        out_specs=pl.BlockSpec((rows, tn), lambda j: (0, j)),
        out_shape=jax.ShapeDtypeStruct((rows, n), F32),
        compiler_params=_cparams(("arbitrary",)),
        name="ada",
    )(c_pad, w, b)


NORM_ROWS = 64


IN_SPLIT = 2


def _in_kernel(x_ref, mod_ref, g_ref, w_ref, o_ref, h_ref):
    i, j = pl.program_id(0), pl.program_id(1)
    nj = pl.num_programs(1)
    tm, tn = o_ref.shape
    slot = i % 2

    def norm_rows(dst, rows):
        gain = g_ref[...] * (1.0 + mod_ref[0, 1:2, :])
        x = x_ref[rows, :]
        r = lax.rsqrt(jnp.mean(x * x, axis=-1, keepdims=True) + EPS)
        h_ref[dst, rows, :] = (x * r * gain + mod_ref[0, 0:1, :]).astype(BF16)

    @pl.when((i == 0) & (j == 0))
    def _():
        def body(k, _):
            norm_rows(0, pl.ds(pl.multiple_of(k * NORM_ROWS, NORM_ROWS), NORM_ROWS))
            return 0
        lax.fori_loop(0, tm // NORM_ROWS, body, 0)

    @pl.when(j < nj - 1)
    def _():
        o_ref[...] = jnp.dot(h_ref[slot], w_ref[...].astype(BF16),
                             preferred_element_type=F32).astype(o_ref.dtype)

    @pl.when(j == nj - 1)
    def _():
        h = h_ref[slot]
        cw, rw = tn // IN_SPLIT, tm // IN_SPLIT
        for q in range(IN_SPLIT):
            cols = slice(q * cw, (q + 1) * cw)
            o_ref[:, cols] = jnp.dot(h, w_ref[:, cols].astype(BF16),
                                     preferred_element_type=F32).astype(o_ref.dtype)
            for k in range(rw // NORM_ROWS):
                norm_rows(1 - slot, slice(q * rw + k * NORM_ROWS, q * rw + (k + 1) * NORM_ROWS))


def _in_proj(x2, mod3, g, w, seq):
    t, d = x2.shape
    n = w.shape[1]
    tm, tn = 1024, 1024
    nt, nj = t // tm, n // tn
    per_b = seq // tm

    def xtile(i, j):
        return jnp.minimum(i + (j == nj - 1).astype(jnp.int32), nt - 1)

    return pl.pallas_call(
        _in_kernel,
        grid=(nt, nj),
        in_specs=[pl.BlockSpec((tm, d), lambda i, j: (xtile(i, j), 0)),
                  pl.BlockSpec((1, 6, d), lambda i, j: (xtile(i, j) // per_b, 0, 0)),
                  pl.BlockSpec((1, d), lambda i, j: (0, 0)),
                  pl.BlockSpec((d, tn), lambda i, j: (0, j))],
        out_specs=pl.BlockSpec((tm, tn), lambda i, j: (i, j)),
        out_shape=jax.ShapeDtypeStruct((t, n), BF16),
        scratch_shapes=[pltpu.VMEM((2, tm, d), BF16)],
        compiler_params=_cparams(("arbitrary", "arbitrary")),
        name="in_proj",
    )(x2, mod3, g, w)


RET_CPS = 4


def _ret_kernel(q_ref, k_ref, v0_ref, v1_ref, rg0_ref, rg1_ref, cos_ref, sin_ref,
                idec_ref, qdec_ref, kdec_ref, wc_ref, wr_ref, wo_ref,
                o_ref, wcb_ref, wrb_ref, wob_ref, state_ref, *, chunk_decay):
    @pl.when(pl.program_id(1) == 0)
    def _():
        state_ref[...] = jnp.zeros_like(state_ref)

    wcb_ref[...] = wc_ref[...].astype(BF16)
    wrb_ref[...] = wr_ref[...].astype(BF16)
    wob_ref[...] = wo_ref[...].astype(BF16)

    def rot(t, cos, sin):
        return t * cos + pltpu.roll(t, RET_DK // 2, axis=1) * sin

    for ci in range(RET_CPS):
        rows = slice(ci * CHUNK, (ci + 1) * CHUNK)
        cos = cos_ref[rows, :]
        sin = sin_ref[rows, :]
        for h in range(RET_HEADS):
            v_ref, rg_ref = (v0_ref, rg0_ref) if h < RET_HEADS // 2 else (v1_ref, rg1_ref)
            vc = (h % (RET_HEADS // 2)) * RET_DV
            qh = rot(q_ref[rows, h * RET_DK:(h + 1) * RET_DK].astype(F32), cos, sin)
            kh = rot(k_ref[rows, h * RET_DK:(h + 1) * RET_DK].astype(F32), cos, sin) * (RET_DK ** -0.5)
            vh = v_ref[rows, vc:vc + RET_DV]
            qb = qh.astype(BF16)
            kb = kh.astype(BF16)
            scores = lax.dot_general(qb, kb, (((1,), (1,)), ((), ())),
                                     preferred_element_type=F32) * idec_ref[h]
            inner = jnp.dot(scores.astype(BF16), vh, preferred_element_type=F32)
            state = state_ref[h]
            cross = jnp.dot((qh * qdec_ref[h]).astype(BF16), state.astype(BF16),
                            preferred_element_type=F32)
            kv = lax.dot_general((kh * kdec_ref[h]).astype(BF16), vh, (((0,), (0,)), ((), ())),
                                 preferred_element_type=F32)
            state_ref[h] = state * chunk_decay[h] + kv
            r = inner + cross
            r = r * lax.rsqrt(jnp.mean(r * r, axis=-1, keepdims=True) + EPS)
            gate = _silu(rg_ref[rows, vc:vc + RET_DV].astype(F32))
            o_ref[rows, h * RET_DV:(h + 1) * RET_DV] = (gate * r).astype(o_ref.dtype)


def _retention(u, w_conv_out, w_ret_out, w_o, batch, seq):
    t = u.shape[0]
    inv = ROPE_BASE ** (-jnp.arange(0, RET_DK, 2, dtype=F32) / RET_DK)
    ang = jnp.arange(seq, dtype=F32)[:, None] * inv[None, :]
    cos, sin = jnp.cos(ang), jnp.sin(ang)
    cos_t = jnp.concatenate([cos, cos], axis=-1)
    sin_t = jnp.concatenate([-sin, sin], axis=-1)
    log_g = jnp.log1p(-(2.0 ** (-5.0 - jnp.arange(RET_HEADS, dtype=F32))))
    i = jnp.arange(CHUNK, dtype=F32)
    diff = i[:, None] - i[None, :]
    idec = jnp.where(diff >= 0, jnp.exp(log_g[:, None, None] * jnp.maximum(diff, 0.0)), 0.0)
    qdec = jnp.broadcast_to(jnp.exp(log_g[:, None] * (i + 1.0))[:, :, None],
                            (RET_HEADS, CHUNK, RET_DK))
    kdec = jnp.broadcast_to(jnp.exp(log_g[:, None] * (CHUNK - 1.0 - i))[:, :, None],
                            (RET_HEADS, CHUNK, RET_DK))
    log_g_np = np.log1p(-(2.0 ** (-5.0 - np.arange(RET_HEADS, dtype=np.float32)))).astype(np.float32)
    chunk_decay = tuple(float(np.exp(np.float32(lg * np.float32(CHUNK)))) for lg in log_g_np)

    rows = RET_CPS * CHUNK
    nst = seq // rows

    def ublk(c):
        return pl.BlockSpec((rows, COL_BLK), lambda b, n, c=c: (b * nst + n, c))

    def table():
        return pl.BlockSpec((RET_HEADS, CHUNK, RET_DK), lambda b, n: (0, 0, 0))

    def slab(w):
        n_steps = batch * nst
        assert w.shape[0] % (8 * n_steps) == 0
        return pl.BlockSpec((w.shape[0] // n_steps, w.shape[1]), lambda b, n: (b * nst + n, 0))

    weights = (w_conv_out, w_ret_out, w_o)
    return pl.pallas_call(
        functools.partial(_ret_kernel, chunk_decay=chunk_decay),
        grid=(batch, nst),
        in_specs=[ublk(Q_BLK), ublk(K_BLK), ublk(V_BLK), ublk(V_BLK + 1),
                  ublk(RG_BLK), ublk(RG_BLK + 1),
                  pl.BlockSpec((rows, RET_DK), lambda b, n: (n, 0)),
                  pl.BlockSpec((rows, RET_DK), lambda b, n: (n, 0)),
                  table(), table(), table()] + [slab(w) for w in weights],
        out_specs=[pl.BlockSpec((rows, RET_HEADS * RET_DV), lambda b, n: (b * nst + n, 0))]
        + [slab(w) for w in weights],
        out_shape=[jax.ShapeDtypeStruct((t, RET_HEADS * RET_DV), BF16)]
        + [jax.ShapeDtypeStruct(w.shape, BF16) for w in weights],
        scratch_shapes=[pltpu.VMEM((RET_HEADS, RET_DK, RET_DV), F32)],
        compiler_params=_cparams(("arbitrary", "arbitrary")),
        name="retention",
    )(u, u, u, u, u, u, cos_t, sin_t, idec, qdec, kdec, *weights)


HALO = 8


MIX_TN = 512


def _mix_kernel(cb_ref, cc_ref, cx_ref, hcc_ref, hcx_ref, z_ref, ga0_ref, ga1_ref, gb0_ref, gb1_ref,
                cw_ref, wc_ref, wr_ref, wo_ref, x_ref, mod_ref, modp_ref, g2_ref, wrt_ref, brt_ref,
                o_ref, hp_ref, ef_ref, rw_ref, m_ref, mg_ref, x1p_ref, *, tiles_per_seq, n_tiles):
    tm = cb_ref.shape[0]
    d = o_ref.shape[1]
    step = pl.program_id(0)

    slot = step % 2

    @pl.when(step == 0)
    def _():
        x1p_ref[1] = jnp.zeros(x1p_ref.shape[1:], F32)

    m = cc_ref[...].astype(F32) * cx_ref[...].astype(F32)
    halo = hcc_ref[...].astype(F32) * hcx_ref[...].astype(F32)
    first = (jnp.minimum(step, n_tiles - 1) % tiles_per_seq) == 0
    m_ref[0:HALO, :] = jnp.where(first, 0.0, halo)
    m_ref[HALO:HALO + tm, :] = m
    conv = (m_ref[HALO - 2:HALO - 2 + tm, :] * cw_ref[0:1, :]
            + m_ref[HALO - 1:HALO - 1 + tm, :] * cw_ref[1:2, :]
            + m * cw_ref[2:3, :])
    p = (cb_ref[...].astype(F32) * conv).astype(BF16)
    z = z_ref[...]
    nchunk = d // MIX_TN
    prev = 1 - slot
    ssq = jnp.zeros((tm, 1), F32)
    for n in range(nchunk):
        cols = slice(n * MIX_TN, (n + 1) * MIX_TN)
        ga_ref, gb_ref = (ga0_ref, gb0_ref) if n < COL_BLK // MIX_TN else (ga1_ref, gb1_ref)
        gcols = slice((n * MIX_TN) % COL_BLK, (n * MIX_TN) % COL_BLK + MIX_TN)
        ya = jnp.dot(p, wc_ref[:, cols], preferred_element_type=F32)
        yb = jnp.dot(z, wr_ref[:, cols], preferred_element_type=F32)
        merged = (_sigmoid(ga_ref[:, gcols].astype(F32)) * ya
                  + _sigmoid(gb_ref[:, gcols].astype(F32)) * yb)
        mg_ref[:, cols] = merged.astype(BF16)
        xp = x1p_ref[prev, :, cols]
        ssq = ssq + jnp.sum(xp * xp, axis=-1, keepdims=True)
    rnorm = lax.rsqrt(ssq * (1.0 / d) + EPS)
    gain = g2_ref[...] * (1.0 + modp_ref[0, 4:5, :])
    shift = modp_ref[0, 3:4, :]
    wh, wl = _split_bf16(wrt_ref[...])
    whl = jnp.concatenate([wh, wl], axis=1)
    acc_hw = jnp.zeros((tm, 2 * LANES), F32)
    acc_lw = jnp.zeros((tm, LANES), F32)
    mg = mg_ref[...]
    for n in range(nchunk):
        cols = slice(n * MIX_TN, (n + 1) * MIX_TN)
        acc = jnp.dot(mg, wo_ref[:, cols], preferred_element_type=F32)
        x1 = x_ref[:, cols] + mod_ref[0, 2:3, cols] * acc
        o_ref[:, cols] = x1
        x1p_ref[slot, :, cols] = x1
        if n % 2 == 0:
            c = (n // 2) * MIX_TN
            halves = []
            for c0 in (c, c + HALF):
                hcols = slice(c0, c0 + MIX_TN)
                h2 = x1p_ref[prev, :, hcols] * rnorm * gain[:, hcols] + shift[:, hcols]
                hh, hl = _split_bf16(h2)
                acc_hw = acc_hw + jnp.dot(hh, whl[hcols, :], preferred_element_type=F32)
                acc_lw = acc_lw + jnp.dot(hl, wh[hcols, :], preferred_element_type=F32)
                halves.append(h2)
            hp_ref[:, c:c + MIX_TN] = _pack_pair(halves[0], halves[1])
    logits = (acc_hw[:, :LANES] + (acc_hw[:, LANES:] + acc_lw)) + brt_ref[...]
    ef_ref[...], rw_ref[...] = _route_from_logits(logits)


def _mix(u, z, conv_w, wc_bf, wr_bf, wo_bf, x2, mod3, g2, w_rt, b_rt, seq):
    t, d = x2.shape
    tm = 256
    nt = t // tm
    hb = tm // HALO
    per_b = seq // tm

    def cur(i):
        return jnp.minimum(i, nt - 1)

    def prev(i):
        return jnp.maximum(i - 1, 0)

    def ublk(c):
        return pl.BlockSpec((tm, COL_BLK), lambda i, c=c: (cur(i), c))

    def halo(c):
        return pl.BlockSpec((HALO, COL_BLK), lambda i, c=c: (jnp.maximum(cur(i) * hb - 1, 0), c))

    def resident(shape):
        return pl.BlockSpec(shape, lambda i: (0, 0), pipeline_mode=pl.Buffered(1))

    return pl.pallas_call(
        functools.partial(_mix_kernel, tiles_per_seq=per_b, n_tiles=nt),
        grid=(nt + 1,),
        in_specs=[ublk(CB_BLK), ublk(CC_BLK), ublk(CX_BLK), halo(CC_BLK), halo(CX_BLK),
                  pl.BlockSpec((tm, z.shape[1]), lambda i: (cur(i), 0)),
                  ublk(GA_BLK), ublk(GA_BLK + 1), ublk(GB_BLK), ublk(GB_BLK + 1),
                  pl.BlockSpec((CONV_K, CONV_CH), lambda i: (0, 0)),
                  resident(wc_bf.shape), resident(wr_bf.shape), resident(wo_bf.shape),
                  pl.BlockSpec((tm, d), lambda i: (cur(i), 0)),
                  pl.BlockSpec((1, 6, d), lambda i: (cur(i) // per_b, 0, 0)),
                  pl.BlockSpec((1, 6, d), lambda i: (prev(i) // per_b, 0, 0)),
                  pl.BlockSpec((1, d), lambda i: (0, 0)),
                  pl.BlockSpec((d, LANES), lambda i: (0, 0)),
                  pl.BlockSpec((1, LANES), lambda i: (0, 0))],
        out_specs=[pl.BlockSpec((tm, d), lambda i: (cur(i), 0)),
                   pl.BlockSpec((tm, HALF), lambda i: (prev(i), 0)),
                   pl.BlockSpec((tm, LANES), lambda i: (prev(i), 0)),
                   pl.BlockSpec((tm, LANES), lambda i: (prev(i), 0))],
        out_shape=[jax.ShapeDtypeStruct((t, d), F32),
                   jax.ShapeDtypeStruct((t, HALF), U32),
                   jax.ShapeDtypeStruct((t, LANES), F32),
                   jax.ShapeDtypeStruct((t, LANES), F32)],
        scratch_shapes=[pltpu.VMEM((tm + HALO, CONV_CH), F32),
                        pltpu.VMEM((tm, d), BF16),
                        pltpu.VMEM((2, tm, d), F32)],
        compiler_params=_cparams(("arbitrary",)),
        name="mix",
    )(u, u, u, u, u, z, u, u, u, u, conv_w, wc_bf, wr_bf, wo_bf, x2, mod3, mod3, g2, w_rt, b_rt)


ROUTE_E0 = N_GROUPS


U32 = jnp.uint32
HALF = D_MODEL // 2


def _pack_pair(lo, hi):
    ulo = lax.bitcast_convert_type(lo.astype(BF16).astype(F32), U32)
    uhi = lax.bitcast_convert_type(hi.astype(BF16).astype(F32), U32)
    return (ulo >> 16) | uhi


def _unpack_pair(w):
    lo = lax.bitcast_convert_type(w << 16, F32)
    hi = lax.bitcast_convert_type(w & jnp.uint32(0xFFFF0000), F32)
    return lo, hi


ROW_SUB = HALF // LANES


def _row_slab(r):
    if isinstance(r, int):
        return pl.ds(r * ROW_SUB, ROW_SUB)
    return pl.ds(pl.multiple_of(r * ROW_SUB, ROW_SUB), ROW_SUB)


def _store_rows(ref, row0, v):
    for c in range(ROW_SUB):
        ref[pl.ds(row0 * ROW_SUB + c, v.shape[0], stride=ROW_SUB), :] = v[:, c * LANES:(c + 1) * LANES]


def _load_rows(ref, row0, n):
    return jnp.concatenate(
        [ref[pl.ds(row0 * ROW_SUB + c, n, stride=ROW_SUB), :] for c in range(ROW_SUB)], axis=1)


def _route_from_logits(logits):
    lane = lax.broadcasted_iota(I32, logits.shape, 1).astype(F32)
    neg = -jnp.inf
    big = float(LANES)
    gl = jnp.where(lane < N_GROUPS, logits, neg)
    gmax = jnp.max(gl, axis=-1, keepdims=True)
    gsel = jnp.min(jnp.where(gl == gmax, lane, big), axis=-1, keepdims=True)
    gsum = jnp.sum(jnp.exp(gl - gmax), axis=-1, keepdims=True)
    g_w = 1.0 / gsum
    lo_lane = ROUTE_E0 + EXPERTS_PER_GROUP * gsel
    emask = (lane >= lo_lane) & (lane < lo_lane + EXPERTS_PER_GROUP)
    el = jnp.where(emask, logits, neg)
    m1 = jnp.max(el, axis=-1, keepdims=True)
    i1 = jnp.min(jnp.where(el == m1, lane, big), axis=-1, keepdims=True)
    el2 = jnp.where(lane == i1, neg, el)
    m2 = jnp.max(el2, axis=-1, keepdims=True)
    i2 = jnp.min(jnp.where(el2 == m2, lane, big), axis=-1, keepdims=True)
    esum = jnp.sum(jnp.exp(el - m1), axis=-1, keepdims=True)
    p1 = 1.0 / esum
    p2 = jnp.exp(m2 - m1) / esum
    tot = p1 + p2
    c1 = g_w * (p1 / tot)
    c2 = g_w * (p2 / tot)
    ef = jnp.where(lane == 0.0, i1 - ROUTE_E0, jnp.where(lane == 1.0, i2 - ROUTE_E0, 0.0))
    cw = jnp.where(lane == 0.0, c1, jnp.where(lane == 1.0, c2, 0.0))
    return ef, cw


META_TB = 512
BLK_ROWS = 256


def _meta_kernel(ef_ref, dest_ref, blk_ref, pref_ref):
    t = ef_ref.shape[0]
    nb = t // META_TB
    lane = lax.broadcasted_iota(I32, (META_TB, LANES), 1).astype(F32)
    r_i = lax.broadcasted_iota(I32, (META_TB, META_TB), 0)
    c_i = lax.broadcasted_iota(I32, (META_TB, META_TB), 1)
    tril = jnp.where(r_i > c_i, 1.0, 0.0).astype(BF16)

    def onehots(i):
        ef = ef_ref[pl.ds(i * META_TB, META_TB), :]
        oh1 = jnp.where(lane == ef[:, 0:1], 1.0, 0.0)
        oh2 = jnp.where(lane == ef[:, 1:2], 1.0, 0.0)
        return oh1, oh2

    def pass1(i, carry):
        oh1, oh2 = onehots(i)
        cnt = oh1 + oh2
        pref = jnp.dot(tril, cnt.astype(BF16), preferred_element_type=F32) + carry
        pref_ref[pl.ds(i * META_TB, META_TB), :] = pref
        return carry + jnp.sum(cnt, axis=0, keepdims=True)

    counts = lax.fori_loop(0, nb, pass1, jnp.zeros((1, LANES), F32))
    nblk = jnp.floor((counts + (EXPERT_BLOCK - 1)) * (1.0 / EXPERT_BLOCK))
    u_r = lax.broadcasted_iota(I32, (LANES, LANES), 0)
    u_c = lax.broadcasted_iota(I32, (LANES, LANES), 1)
    upper = jnp.where(u_r <= u_c, 1.0, 0.0).astype(BF16)
    nblk8 = jnp.broadcast_to(nblk, (8, LANES))
    pend_blk = jnp.dot(nblk8.astype(BF16), upper, preferred_element_type=F32)[0:1, :]
    pstart_blk = pend_blk - nblk
    pstart = pstart_blk * float(EXPERT_BLOCK)

    def pass2(i, _):
        oh1, oh2 = onehots(i)
        base = pref_ref[pl.ds(i * META_TB, META_TB), :] + pstart
        d1 = jnp.sum(oh1 * base, axis=-1, keepdims=True)
        d2 = jnp.sum(oh2 * base, axis=-1, keepdims=True)
        dest = jnp.where(lane == 0.0, d1, jnp.where(lane == 1.0, d2, 0.0))
        dest_ref[pl.ds(i * META_TB, META_TB), :] = dest.astype(I32)
        return 0

    lax.fori_loop(0, nb, pass2, 0)

    blane = lax.broadcasted_iota(I32, (BLK_ROWS, LANES), 1).astype(F32)
    brow = lax.broadcasted_iota(I32, (BLK_ROWS, LANES), 0).astype(F32)
    emask = blane < N_EXPERTS
    owner = jnp.sum(jnp.where(emask & (pend_blk <= brow), 1.0, 0.0), axis=-1, keepdims=True)
    owner = jnp.minimum(owner, float(N_EXPERTS - 1))
    total = jnp.sum(jnp.where(emask, nblk, 0.0), axis=-1, keepdims=True)
    used = jnp.where(brow < total, 1.0, 0.0)
    mine = emask & (blane == owner)
    cnt_b = jnp.sum(jnp.where(mine, counts, 0.0), axis=-1, keepdims=True)
    ps_b = jnp.sum(jnp.where(mine, pstart_blk, 0.0), axis=-1, keepdims=True)
    nvalid = jnp.clip(cnt_b - (brow - ps_b) * float(EXPERT_BLOCK), 0.0, float(EXPERT_BLOCK))
    pad_lo = nvalid * used
    blk_ref[...] = jnp.where(blane == 0.0, owner,
                             jnp.where(blane == 1.0, used,
                                       jnp.where(blane == 2.0, pad_lo, 0.0))).astype(I32)


def _meta(ef):
    t = ef.shape[0]
    return pl.pallas_call(
        _meta_kernel,
        out_shape=[jax.ShapeDtypeStruct((t, LANES), I32),
                   jax.ShapeDtypeStruct((BLK_ROWS, LANES), I32)],
        scratch_shapes=[pltpu.VMEM((t, LANES), F32)],
        compiler_params=pltpu.CompilerParams(vmem_limit_bytes=VMEM_LIMIT),
        name="meta",
    )(ef)


DISP_TM = 256
TOP_K = 2


def _dispatch_kernel(pad_lo_ref, dest_ref, h_ref, xbuf_ref, stage, zrow, sems, zsem):
    i = pl.program_id(0)
    n = pl.num_programs(0)
    tm = h_ref.shape[0]
    n_blk = pad_lo_ref.shape[0]
    slot = i % 2

    def pad_rows(action):
        def blk(b, _):
            lo = pad_lo_ref[b]
            base = pl.multiple_of(b * EXPERT_BLOCK, EXPERT_BLOCK)

            @pl.when(lo == 0)
            def _():
                whole = pl.ds(pl.multiple_of(base * ROW_SUB, EXPERT_BLOCK * ROW_SUB),
                              EXPERT_BLOCK * ROW_SUB)
                action(pltpu.make_async_copy(zrow, xbuf_ref.at[whole], zsem))

            @pl.when(lo > 0)
            def _():
                def row(r, _):
                    action(pltpu.make_async_copy(zrow.at[_row_slab(0)],
                                                 xbuf_ref.at[_row_slab(base + r)], zsem))
                    return 0
                lax.fori_loop(lo, EXPERT_BLOCK, row, 0)
            return 0
        lax.fori_loop(0, n_blk, blk, 0)

    @pl.when(i == 0)
    def _():
        zrow[...] = jnp.zeros_like(zrow)
        pad_rows(lambda cp: cp.start())

    _store_rows(stage.at[slot], 0, h_ref[...])
    for j in range(tm):
        for k in range(TOP_K):
            d = dest_ref[0, 0, TOP_K * j + k]
            pltpu.make_async_copy(stage.at[slot, _row_slab(j)], xbuf_ref.at[_row_slab(d)],
                                  sems.at[slot]).start(priority=k)

    def wait_tile(s):
        for _ in range(TOP_K):
            pltpu.make_async_copy(stage.at[s], xbuf_ref.at[pl.ds(0, tm * ROW_SUB)], sems.at[s]).wait()

    @pl.when(i > 0)
    def _():
        wait_tile(1 - slot)

    @pl.when(i == n - 1)
    def _():
        wait_tile(slot)
        pad_rows(lambda cp: cp.wait())


def _dispatch(pad_lo, dest3, hp, n_rows):
    t, w = hp.shape
    tm = DISP_TM
    grid_spec = pltpu.PrefetchScalarGridSpec(
        num_scalar_prefetch=1,
        grid=(t // tm,),
        in_specs=[pl.BlockSpec((1, 1, TOP_K * tm), lambda i, *_: (i, 0, 0), memory_space=pltpu.SMEM),
                  pl.BlockSpec((tm, w), lambda i, *_: (i, 0))],
        out_specs=pl.BlockSpec(memory_space=pl.ANY),
        scratch_shapes=[pltpu.VMEM((2, tm * ROW_SUB, LANES), U32),
                        pltpu.VMEM((EXPERT_BLOCK * ROW_SUB, LANES), U32),
                        pltpu.SemaphoreType.DMA((2,)), pltpu.SemaphoreType.DMA(())],
    )
    return pl.pallas_call(
        _dispatch_kernel,
        grid_spec=grid_spec,
        out_shape=jax.ShapeDtypeStruct((n_rows * ROW_SUB, LANES), U32),
        compiler_params=_cparams(("arbitrary",)),
        name="dispatch",
    )(pad_lo, dest3, hp)


MOE_SUB = 4
W_RING = 2
CAST_ROWS = 256
SWITCH_AFTER = 4


def _moe_kernel(run_e_ref, run_id_ref, used_ref, nruns_ref, x_ref, wg_hbm, wu_hbm, wd_hbm, o_ref,
                wg_st, wu_st, wd_st, wg_bf, wu_bf, wd_bf, sems, state):
    step = pl.program_id(0)
    n_runs = nruns_ref[0]
    streams = ((wg_hbm, wg_st, wg_bf), (wu_hbm, wu_st, wu_bf), (wd_hbm, wd_st, wd_bf))

    def weight_copy(m, run):
        hbm, stage, _ = streams[m]
        return pltpu.make_async_copy(hbm.at[run_e_ref[run]], stage, sems.at[m])

    @pl.when(step == 0)
    def _():
        state[0] = 0
        state[1] = 0
        for m in range(len(streams)):
            weight_copy(m, 0).start(priority=1)

    def switch_to(run, b):
        ring = lax.rem(run, W_RING)
        for m, (_, stage, bf) in enumerate(streams):
            weight_copy(m, run).wait()

            def body(i, _, stage=stage, bf=bf):
                rows = pl.ds(pl.multiple_of(i * CAST_ROWS, CAST_ROWS), CAST_ROWS)
                bf[ring, rows, :] = stage[rows, :].astype(BF16)
                return 0
            lax.fori_loop(0, stage.shape[0] // CAST_ROWS, body, 0)

            @pl.when(run + 1 < n_runs)
            def _(m=m):
                weight_copy(m, run + 1).start(priority=1)
        state[0] = run + 1
        state[1] = b + SWITCH_AFTER

    def run_block(b, row0):
        run = run_id_ref[b]
        pl.when(state[0] == run)(functools.partial(switch_to, run, b))
        pl.when((state[0] == run + 1) & (run + 1 < n_runs) & (b >= state[1]))(
            functools.partial(switch_to, run + 1, b))

        ring = lax.rem(run, W_RING)
        xlo, xhi = _unpack_pair(_load_rows(x_ref, row0, EXPERT_BLOCK))
        xb = jnp.concatenate([xlo.astype(BF16), xhi.astype(BF16)], axis=1)
        g = jnp.dot(xb, wg_bf[ring], preferred_element_type=F32)
        u = jnp.dot(xb, wu_bf[ring], preferred_element_type=F32)
        hmid = (_silu(g) * u).astype(BF16)
        y = jnp.dot(hmid, wd_bf[ring], preferred_element_type=F32)
        _store_rows(o_ref, row0, _pack_pair(y[:, :HALF], y[:, HALF:]))

    for j in range(MOE_SUB):
        b = step * MOE_SUB + j
        row0 = j * EXPERT_BLOCK
        pl.when(used_ref[b] == 1)(functools.partial(run_block, b, row0))

        @pl.when(used_ref[b] == 0)
        def _():
            o_ref[pl.ds(row0 * ROW_SUB, EXPERT_BLOCK * ROW_SUB), :] = jnp.zeros(
                (EXPERT_BLOCK * ROW_SUB, LANES), U32)


def _moe(run_e, run_id, used, n_runs, xbuf, w_gate, w_up, w_down):
    r = xbuf.shape[0]
    rows = MOE_SUB * EXPERT_BLOCK * ROW_SUB
    d, de = w_gate.shape[1:]

    def xmap(s, run_e_ref, run_id_ref, used_ref, nruns_ref):
        del run_e_ref, run_id_ref, nruns_ref
        return (jnp.where(used_ref[s * MOE_SUB] == 1, s, 0), 0)

    grid_spec = pltpu.PrefetchScalarGridSpec(
        num_scalar_prefetch=4,
        grid=(r // rows,),
        in_specs=[pl.BlockSpec((rows, LANES), xmap),
                  pl.BlockSpec(memory_space=pl.ANY),
                  pl.BlockSpec(memory_space=pl.ANY),
                  pl.BlockSpec(memory_space=pl.ANY)],
        out_specs=pl.BlockSpec((rows, LANES), lambda s, *_: (s, 0)),
        scratch_shapes=[pltpu.VMEM((d, de), F32), pltpu.VMEM((d, de), F32), pltpu.VMEM((de, d), F32),
                        pltpu.VMEM((W_RING, d, de), BF16), pltpu.VMEM((W_RING, d, de), BF16),
                        pltpu.VMEM((W_RING, de, d), BF16),
                        pltpu.SemaphoreType.DMA((3,)), pltpu.SMEM((2,), I32)],
    )
    return pl.pallas_call(
        _moe_kernel,
        grid_spec=grid_spec,
        out_shape=jax.ShapeDtypeStruct((r, LANES), U32),
        compiler_params=_cparams(("arbitrary",), vmem=VMEM_LIMIT_MOE),
        name="moe",
    )(run_e, run_id, used, n_runs, xbuf, w_gate, w_up, w_down)


FIN_TM = 256


def _final_kernel(dest_ref, dnext_ref, x_ref, cw_ref, mod_ref, g_ref, y_hbm, o_ref, ybuf, sems):
    i = pl.program_id(0)
    n = pl.num_programs(0)
    tm = x_ref.shape[0]
    slot = i % 2

    def gather(d_ref, s):
        for j in range(tm):
            for k in range(TOP_K):
                d = d_ref[0, 0, TOP_K * j + k]
                pltpu.make_async_copy(y_hbm.at[_row_slab(d)], ybuf.at[s, k, _row_slab(j)],
                                      sems.at[s]).start(priority=k)

    @pl.when(i == 0)
    def _():
        gather(dest_ref, 0)

    @pl.when(i + 1 < n)
    def _():
        gather(dnext_ref, 1 - slot)

    for k in range(TOP_K):
        pltpu.make_async_copy(y_hbm.at[pl.ds(0, tm * ROW_SUB)], ybuf.at[slot, k], sems.at[slot]).wait()

    cw = cw_ref[...]
    c0, c1 = cw[:, 0:1], cw[:, 1:2]
    lo0, hi0 = _unpack_pair(_load_rows(ybuf.at[slot, 0], 0, tm))
    lo1, hi1 = _unpack_pair(_load_rows(ybuf.at[slot, 1], 0, tm))
    g2 = mod_ref[0, 5:6, :]
    xlo = x_ref[:, :HALF] + g2[:, :HALF] * (lo0 * c0 + lo1 * c1)
    xhi = x_ref[:, HALF:] + g2[:, HALF:] * (hi0 * c0 + hi1 * c1)
    ms = (jnp.sum(xlo * xlo, axis=-1, keepdims=True)
          + jnp.sum(xhi * xhi, axis=-1, keepdims=True)) * (1.0 / D_MODEL)
    r = lax.rsqrt(ms + EPS)
    o_ref[:, :HALF] = xlo * r * g_ref[:, :HALF]
    o_ref[:, HALF:] = xhi * r * g_ref[:, HALF:]


def _final(dest3, x1, cw, mod3, g, ybuf, seq):
    t, d = x1.shape
    tm = FIN_TM
    nt = t // tm
    per_b = seq // tm

    def dspec(shift):
        return pl.BlockSpec((1, 1, TOP_K * tm), lambda i: (jnp.minimum(i + shift, nt - 1), 0, 0),
                            memory_space=pltpu.SMEM)

    return pl.pallas_call(
        _final_kernel,
        grid=(nt,),
        in_specs=[dspec(0), dspec(1),
                  pl.BlockSpec((tm, d), lambda i: (i, 0)),
                  pl.BlockSpec((tm, LANES), lambda i: (i, 0)),
                  pl.BlockSpec((1, 6, d), lambda i: (i // per_b, 0, 0)),
                  pl.BlockSpec((1, d), lambda i: (0, 0)),
                  pl.BlockSpec(memory_space=pl.ANY)],
        out_specs=pl.BlockSpec((tm, d), lambda i: (i, 0)),
        out_shape=jax.ShapeDtypeStruct((t, d), F32),
        scratch_shapes=[pltpu.VMEM((2, TOP_K, tm * ROW_SUB, LANES), U32),
                        pltpu.SemaphoreType.DMA((2,))],
        compiler_params=_cparams(("arbitrary",)),
        name="final",
    )(dest3, dest3, x1, cw, mod3, g, ybuf)


def kernel(x, c, w_ada, b_ada, norm1_g, w_in, conv_w, w_conv_out, w_ret_out, w_o, norm2_g,
           w_router_group, b_router_group, w_router_expert, b_router_expert,
           w_gate, w_up, w_down, norm_f_g):
    batch, seq, d = x.shape
    t = batch * seq
    depth = w_ada.shape[0]
    assert d == D_MODEL and w_in.shape[2] == IN_COLS and w_gate.shape[1] == N_EXPERTS
    assert depth == 1, "the final kernel fuses the last rmsnorm into the single layer"
    n_rows = t * 2 + N_EXPERTS * EXPERT_BLOCK
    n_blk = n_rows // EXPERT_BLOCK
    assert n_blk <= BLK_ROWS

    c_pad = jnp.pad(c, ((0, 8 - batch), (0, 0)))
    xs = x.reshape(t, d)
    for l in range(depth):
        mod = _ada(c_pad, w_ada[l], b_ada[l].reshape(1, -1))[:batch]
        mod3 = mod.reshape(batch, 6, d)

        u = _in_proj(xs, mod3, norm1_g[l].reshape(1, d), w_in[l], seq)
        z, wc_bf, wr_bf, wo_bf = _retention(u, w_conv_out[l], w_ret_out[l], w_o[l], batch, seq)
        w_r = jnp.pad(jnp.concatenate([w_router_group[l], w_router_expert[l]], axis=1),
                      ((0, 0), (0, LANES - N_GROUPS - N_EXPERTS)))
        b_r = jnp.pad(jnp.concatenate([b_router_group[l], b_router_expert[l]]),
                      (0, LANES - N_GROUPS - N_EXPERTS)).reshape(1, LANES)
        x1, h2, ef, cw = _mix(u, z, conv_w[l], wc_bf, wr_bf, wo_bf, xs, mod3,
                              norm2_g[l].reshape(1, d), w_r, b_r, seq)

        dest, blk = _meta(ef)
        blk_e = blk[:n_blk, 0]
        used = blk[:n_blk, 1]
        prev_e = jnp.concatenate([jnp.full((1,), -1, I32), blk_e[:-1]])
        first = ((blk_e != prev_e) & (used == 1)).astype(I32)
        run_id = jnp.maximum(jnp.cumsum(first) - 1, 0).astype(I32)
        n_runs = jnp.sum(first).astype(I32).reshape(1)
        present = jnp.any((blk_e[None, :] == jnp.arange(N_EXPERTS, dtype=I32)[:, None])
                          & (used[None, :] == 1), axis=1)
        eids = jnp.arange(N_EXPERTS, dtype=I32)
        rank = jnp.cumsum(present.astype(I32)) - 1
        run_e = jnp.sum(jnp.where(present[None, :] & (rank[None, :] == eids[:, None]),
                                  eids[None, :], 0), axis=1).astype(I32)

        dest2 = dest[:, :TOP_K]
        xbuf = _dispatch(blk[:n_blk, 2], dest2.reshape(t // DISP_TM, 1, TOP_K * DISP_TM), h2, n_rows)
        ybuf = _moe(run_e, run_id, used, n_runs, xbuf, w_gate[l], w_up[l], w_down[l])
        xs = _final(dest2.reshape(t // FIN_TM, 1, TOP_K * FIN_TM), x1, cw, mod3,
                    norm_f_g.reshape(1, d), ybuf, seq)
    return xs.reshape(batch, seq, d)
```

```python
import functools

import numpy as np
import jax
import jax.numpy as jnp
from jax import lax
from jax.experimental import pallas as pl
from jax.experimental.pallas import tpu as pltpu

F32 = jnp.float32
BF16 = jnp.bfloat16
I32 = jnp.int32

D_MODEL = 2048
CONV_CH = 1024
CONV_K = 3
RET_HEADS = 8
RET_DK = 128
RET_DV = 256
CHUNK = 128
ROPE_BASE = 10000.0
N_GROUPS = 4
EXPERTS_PER_GROUP = 8
N_EXPERTS = 32
EXPERT_BLOCK = 128
EPS = 1e-6
IN_COLS = 13312
LANES = 128
COL_BLK = 1024
CB_BLK, CC_BLK, CX_BLK, Q_BLK, K_BLK, V_BLK, RG_BLK, GA_BLK, GB_BLK = 0, 1, 2, 3, 4, 5, 7, 9, 11

V7X_VMEM_BYTES = 64 * 1024 * 1024
VMEM_LIMIT = V7X_VMEM_BYTES - 8 * 1024 * 1024
VMEM_LIMIT_MOE = V7X_VMEM_BYTES - 4 * 1024 * 1024


def _cparams(sem, vmem=VMEM_LIMIT):
    return pltpu.CompilerParams(dimension_semantics=sem, vmem_limit_bytes=vmem)


def _sigmoid(v):
    return 1.0 / (1.0 + jnp.exp(-v))


def _silu(v):
    return v * _sigmoid(v)


def _split_bf16(v):
    hi = v.astype(BF16)
    lo = (v - hi.astype(F32)).astype(BF16)
    return hi, lo


def _ada_kernel(c_ref, w_ref, b_ref, o_ref):
    rows = c_ref.shape[0]
    ch, cl = _split_bf16(_silu(c_ref[...]))
    wh, wl = _split_bf16(w_ref[...])
    p = jnp.dot(jnp.concatenate([ch, cl], axis=0), wh, preferred_element_type=F32)
    o_ref[...] = (p[:rows] + (p[rows:] + jnp.dot(ch, wl, preferred_element_type=F32))) + b_ref[...]


def _ada(c_pad, w, b):
    rows, d = c_pad.shape
    n = w.shape[1]
    tn = 1024
    return pl.pallas_call(
        _ada_kernel,
        grid=(n // tn,),
        in_specs=[pl.BlockSpec((rows, d), lambda j: (0, 0)),
                  pl.BlockSpec((d, tn), lambda j: (0, j)),
                  pl.BlockSpec((1, tn), lambda j: (0, j))],
        out_specs=pl.BlockSpec((rows, tn), lambda j: (0, j)),
        out_shape=jax.ShapeDtypeStruct((rows, n), F32),
        compiler_params=_cparams(("arbitrary",)),
        name="ada",
    )(c_pad, w, b)


NORM_ROWS = 64


IN_SPLIT = 2


def _in_kernel(x_ref, mod_ref, g_ref, w_ref, o_ref, h_ref):
    i, j = pl.program_id(0), pl.program_id(1)
    nj = pl.num_programs(1)
    tm, tn = o_ref.shape
    slot = i % 2

    def norm_rows(dst, rows):
        gain = g_ref[...] * (1.0 + mod_ref[0, 1:2, :])
        x = x_ref[rows, :]
        r = lax.rsqrt(jnp.mean(x * x, axis=-1, keepdims=True) + EPS)
        h_ref[dst, rows, :] = (x * r * gain + mod_ref[0, 0:1, :]).astype(BF16)

    @pl.when((i == 0) & (j == 0))
    def _():
        def body(k, _):
            norm_rows(0, pl.ds(pl.multiple_of(k * NORM_ROWS, NORM_ROWS), NORM_ROWS))
            return 0
        lax.fori_loop(0, tm // NORM_ROWS, body, 0)

    @pl.when(j < nj - 1)
    def _():
        o_ref[...] = jnp.dot(h_ref[slot], w_ref[...].astype(BF16),
                             preferred_element_type=F32).astype(o_ref.dtype)

    @pl.when(j == nj - 1)
    def _():
        h = h_ref[slot]
        cw, rw = tn // IN_SPLIT, tm // IN_SPLIT
        for q in range(IN_SPLIT):
            cols = slice(q * cw, (q + 1) * cw)
            o_ref[:, cols] = jnp.dot(h, w_ref[:, cols].astype(BF16),
                                     preferred_element_type=F32).astype(o_ref.dtype)
            for k in range(rw // NORM_ROWS):
                norm_rows(1 - slot, slice(q * rw + k * NORM_ROWS, q * rw + (k + 1) * NORM_ROWS))


def _in_proj(x2, mod3, g, w, seq):
    t, d = x2.shape
    n = w.shape[1]
    tm, tn = 1024, 1024
    nt, nj = t // tm, n // tn
    per_b = seq // tm

    def xtile(i, j):
        return jnp.minimum(i + (j == nj - 1).astype(jnp.int32), nt - 1)

    return pl.pallas_call(
        _in_kernel,
        grid=(nt, nj),
        in_specs=[pl.BlockSpec((tm, d), lambda i, j: (xtile(i, j), 0)),
                  pl.BlockSpec((1, 6, d), lambda i, j: (xtile(i, j) // per_b, 0, 0)),
                  pl.BlockSpec((1, d), lambda i, j: (0, 0)),
                  pl.BlockSpec((d, tn), lambda i, j: (0, j))],
        out_specs=pl.BlockSpec((tm, tn), lambda i, j: (i, j)),
        out_shape=jax.ShapeDtypeStruct((t, n), BF16),
        scratch_shapes=[pltpu.VMEM((2, tm, d), BF16)],
        compiler_params=_cparams(("arbitrary", "arbitrary")),
        name="in_proj",
    )(x2, mod3, g, w)


RET_CPS = 4


def _ret_kernel(q_ref, k_ref, v0_ref, v1_ref, rg0_ref, rg1_ref, cos_ref, sin_ref,
                idec_ref, qdec_ref, kdec_ref, wc_ref, wr_ref, wo_ref,
                o_ref, wcb_ref, wrb_ref, wob_ref, state_ref, *, chunk_decay):
    @pl.when(pl.program_id(1) == 0)
    def _():
        state_ref[...] = jnp.zeros_like(state_ref)

    wcb_ref[...] = wc_ref[...].astype(BF16)
    wrb_ref[...] = wr_ref[...].astype(BF16)
    wob_ref[...] = wo_ref[...].astype(BF16)

    def rot(t, cos, sin):
        return t * cos + pltpu.roll(t, RET_DK // 2, axis=1) * sin

    for ci in range(RET_CPS):
        rows = slice(ci * CHUNK, (ci + 1) * CHUNK)
        cos = cos_ref[rows, :]
        sin = sin_ref[rows, :]
        for h in range(RET_HEADS):
            v_ref, rg_ref = (v0_ref, rg0_ref) if h < RET_HEADS // 2 else (v1_ref, rg1_ref)
            vc = (h % (RET_HEADS // 2)) * RET_DV
            qh = rot(q_ref[rows, h * RET_DK:(h + 1) * RET_DK].astype(F32), cos, sin)
            kh = rot(k_ref[rows, h * RET_DK:(h + 1) * RET_DK].astype(F32), cos, sin) * (RET_DK ** -0.5)
            vh = v_ref[rows, vc:vc + RET_DV]
            qb = qh.astype(BF16)
            kb = kh.astype(BF16)
            scores = lax.dot_general(qb, kb, (((1,), (1,)), ((), ())),
                                     preferred_element_type=F32) * idec_ref[h]
            inner = jnp.dot(scores.astype(BF16), vh, preferred_element_type=F32)
            state = state_ref[h]
            cross = jnp.dot((qh * qdec_ref[h]).astype(BF16), state.astype(BF16),
                            preferred_element_type=F32)
            kv = lax.dot_general((kh * kdec_ref[h]).astype(BF16), vh, (((0,), (0,)), ((), ())),
                                 preferred_element_type=F32)
            state_ref[h] = state * chunk_decay[h] + kv
            r = inner + cross
            r = r * lax.rsqrt(jnp.mean(r * r, axis=-1, keepdims=True) + EPS)
            gate = _silu(rg_ref[rows, vc:vc + RET_DV].astype(F32))
            o_ref[rows, h * RET_DV:(h + 1) * RET_DV] = (gate * r).astype(o_ref.dtype)


def _retention(u, w_conv_out, w_ret_out, w_o, batch, seq):
    t = u.shape[0]
    f32 = np.float32
    inv = (f32(ROPE_BASE) ** (-np.arange(0, RET_DK, 2, dtype=f32) / f32(RET_DK))).astype(f32)
    ang = np.arange(seq, dtype=f32)[:, None] * inv[None, :]
    cos, sin = np.cos(ang).astype(f32), np.sin(ang).astype(f32)
    cos_t = np.concatenate([cos, cos], axis=-1)
    sin_t = np.concatenate([-sin, sin], axis=-1)
    log_g = np.log1p(-(f32(2.0) ** (f32(-5.0) - np.arange(RET_HEADS, dtype=f32)))).astype(f32)
    i = np.arange(CHUNK, dtype=f32)
    diff = i[:, None] - i[None, :]
    idec = np.where(diff >= 0, np.exp(log_g[:, None, None] * np.maximum(diff, f32(0.0))), f32(0.0)).astype(f32)
    qdec = np.broadcast_to(np.exp(log_g[:, None] * (i + f32(1.0))).astype(f32)[:, :, None],
                           (RET_HEADS, CHUNK, RET_DK))
    kdec = np.broadcast_to(np.exp(log_g[:, None] * (f32(CHUNK - 1.0) - i)).astype(f32)[:, :, None],
                           (RET_HEADS, CHUNK, RET_DK))
    chunk_decay = tuple(float(np.exp(lg * f32(CHUNK))) for lg in log_g)

    rows = RET_CPS * CHUNK
    nst = seq // rows

    def ublk(c):
        return pl.BlockSpec((rows, COL_BLK), lambda b, n, c=c: (b * nst + n, c))

    def table():
        return pl.BlockSpec((RET_HEADS, CHUNK, RET_DK), lambda b, n: (0, 0, 0))

    def slab(w):
        n_steps = batch * nst
        assert w.shape[0] % (8 * n_steps) == 0
        return pl.BlockSpec((w.shape[0] // n_steps, w.shape[1]), lambda b, n: (b * nst + n, 0))

    weights = (w_conv_out, w_ret_out, w_o)
    return pl.pallas_call(
        functools.partial(_ret_kernel, chunk_decay=chunk_decay),
        grid=(batch, nst),
        in_specs=[ublk(Q_BLK), ublk(K_BLK), ublk(V_BLK), ublk(V_BLK + 1),
                  ublk(RG_BLK), ublk(RG_BLK + 1),
                  pl.BlockSpec((rows, RET_DK), lambda b, n: (n, 0)),
                  pl.BlockSpec((rows, RET_DK), lambda b, n: (n, 0)),
                  table(), table(), table()] + [slab(w) for w in weights],
        out_specs=[pl.BlockSpec((rows, RET_HEADS * RET_DV), lambda b, n: (b * nst + n, 0))]
        + [slab(w) for w in weights],
        out_shape=[jax.ShapeDtypeStruct((t, RET_HEADS * RET_DV), BF16)]
        + [jax.ShapeDtypeStruct(w.shape, BF16) for w in weights],
        scratch_shapes=[pltpu.VMEM((RET_HEADS, RET_DK, RET_DV), F32)],
        compiler_params=_cparams(("arbitrary", "arbitrary")),
        name="retention",
    )(u, u, u, u, u, u, *(jnp.asarray(tab) for tab in (cos_t, sin_t, idec, qdec, kdec)), *weights)


HALO = 8


MIX_TN = 512


def _mix_kernel(cb_ref, cc_ref, cx_ref, hcc_ref, hcx_ref, z_ref, ga0_ref, ga1_ref, gb0_ref, gb1_ref,
                cw_ref, wc_ref, wr_ref, wo_ref, x_ref, mod_ref, modp_ref, g2_ref, wrt_ref, brt_ref,
                o_ref, hp_ref, ef_ref, rw_ref, m_ref, mg_ref, x1p_ref, *, tiles_per_seq, n_tiles):
    tm = cb_ref.shape[0]
    d = o_ref.shape[1]
    step = pl.program_id(0)

    slot = step % 2

    @pl.when(step == 0)
    def _():
        x1p_ref[1] = jnp.zeros(x1p_ref.shape[1:], F32)

    m = cc_ref[...].astype(F32) * cx_ref[...].astype(F32)
    halo = hcc_ref[...].astype(F32) * hcx_ref[...].astype(F32)
    first = (jnp.minimum(step, n_tiles - 1) % tiles_per_seq) == 0
    m_ref[0:HALO, :] = jnp.where(first, 0.0, halo)
    m_ref[HALO:HALO + tm, :] = m
    conv = (m_ref[HALO - 2:HALO - 2 + tm, :] * cw_ref[0:1, :]
            + m_ref[HALO - 1:HALO - 1 + tm, :] * cw_ref[1:2, :]
            + m * cw_ref[2:3, :])
    p = (cb_ref[...].astype(F32) * conv).astype(BF16)
    z = z_ref[...]
    nchunk = d // MIX_TN
    prev = 1 - slot
    ssq = jnp.zeros((tm, 1), F32)
    for n in range(nchunk):
        cols = slice(n * MIX_TN, (n + 1) * MIX_TN)
        ga_ref, gb_ref = (ga0_ref, gb0_ref) if n < COL_BLK // MIX_TN else (ga1_ref, gb1_ref)
        gcols = slice((n * MIX_TN) % COL_BLK, (n * MIX_TN) % COL_BLK + MIX_TN)
        ya = jnp.dot(p, wc_ref[:, cols], preferred_element_type=F32)
        yb = jnp.dot(z, wr_ref[:, cols], preferred_element_type=F32)
        merged = (_sigmoid(ga_ref[:, gcols].astype(F32)) * ya
                  + _sigmoid(gb_ref[:, gcols].astype(F32)) * yb)
        mg_ref[:, cols] = merged.astype(BF16)
        xp = x1p_ref[prev, :, cols]
        ssq = ssq + jnp.sum(xp * xp, axis=-1, keepdims=True)
    rnorm = lax.rsqrt(ssq * (1.0 / d) + EPS)
    gain = g2_ref[...] * (1.0 + modp_ref[0, 4:5, :])
    shift = modp_ref[0, 3:4, :]
    wh, wl = _split_bf16(wrt_ref[...])
    whl = jnp.concatenate([wh, wl], axis=1)
    acc_hw = jnp.zeros((tm, 2 * LANES), F32)
    acc_lw = jnp.zeros((tm, LANES), F32)
    mg = mg_ref[...]
    for n in range(nchunk):
        cols = slice(n * MIX_TN, (n + 1) * MIX_TN)
        acc = jnp.dot(mg, wo_ref[:, cols], preferred_element_type=F32)
        x1 = x_ref[:, cols] + mod_ref[0, 2:3, cols] * acc
        o_ref[:, cols] = x1
        x1p_ref[slot, :, cols] = x1
        if n % 2 == 0:
            c = (n // 2) * MIX_TN
            halves = []
            for c0 in (c, c + HALF):
                hcols = slice(c0, c0 + MIX_TN)
                h2 = x1p_ref[prev, :, hcols] * rnorm * gain[:, hcols] + shift[:, hcols]
                hh, hl = _split_bf16(h2)
                acc_hw = acc_hw + jnp.dot(hh, whl[hcols, :], preferred_element_type=F32)
                acc_lw = acc_lw + jnp.dot(hl, wh[hcols, :], preferred_element_type=F32)
                halves.append(h2)
            hp_ref[:, c:c + MIX_TN] = _pack_pair(halves[0], halves[1])
    logits = (acc_hw[:, :LANES] + (acc_hw[:, LANES:] + acc_lw)) + brt_ref[...]
    ef_ref[...], rw_ref[...] = _route_from_logits(logits)


def _mix(u, z, conv_w, wc_bf, wr_bf, wo_bf, x2, mod3, g2, w_rt, b_rt, seq):
    t, d = x2.shape
    tm = 256
    nt = t // tm
    hb = tm // HALO
    per_b = seq // tm

    def cur(i):
        return jnp.minimum(i, nt - 1)

    def prev(i):
        return jnp.maximum(i - 1, 0)

    def ublk(c):
        return pl.BlockSpec((tm, COL_BLK), lambda i, c=c: (cur(i), c))

    def halo(c):
        return pl.BlockSpec((HALO, COL_BLK), lambda i, c=c: (jnp.maximum(cur(i) * hb - 1, 0), c))

    def resident(shape):
        return pl.BlockSpec(shape, lambda i: (0, 0), pipeline_mode=pl.Buffered(1))

    return pl.pallas_call(
        functools.partial(_mix_kernel, tiles_per_seq=per_b, n_tiles=nt),
        grid=(nt + 1,),
        in_specs=[ublk(CB_BLK), ublk(CC_BLK), ublk(CX_BLK), halo(CC_BLK), halo(CX_BLK),
                  pl.BlockSpec((tm, z.shape[1]), lambda i: (cur(i), 0)),
                  ublk(GA_BLK), ublk(GA_BLK + 1), ublk(GB_BLK), ublk(GB_BLK + 1),
                  pl.BlockSpec((CONV_K, CONV_CH), lambda i: (0, 0)),
                  resident(wc_bf.shape), resident(wr_bf.shape), resident(wo_bf.shape),
                  pl.BlockSpec((tm, d), lambda i: (cur(i), 0)),
                  pl.BlockSpec((1, 6, d), lambda i: (cur(i) // per_b, 0, 0)),
                  pl.BlockSpec((1, 6, d), lambda i: (prev(i) // per_b, 0, 0)),
                  pl.BlockSpec((1, d), lambda i: (0, 0)),
                  pl.BlockSpec((d, LANES), lambda i: (0, 0)),
                  pl.BlockSpec((1, LANES), lambda i: (0, 0))],
        out_specs=[pl.BlockSpec((tm, d), lambda i: (cur(i), 0)),
                   pl.BlockSpec((tm, HALF), lambda i: (prev(i), 0)),
                   pl.BlockSpec((tm, LANES), lambda i: (prev(i), 0)),
                   pl.BlockSpec((tm, LANES), lambda i: (prev(i), 0))],
        out_shape=[jax.ShapeDtypeStruct((t, d), F32),
                   jax.ShapeDtypeStruct((t, HALF), U32),
                   jax.ShapeDtypeStruct((t, LANES), F32),
                   jax.ShapeDtypeStruct((t, LANES), F32)],
        scratch_shapes=[pltpu.VMEM((tm + HALO, CONV_CH), F32),
                        pltpu.VMEM((tm, d), BF16),
                        pltpu.VMEM((2, tm, d), F32)],
        compiler_params=_cparams(("arbitrary",)),
        name="mix",
    )(u, u, u, u, u, z, u, u, u, u, conv_w, wc_bf, wr_bf, wo_bf, x2, mod3, mod3, g2, w_rt, b_rt)


ROUTE_E0 = N_GROUPS


U32 = jnp.uint32
HALF = D_MODEL // 2


def _pack_pair(lo, hi):
    ulo = lax.bitcast_convert_type(lo.astype(BF16).astype(F32), U32)
    uhi = lax.bitcast_convert_type(hi.astype(BF16).astype(F32), U32)
    return (ulo >> 16) | uhi


def _unpack_pair(w):
    lo = lax.bitcast_convert_type(w << 16, F32)
    hi = lax.bitcast_convert_type(w & jnp.uint32(0xFFFF0000), F32)
    return lo, hi


ROW_SUB = HALF // LANES


def _row_slab(r):
    if isinstance(r, int):
        return pl.ds(r * ROW_SUB, ROW_SUB)
    return pl.ds(pl.multiple_of(r * ROW_SUB, ROW_SUB), ROW_SUB)


def _store_rows(ref, row0, v):
    for c in range(ROW_SUB):
        ref[pl.ds(row0 * ROW_SUB + c, v.shape[0], stride=ROW_SUB), :] = v[:, c * LANES:(c + 1) * LANES]


def _load_rows(ref, row0, n):
    return jnp.concatenate(
        [ref[pl.ds(row0 * ROW_SUB + c, n, stride=ROW_SUB), :] for c in range(ROW_SUB)], axis=1)


def _route_from_logits(logits):
    lane = lax.broadcasted_iota(I32, logits.shape, 1).astype(F32)
    neg = -jnp.inf
    big = float(LANES)
    gl = jnp.where(lane < N_GROUPS, logits, neg)
    gmax = jnp.max(gl, axis=-1, keepdims=True)
    gsel = jnp.min(jnp.where(gl == gmax, lane, big), axis=-1, keepdims=True)
    gsum = jnp.sum(jnp.exp(gl - gmax), axis=-1, keepdims=True)
    g_w = 1.0 / gsum
    lo_lane = ROUTE_E0 + EXPERTS_PER_GROUP * gsel
    emask = (lane >= lo_lane) & (lane < lo_lane + EXPERTS_PER_GROUP)
    el = jnp.where(emask, logits, neg)
    m1 = jnp.max(el, axis=-1, keepdims=True)
    i1 = jnp.min(jnp.where(el == m1, lane, big), axis=-1, keepdims=True)
    el2 = jnp.where(lane == i1, neg, el)
    m2 = jnp.max(el2, axis=-1, keepdims=True)
    i2 = jnp.min(jnp.where(el2 == m2, lane, big), axis=-1, keepdims=True)
    esum = jnp.sum(jnp.exp(el - m1), axis=-1, keepdims=True)
    p1 = 1.0 / esum
    p2 = jnp.exp(m2 - m1) / esum
    tot = p1 + p2
    c1 = g_w * (p1 / tot)
    c2 = g_w * (p2 / tot)
    ef = jnp.where(lane == 0.0, i1 - ROUTE_E0, jnp.where(lane == 1.0, i2 - ROUTE_E0, 0.0))
    cw = jnp.where(lane == 0.0, c1, jnp.where(lane == 1.0, c2, 0.0))
    return ef, cw


META_TB = 512
DEST_ROWS = 8
BLK_ROWS = 256


def _meta_kernel(ef_ref, dest_ref, blk_ref, pref_ref):
    t = ef_ref.shape[0]
    nb = t // META_TB
    lane = lax.broadcasted_iota(I32, (META_TB, LANES), 1).astype(F32)
    r_i = lax.broadcasted_iota(I32, (META_TB, META_TB), 0)
    c_i = lax.broadcasted_iota(I32, (META_TB, META_TB), 1)
    tril = jnp.where(r_i > c_i, 1.0, 0.0).astype(BF16)

    def onehots(i):
        ef = ef_ref[pl.ds(i * META_TB, META_TB), :]
        oh1 = jnp.where(lane == ef[:, 0:1], 1.0, 0.0)
        oh2 = jnp.where(lane == ef[:, 1:2], 1.0, 0.0)
        return oh1, oh2

    def pass1(i, carry):
        oh1, oh2 = onehots(i)
        cnt = oh1 + oh2
        pref = jnp.dot(tril, cnt.astype(BF16), preferred_element_type=F32) + carry
        pref_ref[pl.ds(i * META_TB, META_TB), :] = pref
        return carry + jnp.sum(cnt, axis=0, keepdims=True)

    counts = lax.fori_loop(0, nb, pass1, jnp.zeros((1, LANES), F32))
    nblk = jnp.floor((counts + (EXPERT_BLOCK - 1)) * (1.0 / EXPERT_BLOCK))
    u_r = lax.broadcasted_iota(I32, (LANES, LANES), 0)
    u_c = lax.broadcasted_iota(I32, (LANES, LANES), 1)
    upper = jnp.where(u_r <= u_c, 1.0, 0.0).astype(BF16)
    nblk8 = jnp.broadcast_to(nblk, (8, LANES))
    pend_blk = jnp.dot(nblk8.astype(BF16), upper, preferred_element_type=F32)[0:1, :]
    pstart_blk = pend_blk - nblk
    pstart = pstart_blk * float(EXPERT_BLOCK)

    for i in range(nb):
        oh1, oh2 = onehots(i)
        base = pref_ref[pl.ds(i * META_TB, META_TB), :] + pstart
        d1 = jnp.sum(oh1 * base, axis=-1, keepdims=True)
        d2 = jnp.sum(oh2 * base, axis=-1, keepdims=True)
        dest = jnp.where(lane == 0.0, d1, jnp.where(lane == 1.0, d2, 0.0))
        dest_ref[:, i * META_TB:(i + 1) * META_TB] = dest.T[:DEST_ROWS, :].astype(I32)

    blane = lax.broadcasted_iota(I32, (BLK_ROWS, LANES), 1).astype(F32)
    brow = lax.broadcasted_iota(I32, (BLK_ROWS, LANES), 0).astype(F32)
    emask = blane < N_EXPERTS
    owner = jnp.sum(jnp.where(emask & (pend_blk <= brow), 1.0, 0.0), axis=-1, keepdims=True)
    owner = jnp.minimum(owner, float(N_EXPERTS - 1))
    total = jnp.sum(jnp.where(emask, nblk, 0.0), axis=-1, keepdims=True)
    used = jnp.where(brow < total, 1.0, 0.0)
    mine = emask & (blane == owner)
    cnt_b = jnp.sum(jnp.where(mine, counts, 0.0), axis=-1, keepdims=True)
    ps_b = jnp.sum(jnp.where(mine, pstart_blk, 0.0), axis=-1, keepdims=True)
    nvalid = jnp.clip(cnt_b - (brow - ps_b) * float(EXPERT_BLOCK), 0.0, float(EXPERT_BLOCK))
    pad_lo = nvalid * used
    blk_ref[...] = jnp.where(blane == 0.0, owner,
                             jnp.where(blane == 1.0, used,
                                       jnp.where(blane == 2.0, pad_lo, 0.0))).astype(I32)


def _meta(ef):
    t = ef.shape[0]
    return pl.pallas_call(
        _meta_kernel,
        out_shape=[jax.ShapeDtypeStruct((DEST_ROWS, t), I32),
                   jax.ShapeDtypeStruct((BLK_ROWS, LANES), I32)],
        scratch_shapes=[pltpu.VMEM((t, LANES), F32)],
        compiler_params=pltpu.CompilerParams(vmem_limit_bytes=VMEM_LIMIT),
        name="meta",
    )(ef)


DISP_TM = 256
TOP_K = 2


def _dispatch_kernel(pad_lo_ref, dest_ref, h_ref, xbuf_ref, stage, zrow, sems, zsem):
    i = pl.program_id(0)
    n = pl.num_programs(0)
    tm = h_ref.shape[0]
    n_blk = pad_lo_ref.shape[0]
    slot = i % 2

    def pad_rows(action):
        def blk(b, _):
            lo = pad_lo_ref[b]
            base = pl.multiple_of(b * EXPERT_BLOCK, EXPERT_BLOCK)

            @pl.when(lo == 0)
            def _():
                whole = pl.ds(pl.multiple_of(base * ROW_SUB, EXPERT_BLOCK * ROW_SUB),
                              EXPERT_BLOCK * ROW_SUB)
                action(pltpu.make_async_copy(zrow, xbuf_ref.at[whole], zsem))

            @pl.when(lo > 0)
            def _():
                def row(r, _):
                    action(pltpu.make_async_copy(zrow.at[_row_slab(0)],
                                                 xbuf_ref.at[_row_slab(base + r)], zsem))
                    return 0
                lax.fori_loop(lo, EXPERT_BLOCK, row, 0)
            return 0
        lax.fori_loop(0, n_blk, blk, 0)

    @pl.when(i == 0)
    def _():
        zrow[...] = jnp.zeros_like(zrow)
        pad_rows(lambda cp: cp.start())

    _store_rows(stage.at[slot], 0, h_ref[...])
    for j in range(tm):
        for k in range(TOP_K):
            d = dest_ref[k, 0, 0, j]
            pltpu.make_async_copy(stage.at[slot, _row_slab(j)], xbuf_ref.at[_row_slab(d)],
                                  sems.at[slot]).start(priority=k)

    def wait_tile(s):
        for _ in range(TOP_K):
            pltpu.make_async_copy(stage.at[s], xbuf_ref.at[pl.ds(0, tm * ROW_SUB)], sems.at[s]).wait()

    @pl.when(i > 0)
    def _():
        wait_tile(1 - slot)

    @pl.when(i == n - 1)
    def _():
        wait_tile(slot)
        pad_rows(lambda cp: cp.wait())


def _dispatch(pad_lo, dest3, hp, n_rows):
    t, w = hp.shape
    tm = DISP_TM
    grid_spec = pltpu.PrefetchScalarGridSpec(
        num_scalar_prefetch=1,
        grid=(t // tm,),
        in_specs=[pl.BlockSpec((TOP_K, 1, 1, tm), lambda i, *_: (0, i, 0, 0), memory_space=pltpu.SMEM),
                  pl.BlockSpec((tm, w), lambda i, *_: (i, 0))],
        out_specs=pl.BlockSpec(memory_space=pl.ANY),
        scratch_shapes=[pltpu.VMEM((2, tm * ROW_SUB, LANES), U32),
                        pltpu.VMEM((EXPERT_BLOCK * ROW_SUB, LANES), U32),
                        pltpu.SemaphoreType.DMA((2,)), pltpu.SemaphoreType.DMA(())],
    )
    return pl.pallas_call(
        _dispatch_kernel,
        grid_spec=grid_spec,
        out_shape=jax.ShapeDtypeStruct((n_rows * ROW_SUB, LANES), U32),
        compiler_params=_cparams(("arbitrary",)),
        name="dispatch",
    )(pad_lo, dest3, hp)


MOE_SUB = 4
W_RING = 2
CAST_ROWS = 256
SWITCH_AFTER = 4


def _moe_kernel(run_e_ref, run_id_ref, used_ref, nruns_ref, x_ref, wg_hbm, wu_hbm, wd_hbm, o_ref,
                wg_st, wu_st, wd_st, wg_bf, wu_bf, wd_bf, sems, state):
    step = pl.program_id(0)
    n_runs = nruns_ref[0]
    streams = ((wg_hbm, wg_st, wg_bf), (wu_hbm, wu_st, wu_bf), (wd_hbm, wd_st, wd_bf))

    def weight_copy(m, run):
        hbm, stage, _ = streams[m]
        return pltpu.make_async_copy(hbm.at[run_e_ref[run]], stage, sems.at[m])

    @pl.when(step == 0)
    def _():
        state[0] = 0
        state[1] = 0
        for m in range(len(streams)):
            weight_copy(m, 0).start(priority=1)

    def switch_to(run, b):
        ring = lax.rem(run, W_RING)
        for m, (_, stage, bf) in enumerate(streams):
            weight_copy(m, run).wait()

            def body(i, _, stage=stage, bf=bf):
                rows = pl.ds(pl.multiple_of(i * CAST_ROWS, CAST_ROWS), CAST_ROWS)
                bf[ring, rows, :] = stage[rows, :].astype(BF16)
                return 0
            lax.fori_loop(0, stage.shape[0] // CAST_ROWS, body, 0)

            @pl.when(run + 1 < n_runs)
            def _(m=m):
                weight_copy(m, run + 1).start(priority=1)
        state[0] = run + 1
        state[1] = b + SWITCH_AFTER

    def run_block(b, row0):
        run = run_id_ref[b]
        pl.when(state[0] == run)(functools.partial(switch_to, run, b))
        pl.when((state[0] == run + 1) & (run + 1 < n_runs) & (b >= state[1]))(
            functools.partial(switch_to, run + 1, b))

        ring = lax.rem(run, W_RING)
        xlo, xhi = _unpack_pair(_load_rows(x_ref, row0, EXPERT_BLOCK))
        xb = jnp.concatenate([xlo.astype(BF16), xhi.astype(BF16)], axis=1)
        g = jnp.dot(xb, wg_bf[ring], preferred_element_type=F32)
        u = jnp.dot(xb, wu_bf[ring], preferred_element_type=F32)
        hmid = (_silu(g) * u).astype(BF16)
        y = jnp.dot(hmid, wd_bf[ring], preferred_element_type=F32)
        _store_rows(o_ref, row0, _pack_pair(y[:, :HALF], y[:, HALF:]))

    for j in range(MOE_SUB):
        b = step * MOE_SUB + j
        row0 = j * EXPERT_BLOCK
        pl.when(used_ref[b] == 1)(functools.partial(run_block, b, row0))

        @pl.when(used_ref[b] == 0)
        def _():
            o_ref[pl.ds(row0 * ROW_SUB, EXPERT_BLOCK * ROW_SUB), :] = jnp.zeros(
                (EXPERT_BLOCK * ROW_SUB, LANES), U32)


def _moe(run_e, run_id, used, n_runs, xbuf, w_gate, w_up, w_down):
    r = xbuf.shape[0]
    rows = MOE_SUB * EXPERT_BLOCK * ROW_SUB
    d, de = w_gate.shape[1:]

    def xmap(s, run_e_ref, run_id_ref, used_ref, nruns_ref):
        del run_e_ref, run_id_ref, nruns_ref
        return (jnp.where(used_ref[s * MOE_SUB] == 1, s, 0), 0)

    grid_spec = pltpu.PrefetchScalarGridSpec(
        num_scalar_prefetch=4,
        grid=(r // rows,),
        in_specs=[pl.BlockSpec((rows, LANES), xmap),
                  pl.BlockSpec(memory_space=pl.ANY),
                  pl.BlockSpec(memory_space=pl.ANY),
                  pl.BlockSpec(memory_space=pl.ANY)],
        out_specs=pl.BlockSpec((rows, LANES), lambda s, *_: (s, 0)),
        scratch_shapes=[pltpu.VMEM((d, de), F32), pltpu.VMEM((d, de), F32), pltpu.VMEM((de, d), F32),
                        pltpu.VMEM((W_RING, d, de), BF16), pltpu.VMEM((W_RING, d, de), BF16),
                        pltpu.VMEM((W_RING, de, d), BF16),
                        pltpu.SemaphoreType.DMA((3,)), pltpu.SMEM((2,), I32)],
    )
    return pl.pallas_call(
        _moe_kernel,
        grid_spec=grid_spec,
        out_shape=jax.ShapeDtypeStruct((r, LANES), U32),
        compiler_params=_cparams(("arbitrary",), vmem=VMEM_LIMIT_MOE),
        name="moe",
    )(run_e, run_id, used, n_runs, xbuf, w_gate, w_up, w_down)


FIN_TM = 256


def _final_kernel(dest_ref, dnext_ref, x_ref, cw_ref, mod_ref, g_ref, y_hbm, o_ref, ybuf, sems):
    i = pl.program_id(0)
    n = pl.num_programs(0)
    tm = x_ref.shape[0]
    slot = i % 2

    def gather(d_ref, s):
        for j in range(tm):
            for k in range(TOP_K):
                d = d_ref[k, 0, 0, j]
                pltpu.make_async_copy(y_hbm.at[_row_slab(d)], ybuf.at[s, k, _row_slab(j)],
                                      sems.at[s]).start(priority=k)

    @pl.when(i == 0)
    def _():
        gather(dest_ref, 0)

    @pl.when(i + 1 < n)
    def _():
        gather(dnext_ref, 1 - slot)

    for k in range(TOP_K):
        pltpu.make_async_copy(y_hbm.at[pl.ds(0, tm * ROW_SUB)], ybuf.at[slot, k], sems.at[slot]).wait()

    cw = cw_ref[...]
    c0, c1 = cw[:, 0:1], cw[:, 1:2]
    lo0, hi0 = _unpack_pair(_load_rows(ybuf.at[slot, 0], 0, tm))
    lo1, hi1 = _unpack_pair(_load_rows(ybuf.at[slot, 1], 0, tm))
    g2 = mod_ref[0, 5:6, :]
    xlo = x_ref[:, :HALF] + g2[:, :HALF] * (lo0 * c0 + lo1 * c1)
    xhi = x_ref[:, HALF:] + g2[:, HALF:] * (hi0 * c0 + hi1 * c1)
    ms = (jnp.sum(xlo * xlo, axis=-1, keepdims=True)
          + jnp.sum(xhi * xhi, axis=-1, keepdims=True)) * (1.0 / D_MODEL)
    r = lax.rsqrt(ms + EPS)
    o_ref[:, :HALF] = xlo * r * g_ref[:, :HALF]
    o_ref[:, HALF:] = xhi * r * g_ref[:, HALF:]


def _final(dest3, x1, cw, mod3, g, ybuf, seq):
    t, d = x1.shape
    tm = FIN_TM
    nt = t // tm
    per_b = seq // tm

    def dspec(shift):
        return pl.BlockSpec((TOP_K, 1, 1, tm), lambda i: (0, jnp.minimum(i + shift, nt - 1), 0, 0),
                            memory_space=pltpu.SMEM)

    return pl.pallas_call(
        _final_kernel,
        grid=(nt,),
        in_specs=[dspec(0), dspec(1),
                  pl.BlockSpec((tm, d), lambda i: (i, 0)),
                  pl.BlockSpec((tm, LANES), lambda i: (i, 0)),
                  pl.BlockSpec((1, 6, d), lambda i: (i // per_b, 0, 0)),
                  pl.BlockSpec((1, d), lambda i: (0, 0)),
                  pl.BlockSpec(memory_space=pl.ANY)],
        out_specs=pl.BlockSpec((tm, d), lambda i: (i, 0)),
        out_shape=jax.ShapeDtypeStruct((t, d), F32),
        scratch_shapes=[pltpu.VMEM((2, TOP_K, tm * ROW_SUB, LANES), U32),
                        pltpu.SemaphoreType.DMA((2,))],
        compiler_params=_cparams(("arbitrary",)),
        name="final",
    )(dest3, dest3, x1, cw, mod3, g, ybuf)


def kernel(x, c, w_ada, b_ada, norm1_g, w_in, conv_w, w_conv_out, w_ret_out, w_o, norm2_g,
           w_router_group, b_router_group, w_router_expert, b_router_expert,
           w_gate, w_up, w_down, norm_f_g):
    batch, seq, d = x.shape
    t = batch * seq
    depth = w_ada.shape[0]
    assert d == D_MODEL and w_in.shape[2] == IN_COLS and w_gate.shape[1] == N_EXPERTS
    assert depth == 1, "the final kernel fuses the last rmsnorm into the single layer"
    n_rows = t * 2 + N_EXPERTS * EXPERT_BLOCK
    n_blk = n_rows // EXPERT_BLOCK
    assert n_blk <= BLK_ROWS

    c_pad = jnp.pad(c, ((0, 8 - batch), (0, 0)))
    xs = x.reshape(t, d)
    for l in range(depth):
        mod = _ada(c_pad, w_ada[l], b_ada[l].reshape(1, -1))[:batch]
        mod3 = mod.reshape(batch, 6, d)

        u = _in_proj(xs, mod3, norm1_g[l].reshape(1, d), w_in[l], seq)
        z, wc_bf, wr_bf, wo_bf = _retention(u, w_conv_out[l], w_ret_out[l], w_o[l], batch, seq)
        w_r = jnp.pad(jnp.concatenate([w_router_group[l], w_router_expert[l]], axis=1),
                      ((0, 0), (0, LANES - N_GROUPS - N_EXPERTS)))
        b_r = jnp.pad(jnp.concatenate([b_router_group[l], b_router_expert[l]]),
                      (0, LANES - N_GROUPS - N_EXPERTS)).reshape(1, LANES)
        x1, h2, ef, cw = _mix(u, z, conv_w[l], wc_bf, wr_bf, wo_bf, xs, mod3,
                              norm2_g[l].reshape(1, d), w_r, b_r, seq)

        dest, blk = _meta(ef)
        blk_e = blk[:n_blk, 0]
        used = blk[:n_blk, 1]
        prev_e = jnp.concatenate([jnp.full((1,), -1, I32), blk_e[:-1]])
        first = ((blk_e != prev_e) & (used == 1)).astype(I32)
        run_id = jnp.maximum(jnp.cumsum(first) - 1, 0).astype(I32)
        n_runs = jnp.sum(first).astype(I32).reshape(1)
        present = jnp.any((blk_e[None, :] == jnp.arange(N_EXPERTS, dtype=I32)[:, None])
                          & (used[None, :] == 1), axis=1)
        eids = jnp.arange(N_EXPERTS, dtype=I32)
        rank = jnp.cumsum(present.astype(I32)) - 1
        run_e = jnp.sum(jnp.where(present[None, :] & (rank[None, :] == eids[:, None]),
                                  eids[None, :], 0), axis=1).astype(I32)

        dest2 = dest[:TOP_K]
        xbuf = _dispatch(blk[:n_blk, 2], dest2.reshape(TOP_K, t // DISP_TM, 1, DISP_TM), h2, n_rows)
        ybuf = _moe(run_e, run_id, used, n_runs, xbuf, w_gate[l], w_up[l], w_down[l])
        xs = _final(dest2.reshape(TOP_K, t // FIN_TM, 1, FIN_TM), x1, cw, mod3,
                    norm_f_g.reshape(1, d), ybuf, seq)
    return xs.reshape(batch, seq, d)
```

```python
import functools

import numpy as np
import jax
import jax.numpy as jnp
from jax import lax
from jax.experimental import pallas as pl
from jax.experimental.pallas import tpu as pltpu

F32 = jnp.float32
BF16 = jnp.bfloat16
I32 = jnp.int32

D_MODEL = 2048
CONV_CH = 1024
CONV_K = 3
RET_HEADS = 8
RET_DK = 128
RET_DV = 256
CHUNK = 128
ROPE_BASE = 10000.0
N_GROUPS = 4
EXPERTS_PER_GROUP = 8
N_EXPERTS = 32
EXPERT_BLOCK = 128
EPS = 1e-6
IN_COLS = 13312
LANES = 128
COL_BLK = 1024
CB_BLK, CC_BLK, CX_BLK, Q_BLK, K_BLK, V_BLK, RG_BLK, GA_BLK, GB_BLK = 0, 1, 2, 3, 4, 5, 7, 9, 11

V7X_VMEM_BYTES = 64 * 1024 * 1024
VMEM_LIMIT = V7X_VMEM_BYTES - 8 * 1024 * 1024
VMEM_LIMIT_MOE = V7X_VMEM_BYTES - 4 * 1024 * 1024


def _cparams(sem, vmem=VMEM_LIMIT):
    return pltpu.CompilerParams(dimension_semantics=sem, vmem_limit_bytes=vmem)


def _sigmoid(v):
    return 1.0 / (1.0 + jnp.exp(-v))


def _silu(v):
    return v * _sigmoid(v)


def _split_bf16(v):
    hi = v.astype(BF16)
    lo = (v - hi.astype(F32)).astype(BF16)
    return hi, lo


def _ada_kernel(c_ref, w_ref, b_ref, o_ref):
    rows = c_ref.shape[0]
    ch, cl = _split_bf16(_silu(c_ref[...]))
    wh, wl = _split_bf16(w_ref[...])
    p = jnp.dot(jnp.concatenate([ch, cl], axis=0), wh, preferred_element_type=F32)
    o_ref[...] = (p[:rows] + (p[rows:] + jnp.dot(ch, wl, preferred_element_type=F32))) + b_ref[...]


def _ada(c_pad, w, b):
    rows, d = c_pad.shape
    n = w.shape[1]
    tn = 1024
    return pl.pallas_call(
        _ada_kernel,
        grid=(n // tn,),
        in_specs=[pl.BlockSpec((rows, d), lambda j: (0, 0)),
                  pl.BlockSpec((d, tn), lambda j: (0, j)),
                  pl.BlockSpec((1, tn), lambda j: (0, j))],
        out_specs=pl.BlockSpec((rows, tn), lambda j: (0, j)),
        out_shape=jax.ShapeDtypeStruct((rows, n), F32),
        compiler_params=_cparams(("arbitrary",)),
        name="ada",
    )(c_pad, w, b)


NORM_ROWS = 64


IN_SPLIT = 2


def _in_kernel(x_ref, mod_ref, g_ref, w_ref, o_ref, h_ref):
    i, j = pl.program_id(0), pl.program_id(1)
    nj = pl.num_programs(1)
    tm, tn = o_ref.shape
    slot = i % 2

    def norm_rows(dst, rows):
        gain = g_ref[...] * (1.0 + mod_ref[0, 1:2, :])
        x = x_ref[rows, :]
        r = lax.rsqrt(jnp.mean(x * x, axis=-1, keepdims=True) + EPS)
        h_ref[dst, rows, :] = (x * r * gain + mod_ref[0, 0:1, :]).astype(BF16)

    @pl.when((i == 0) & (j == 0))
    def _():
        def body(k, _):
            norm_rows(0, pl.ds(pl.multiple_of(k * NORM_ROWS, NORM_ROWS), NORM_ROWS))
            return 0
        lax.fori_loop(0, tm // NORM_ROWS, body, 0)

    @pl.when(j < nj - 1)
    def _():
        o_ref[...] = jnp.dot(h_ref[slot], w_ref[...].astype(BF16),
                             preferred_element_type=F32).astype(o_ref.dtype)

    @pl.when(j == nj - 1)
    def _():
        h = h_ref[slot]
        cw, rw = tn // IN_SPLIT, tm // IN_SPLIT
        for q in range(IN_SPLIT):
            cols = slice(q * cw, (q + 1) * cw)
            o_ref[:, cols] = jnp.dot(h, w_ref[:, cols].astype(BF16),
                                     preferred_element_type=F32).astype(o_ref.dtype)
            for k in range(rw // NORM_ROWS):
                norm_rows(1 - slot, slice(q * rw + k * NORM_ROWS, q * rw + (k + 1) * NORM_ROWS))


def _in_proj(x2, mod3, g, w, seq):
    t, d = x2.shape
    n = w.shape[1]
    tm, tn = 1024, 1024
    nt, nj = t // tm, n // tn
    per_b = seq // tm

    def xtile(i, j):
        return jnp.minimum(i + (j == nj - 1).astype(jnp.int32), nt - 1)

    return pl.pallas_call(
        _in_kernel,
        grid=(nt, nj),
        in_specs=[pl.BlockSpec((tm, d), lambda i, j: (xtile(i, j), 0)),
                  pl.BlockSpec((1, 6, d), lambda i, j: (xtile(i, j) // per_b, 0, 0)),
                  pl.BlockSpec((1, d), lambda i, j: (0, 0)),
                  pl.BlockSpec((d, tn), lambda i, j: (0, j))],
        out_specs=pl.BlockSpec((tm, tn), lambda i, j: (i, j)),
        out_shape=jax.ShapeDtypeStruct((t, n), BF16),
        scratch_shapes=[pltpu.VMEM((2, tm, d), BF16)],
        compiler_params=_cparams(("arbitrary", "arbitrary")),
        name="in_proj",
    )(x2, mod3, g, w)


RET_CPS = 4


def _ret_kernel(q_ref, k_ref, v0_ref, v1_ref, rg0_ref, rg1_ref, cos_ref, sin_ref,
                idec_ref, qdec_ref, kdec_ref, wc_ref, wr_ref, wo_ref,
                o_ref, wcb_ref, wrb_ref, wob_ref, state_ref, *, chunk_decay):
    @pl.when(pl.program_id(1) == 0)
    def _():
        state_ref[...] = jnp.zeros_like(state_ref)

    wcb_ref[...] = wc_ref[...].astype(BF16)
    wrb_ref[...] = wr_ref[...].astype(BF16)
    wob_ref[...] = wo_ref[...].astype(BF16)

    def rot(t, cos, sin):
        return t * cos + pltpu.roll(t, RET_DK // 2, axis=1) * sin

    for ci in range(RET_CPS):
        rows = slice(ci * CHUNK, (ci + 1) * CHUNK)
        cos = cos_ref[rows, :]
        sin = sin_ref[rows, :]
        for h in range(RET_HEADS):
            v_ref, rg_ref = (v0_ref, rg0_ref) if h < RET_HEADS // 2 else (v1_ref, rg1_ref)
            vc = (h % (RET_HEADS // 2)) * RET_DV
            qh = rot(q_ref[rows, h * RET_DK:(h + 1) * RET_DK].astype(F32), cos, sin)
            kh = rot(k_ref[rows, h * RET_DK:(h + 1) * RET_DK].astype(F32), cos, sin) * (RET_DK ** -0.5)
            vh = v_ref[rows, vc:vc + RET_DV]
            qb = qh.astype(BF16)
            kb = kh.astype(BF16)
            scores = lax.dot_general(qb, kb, (((1,), (1,)), ((), ())),
                                     preferred_element_type=F32) * idec_ref[h]
            inner = jnp.dot(scores.astype(BF16), vh, preferred_element_type=F32)
            state = state_ref[h]
            cross = jnp.dot((qh * qdec_ref[h]).astype(BF16), state.astype(BF16),
                            preferred_element_type=F32)
            kv = lax.dot_general((kh * kdec_ref[h]).astype(BF16), vh, (((0,), (0,)), ((), ())),
                                 preferred_element_type=F32)
            state_ref[h] = state * chunk_decay[h] + kv
            r = inner + cross
            r = r * lax.rsqrt(jnp.mean(r * r, axis=-1, keepdims=True) + EPS)
            gate = _silu(rg_ref[rows, vc:vc + RET_DV].astype(F32))
            o_ref[rows, h * RET_DV:(h + 1) * RET_DV] = (gate * r).astype(o_ref.dtype)


def _retention(u, w_conv_out, w_ret_out, w_o, batch, seq):
    t = u.shape[0]
    f32 = np.float32
    inv = (f32(ROPE_BASE) ** (-np.arange(0, RET_DK, 2, dtype=f32) / f32(RET_DK))).astype(f32)
    ang = np.arange(seq, dtype=f32)[:, None] * inv[None, :]
    cos, sin = np.cos(ang).astype(f32), np.sin(ang).astype(f32)
    cos_t = np.concatenate([cos, cos], axis=-1)
    sin_t = np.concatenate([-sin, sin], axis=-1)
    log_g = np.log1p(-(f32(2.0) ** (f32(-5.0) - np.arange(RET_HEADS, dtype=f32)))).astype(f32)
    i = np.arange(CHUNK, dtype=f32)
    diff = i[:, None] - i[None, :]
    idec = np.where(diff >= 0, np.exp(log_g[:, None, None] * np.maximum(diff, f32(0.0))), f32(0.0)).astype(f32)
    qdec = np.broadcast_to(np.exp(log_g[:, None] * (i + f32(1.0))).astype(f32)[:, :, None],
                           (RET_HEADS, CHUNK, RET_DK))
    kdec = np.broadcast_to(np.exp(log_g[:, None] * (f32(CHUNK - 1.0) - i)).astype(f32)[:, :, None],
                           (RET_HEADS, CHUNK, RET_DK))
    chunk_decay = tuple(float(np.exp(lg * f32(CHUNK))) for lg in log_g)

    rows = RET_CPS * CHUNK
    nst = seq // rows

    def ublk(c):
        return pl.BlockSpec((rows, COL_BLK), lambda b, n, c=c: (b * nst + n, c))

    def table():
        return pl.BlockSpec((RET_HEADS, CHUNK, RET_DK), lambda b, n: (0, 0, 0))

    def slab(w):
        n_steps = batch * nst
        assert w.shape[0] % (8 * n_steps) == 0
        return pl.BlockSpec((w.shape[0] // n_steps, w.shape[1]), lambda b, n: (b * nst + n, 0))

    weights = (w_conv_out, w_ret_out, w_o)
    return pl.pallas_call(
        functools.partial(_ret_kernel, chunk_decay=chunk_decay),
        grid=(batch, nst),
        in_specs=[ublk(Q_BLK), ublk(K_BLK), ublk(V_BLK), ublk(V_BLK + 1),
                  ublk(RG_BLK), ublk(RG_BLK + 1),
                  pl.BlockSpec((rows, RET_DK), lambda b, n: (n, 0)),
                  pl.BlockSpec((rows, RET_DK), lambda b, n: (n, 0)),
                  table(), table(), table()] + [slab(w) for w in weights],
        out_specs=[pl.BlockSpec((rows, RET_HEADS * RET_DV), lambda b, n: (b * nst + n, 0))]
        + [slab(w) for w in weights],
        out_shape=[jax.ShapeDtypeStruct((t, RET_HEADS * RET_DV), BF16)]
        + [jax.ShapeDtypeStruct(w.shape, BF16) for w in weights],
        scratch_shapes=[pltpu.VMEM((RET_HEADS, RET_DK, RET_DV), F32)],
        compiler_params=_cparams(("arbitrary", "arbitrary")),
        name="retention",
    )(u, u, u, u, u, u, *(jnp.asarray(tab) for tab in (cos_t, sin_t, idec, qdec, kdec)), *weights)


HALO = 8


MIX_TN = 512


def _mix_kernel(cb_ref, cc_ref, cx_ref, hcc_ref, hcx_ref, z_ref, ga0_ref, ga1_ref, gb0_ref, gb1_ref,
                cw_ref, wc_ref, wr_ref, wo_ref, x_ref, mod_ref, modp_ref, g2_ref, wrt_ref, brt_ref,
                o_ref, hp_ref, ef_ref, rw_ref, m_ref, mg_ref, x1p_ref, *, tiles_per_seq, n_tiles):
    tm = cb_ref.shape[0]
    d = o_ref.shape[1]
    step = pl.program_id(0)

    slot = step % 2

    @pl.when(step == 0)
    def _():
        x1p_ref[1] = jnp.zeros(x1p_ref.shape[1:], F32)

    m = cc_ref[...].astype(F32) * cx_ref[...].astype(F32)
    halo = hcc_ref[...].astype(F32) * hcx_ref[...].astype(F32)
    first = (jnp.minimum(step, n_tiles - 1) % tiles_per_seq) == 0
    m_ref[0:HALO, :] = jnp.where(first, 0.0, halo)
    m_ref[HALO:HALO + tm, :] = m
    conv = (m_ref[HALO - 2:HALO - 2 + tm, :] * cw_ref[0:1, :]
            + m_ref[HALO - 1:HALO - 1 + tm, :] * cw_ref[1:2, :]
            + m * cw_ref[2:3, :])
    p = (cb_ref[...].astype(F32) * conv).astype(BF16)
    z = z_ref[...]
    nchunk = d // MIX_TN
    prev = 1 - slot
    ssq = jnp.zeros((tm, 1), F32)
    for n in range(nchunk):
        cols = slice(n * MIX_TN, (n + 1) * MIX_TN)
        ga_ref, gb_ref = (ga0_ref, gb0_ref) if n < COL_BLK // MIX_TN else (ga1_ref, gb1_ref)
        gcols = slice((n * MIX_TN) % COL_BLK, (n * MIX_TN) % COL_BLK + MIX_TN)
        ya = jnp.dot(p, wc_ref[:, cols], preferred_element_type=F32)
        yb = jnp.dot(z, wr_ref[:, cols], preferred_element_type=F32)
        merged = (_sigmoid(ga_ref[:, gcols].astype(F32)) * ya
                  + _sigmoid(gb_ref[:, gcols].astype(F32)) * yb)
        mg_ref[:, cols] = merged.astype(BF16)
        xp = x1p_ref[prev, :, cols]
        ssq = ssq + jnp.sum(xp * xp, axis=-1, keepdims=True)
    rnorm = lax.rsqrt(ssq * (1.0 / d) + EPS)
    gain = g2_ref[...] * (1.0 + modp_ref[0, 4:5, :])
    shift = modp_ref[0, 3:4, :]
    wh, wl = _split_bf16(wrt_ref[...])
    whl = jnp.concatenate([wh, wl], axis=1)
    acc_hw = jnp.zeros((tm, 2 * LANES), F32)
    acc_lw = jnp.zeros((tm, LANES), F32)
    mg = mg_ref[...]
    for n in range(nchunk):
        cols = slice(n * MIX_TN, (n + 1) * MIX_TN)
        acc = jnp.dot(mg, wo_ref[:, cols], preferred_element_type=F32)
        x1 = x_ref[:, cols] + mod_ref[0, 2:3, cols] * acc
        o_ref[:, cols] = x1
        x1p_ref[slot, :, cols] = x1
        if n % 2 == 0:
            c = (n // 2) * MIX_TN
            halves = []
            for c0 in (c, c + HALF):
                hcols = slice(c0, c0 + MIX_TN)
                h2 = x1p_ref[prev, :, hcols] * rnorm * gain[:, hcols] + shift[:, hcols]
                hh, hl = _split_bf16(h2)
                acc_hw = acc_hw + jnp.dot(hh, whl[hcols, :], preferred_element_type=F32)
                acc_lw = acc_lw + jnp.dot(hl, wh[hcols, :], preferred_element_type=F32)
                halves.append(h2)
            hp_ref[:, c:c + MIX_TN] = _pack_pair(halves[0], halves[1])
    logits = (acc_hw[:, :LANES] + (acc_hw[:, LANES:] + acc_lw)) + brt_ref[...]
    ef_ref[...], rw_ref[...] = _route_from_logits(logits)


def _mix(u, z, conv_w, wc_bf, wr_bf, wo_bf, x2, mod3, g2, w_rt, b_rt, seq):
    t, d = x2.shape
    tm = 256
    nt = t // tm
    hb = tm // HALO
    per_b = seq // tm

    def cur(i):
        return jnp.minimum(i, nt - 1)

    def prev(i):
        return jnp.maximum(i - 1, 0)

    def ublk(c):
        return pl.BlockSpec((tm, COL_BLK), lambda i, c=c: (cur(i), c))

    def halo(c):
        return pl.BlockSpec((HALO, COL_BLK), lambda i, c=c: (jnp.maximum(cur(i) * hb - 1, 0), c))

    def resident(shape):
        return pl.BlockSpec(shape, lambda i: (0, 0), pipeline_mode=pl.Buffered(1))

    return pl.pallas_call(
        functools.partial(_mix_kernel, tiles_per_seq=per_b, n_tiles=nt),
        grid=(nt + 1,),
        in_specs=[ublk(CB_BLK), ublk(CC_BLK), ublk(CX_BLK), halo(CC_BLK), halo(CX_BLK),
                  pl.BlockSpec((tm, z.shape[1]), lambda i: (cur(i), 0)),
                  ublk(GA_BLK), ublk(GA_BLK + 1), ublk(GB_BLK), ublk(GB_BLK + 1),
                  pl.BlockSpec((CONV_K, CONV_CH), lambda i: (0, 0)),
                  resident(wc_bf.shape), resident(wr_bf.shape), resident(wo_bf.shape),
                  pl.BlockSpec((tm, d), lambda i: (cur(i), 0)),
                  pl.BlockSpec((1, 6, d), lambda i: (cur(i) // per_b, 0, 0)),
                  pl.BlockSpec((1, 6, d), lambda i: (prev(i) // per_b, 0, 0)),
                  pl.BlockSpec((1, d), lambda i: (0, 0)),
                  pl.BlockSpec((d, LANES), lambda i: (0, 0)),
                  pl.BlockSpec((1, LANES), lambda i: (0, 0))],
        out_specs=[pl.BlockSpec((tm, d), lambda i: (cur(i), 0)),
                   pl.BlockSpec((tm, HALF), lambda i: (prev(i), 0)),
                   pl.BlockSpec((tm, LANES), lambda i: (prev(i), 0)),
                   pl.BlockSpec((tm, LANES), lambda i: (prev(i), 0))],
        out_shape=[jax.ShapeDtypeStruct((t, d), F32),
                   jax.ShapeDtypeStruct((t, HALF), U32),
                   jax.ShapeDtypeStruct((t, LANES), F32),
                   jax.ShapeDtypeStruct((t, LANES), F32)],
        scratch_shapes=[pltpu.VMEM((tm + HALO, CONV_CH), F32),
                        pltpu.VMEM((tm, d), BF16),
                        pltpu.VMEM((2, tm, d), F32)],
        compiler_params=_cparams(("arbitrary",)),
        name="mix",
    )(u, u, u, u, u, z, u, u, u, u, conv_w, wc_bf, wr_bf, wo_bf, x2, mod3, mod3, g2, w_rt, b_rt)


ROUTE_E0 = N_GROUPS


U32 = jnp.uint32
HALF = D_MODEL // 2


def _pack_pair(lo, hi):
    ulo = lax.bitcast_convert_type(lo.astype(BF16).astype(F32), U32)
    uhi = lax.bitcast_convert_type(hi.astype(BF16).astype(F32), U32)
    return (ulo >> 16) | uhi


def _unpack_pair(w):
    lo = lax.bitcast_convert_type(w << 16, F32)
    hi = lax.bitcast_convert_type(w & jnp.uint32(0xFFFF0000), F32)
    return lo, hi


ROW_SUB = HALF // LANES


def _row_slab(r):
    if isinstance(r, int):
        return pl.ds(r * ROW_SUB, ROW_SUB)
    return pl.ds(pl.multiple_of(r * ROW_SUB, ROW_SUB), ROW_SUB)


def _store_rows(ref, row0, v):
    for c in range(ROW_SUB):
        ref[pl.ds(row0 * ROW_SUB + c, v.shape[0], stride=ROW_SUB), :] = v[:, c * LANES:(c + 1) * LANES]


def _load_rows(ref, row0, n):
    return jnp.concatenate(
        [ref[pl.ds(row0 * ROW_SUB + c, n, stride=ROW_SUB), :] for c in range(ROW_SUB)], axis=1)


def _route_from_logits(logits):
    lane = lax.broadcasted_iota(I32, logits.shape, 1).astype(F32)
    neg = -jnp.inf
    big = float(LANES)
    gl = jnp.where(lane < N_GROUPS, logits, neg)
    gmax = jnp.max(gl, axis=-1, keepdims=True)
    gsel = jnp.min(jnp.where(gl == gmax, lane, big), axis=-1, keepdims=True)
    gsum = jnp.sum(jnp.exp(gl - gmax), axis=-1, keepdims=True)
    g_w = 1.0 / gsum
    lo_lane = ROUTE_E0 + EXPERTS_PER_GROUP * gsel
    emask = (lane >= lo_lane) & (lane < lo_lane + EXPERTS_PER_GROUP)
    el = jnp.where(emask, logits, neg)
    m1 = jnp.max(el, axis=-1, keepdims=True)
    i1 = jnp.min(jnp.where(el == m1, lane, big), axis=-1, keepdims=True)
    el2 = jnp.where(lane == i1, neg, el)
    m2 = jnp.max(el2, axis=-1, keepdims=True)
    i2 = jnp.min(jnp.where(el2 == m2, lane, big), axis=-1, keepdims=True)
    esum = jnp.sum(jnp.exp(el - m1), axis=-1, keepdims=True)
    p1 = 1.0 / esum
    p2 = jnp.exp(m2 - m1) / esum
    tot = p1 + p2
    c1 = g_w * (p1 / tot)
    c2 = g_w * (p2 / tot)
    ef = jnp.where(lane == 0.0, i1 - ROUTE_E0, jnp.where(lane == 1.0, i2 - ROUTE_E0, 0.0))
    cw = jnp.where(lane == 0.0, c1, jnp.where(lane == 1.0, c2, 0.0))
    return ef, cw


META_TB = 512
DEST_ROWS = 8
BLK_ROWS = 256


def _meta_kernel(ef_ref, dest_ref, blk_ref, pref_ref):
    t = ef_ref.shape[0]
    nb = t // META_TB
    lane = lax.broadcasted_iota(I32, (META_TB, LANES), 1).astype(F32)
    r_i = lax.broadcasted_iota(I32, (META_TB, META_TB), 0)
    c_i = lax.broadcasted_iota(I32, (META_TB, META_TB), 1)
    tril = jnp.where(r_i > c_i, 1.0, 0.0).astype(BF16)

    def onehots(i):
        ef = ef_ref[pl.ds(i * META_TB, META_TB), :]
        oh1 = jnp.where(lane == ef[:, 0:1], 1.0, 0.0)
        oh2 = jnp.where(lane == ef[:, 1:2], 1.0, 0.0)
        return oh1, oh2

    def pass1(i, carry):
        oh1, oh2 = onehots(i)
        cnt = oh1 + oh2
        pref = jnp.dot(tril, cnt.astype(BF16), preferred_element_type=F32) + carry
        pref_ref[pl.ds(i * META_TB, META_TB), :] = pref
        return carry + jnp.sum(cnt, axis=0, keepdims=True)

    counts = lax.fori_loop(0, nb, pass1, jnp.zeros((1, LANES), F32))
    nblk = jnp.floor((counts + (EXPERT_BLOCK - 1)) * (1.0 / EXPERT_BLOCK))
    u_r = lax.broadcasted_iota(I32, (LANES, LANES), 0)
    u_c = lax.broadcasted_iota(I32, (LANES, LANES), 1)
    upper = jnp.where(u_r <= u_c, 1.0, 0.0).astype(BF16)
    nblk8 = jnp.broadcast_to(nblk, (8, LANES))
    pend_blk = jnp.dot(nblk8.astype(BF16), upper, preferred_element_type=F32)[0:1, :]
    pstart_blk = pend_blk - nblk
    pstart = pstart_blk * float(EXPERT_BLOCK)

    for i in range(nb):
        oh1, oh2 = onehots(i)
        base = pref_ref[pl.ds(i * META_TB, META_TB), :] + pstart
        d1 = jnp.sum(oh1 * base, axis=-1, keepdims=True)
        d2 = jnp.sum(oh2 * base, axis=-1, keepdims=True)
        dest = jnp.where(lane == 0.0, d1, jnp.where(lane == 1.0, d2, 0.0))
        dest_ref[:, i * META_TB:(i + 1) * META_TB] = dest.T[:DEST_ROWS, :].astype(I32)

    blane = lax.broadcasted_iota(I32, (BLK_ROWS, LANES), 1).astype(F32)
    brow = lax.broadcasted_iota(I32, (BLK_ROWS, LANES), 0).astype(F32)
    emask = blane < N_EXPERTS
    owner = jnp.sum(jnp.where(emask & (pend_blk <= brow), 1.0, 0.0), axis=-1, keepdims=True)
    owner = jnp.minimum(owner, float(N_EXPERTS - 1))
    total = jnp.sum(jnp.where(emask, nblk, 0.0), axis=-1, keepdims=True)
    used = jnp.where(brow < total, 1.0, 0.0)
    mine = emask & (blane == owner)
    cnt_b = jnp.sum(jnp.where(mine, counts, 0.0), axis=-1, keepdims=True)
    ps_b = jnp.sum(jnp.where(mine, pstart_blk, 0.0), axis=-1, keepdims=True)
    nvalid = jnp.clip(cnt_b - (brow - ps_b) * float(EXPERT_BLOCK), 0.0, float(EXPERT_BLOCK))
    pad_lo = nvalid * used
    blk_ref[...] = jnp.where(blane == 0.0, owner,
                             jnp.where(blane == 1.0, used,
                                       jnp.where(blane == 2.0, pad_lo, 0.0))).astype(I32)


def _meta(ef):
    t = ef.shape[0]
    return pl.pallas_call(
        _meta_kernel,
        out_shape=[jax.ShapeDtypeStruct((DEST_ROWS, t), I32),
                   jax.ShapeDtypeStruct((BLK_ROWS, LANES), I32)],
        scratch_shapes=[pltpu.VMEM((t, LANES), F32)],
        compiler_params=pltpu.CompilerParams(vmem_limit_bytes=VMEM_LIMIT),
        name="meta",
    )(ef)


DISP_TM = 256
TOP_K = 2


def _dispatch_kernel(pad_lo_ref, dest_ref, h_ref, xbuf_ref, stage, zrow, sems, zsem):
    i = pl.program_id(0)
    n = pl.num_programs(0)
    tm = h_ref.shape[0]
    n_blk = pad_lo_ref.shape[0]
    slot = i % 2

    def pad_rows(action):
        def blk(b, _):
            lo = pad_lo_ref[b]
            base = pl.multiple_of(b * EXPERT_BLOCK, EXPERT_BLOCK)

            @pl.when(lo == 0)
            def _():
                whole = pl.ds(pl.multiple_of(base * ROW_SUB, EXPERT_BLOCK * ROW_SUB),
                              EXPERT_BLOCK * ROW_SUB)
                action(pltpu.make_async_copy(zrow, xbuf_ref.at[whole], zsem))

            @pl.when(lo > 0)
            def _():
                def row(r, _):
                    action(pltpu.make_async_copy(zrow.at[_row_slab(0)],
                                                 xbuf_ref.at[_row_slab(base + r)], zsem))
                    return 0
                lax.fori_loop(lo, EXPERT_BLOCK, row, 0)
            return 0
        lax.fori_loop(0, n_blk, blk, 0)

    @pl.when(i == 0)
    def _():
        zrow[...] = jnp.zeros_like(zrow)
        pad_rows(lambda cp: cp.start())

    _store_rows(stage.at[slot], 0, h_ref[...])
    for j in range(tm):
        for k in range(TOP_K):
            d = dest_ref[k, 0, 0, j]
            pltpu.make_async_copy(stage.at[slot, _row_slab(j)], xbuf_ref.at[_row_slab(d)],
                                  sems.at[slot]).start(priority=k)

    def wait_tile(s):
        for _ in range(TOP_K):
            pltpu.make_async_copy(stage.at[s], xbuf_ref.at[pl.ds(0, tm * ROW_SUB)], sems.at[s]).wait()

    @pl.when(i > 0)
    def _():
        wait_tile(1 - slot)

    @pl.when(i == n - 1)
    def _():
        wait_tile(slot)
        pad_rows(lambda cp: cp.wait())


def _dispatch(pad_lo, dest3, hp, n_rows):
    t, w = hp.shape
    tm = DISP_TM
    grid_spec = pltpu.PrefetchScalarGridSpec(
        num_scalar_prefetch=1,
        grid=(t // tm,),
        in_specs=[pl.BlockSpec((TOP_K, 1, 1, tm), lambda i, *_: (0, i, 0, 0), memory_space=pltpu.SMEM),
                  pl.BlockSpec((tm, w), lambda i, *_: (i, 0))],
        out_specs=pl.BlockSpec(memory_space=pl.ANY),
        scratch_shapes=[pltpu.VMEM((2, tm * ROW_SUB, LANES), U32),
                        pltpu.VMEM((EXPERT_BLOCK * ROW_SUB, LANES), U32),
                        pltpu.SemaphoreType.DMA((2,)), pltpu.SemaphoreType.DMA(())],
    )
    return pl.pallas_call(
        _dispatch_kernel,
        grid_spec=grid_spec,
        out_shape=jax.ShapeDtypeStruct((n_rows * ROW_SUB, LANES), U32),
        compiler_params=_cparams(("arbitrary",)),
        name="dispatch",
    )(pad_lo, dest3, hp)


MOE_SUB = 4
W_RING = 2
CAST_ROWS = 256
SWITCH_AFTER = 4


def _moe_kernel(run_e_ref, run_id_ref, used_ref, nruns_ref, x_ref, wg_hbm, wu_hbm, wd_hbm, o_ref,
                wg_st, wu_st, wd_st, wg_bf, wu_bf, wd_bf, sems, state):
    step = pl.program_id(0)
    n_runs = nruns_ref[0]
    streams = ((wg_hbm, wg_st, wg_bf), (wu_hbm, wu_st, wu_bf), (wd_hbm, wd_st, wd_bf))

    def weight_copy(m, run):
        hbm, stage, _ = streams[m]
        return pltpu.make_async_copy(hbm.at[run_e_ref[run]], stage, sems.at[m])

    @pl.when(step == 0)
    def _():
        state[0] = 0
        state[1] = 0
        for m in range(len(streams)):
            weight_copy(m, 0).start(priority=1)

    def switch_to(run, b):
        ring = lax.rem(run, W_RING)
        for m, (_, stage, bf) in enumerate(streams):
            weight_copy(m, run).wait()

            def body(i, _, stage=stage, bf=bf):
                rows = pl.ds(pl.multiple_of(i * CAST_ROWS, CAST_ROWS), CAST_ROWS)
                bf[ring, rows, :] = stage[rows, :].astype(BF16)
                return 0
            lax.fori_loop(0, stage.shape[0] // CAST_ROWS, body, 0)

            @pl.when(run + 1 < n_runs)
            def _(m=m):
                weight_copy(m, run + 1).start(priority=1)
        state[0] = run + 1
        state[1] = b + SWITCH_AFTER

    def run_block(b, row0):
        run = run_id_ref[b]
        pl.when(state[0] == run)(functools.partial(switch_to, run, b))
        pl.when((state[0] == run + 1) & (run + 1 < n_runs) & (b >= state[1]))(
            functools.partial(switch_to, run + 1, b))

        ring = lax.rem(run, W_RING)
        xlo, xhi = _unpack_pair(_load_rows(x_ref, row0, EXPERT_BLOCK))
        xb = jnp.concatenate([xlo.astype(BF16), xhi.astype(BF16)], axis=1)
        g = jnp.dot(xb, wg_bf[ring], preferred_element_type=F32)
        u = jnp.dot(xb, wu_bf[ring], preferred_element_type=F32)
        hmid = (_silu(g) * u).astype(BF16)
        y = jnp.dot(hmid, wd_bf[ring], preferred_element_type=F32)
        _store_rows(o_ref, row0, _pack_pair(y[:, :HALF], y[:, HALF:]))

    for j in range(MOE_SUB):
        b = step * MOE_SUB + j
        row0 = j * EXPERT_BLOCK
        pl.when(used_ref[b] == 1)(functools.partial(run_block, b, row0))

        @pl.when(used_ref[b] == 0)
        def _():
            o_ref[pl.ds(row0 * ROW_SUB, EXPERT_BLOCK * ROW_SUB), :] = jnp.zeros(
                (EXPERT_BLOCK * ROW_SUB, LANES), U32)


def _moe(run_e, run_id, used, n_runs, xbuf, w_gate, w_up, w_down):
    r = xbuf.shape[0]
    rows = MOE_SUB * EXPERT_BLOCK * ROW_SUB
    d, de = w_gate.shape[1:]

    def xmap(s, run_e_ref, run_id_ref, used_ref, nruns_ref):
        del run_e_ref, run_id_ref, nruns_ref
        return (jnp.where(used_ref[s * MOE_SUB] == 1, s, 0), 0)

    grid_spec = pltpu.PrefetchScalarGridSpec(
        num_scalar_prefetch=4,
        grid=(r // rows,),
        in_specs=[pl.BlockSpec((rows, LANES), xmap),
                  pl.BlockSpec(memory_space=pl.ANY),
                  pl.BlockSpec(memory_space=pl.ANY),
                  pl.BlockSpec(memory_space=pl.ANY)],
        out_specs=pl.BlockSpec((rows, LANES), lambda s, *_: (s, 0)),
        scratch_shapes=[pltpu.VMEM((d, de), F32), pltpu.VMEM((d, de), F32), pltpu.VMEM((de, d), F32),
                        pltpu.VMEM((W_RING, d, de), BF16), pltpu.VMEM((W_RING, d, de), BF16),
                        pltpu.VMEM((W_RING, de, d), BF16),
                        pltpu.SemaphoreType.DMA((3,)), pltpu.SMEM((2,), I32)],
    )
    return pl.pallas_call(
        _moe_kernel,
        grid_spec=grid_spec,
        out_shape=jax.ShapeDtypeStruct((r, LANES), U32),
        compiler_params=_cparams(("arbitrary",), vmem=VMEM_LIMIT_MOE),
        name="moe",
    )(run_e, run_id, used, n_runs, xbuf, w_gate, w_up, w_down)


FIN_TM = 256


FIN_GROUP = 32


def _final_kernel(dest_ref, dnext_ref, x_ref, cw_ref, mod_ref, g_ref, y_hbm, o_ref, ya, yb, sems):
    i = pl.program_id(0)
    n = pl.num_programs(0)
    tm = x_ref.shape[0]

    def gather(d_ref, buf, sem, rows):
        for j in rows:
            for k in range(TOP_K):
                d = d_ref[k, 0, 0, j]
                pltpu.make_async_copy(y_hbm.at[_row_slab(d)], buf.at[k, _row_slab(j)],
                                      sem).start(priority=k)

    def wait_tile(buf, sem):
        for k in range(TOP_K):
            pltpu.make_async_copy(y_hbm.at[pl.ds(0, tm * ROW_SUB)], buf.at[k], sem).wait()

    @pl.when(i == 0)
    def _():
        gather(dest_ref, ya, sems.at[0], range(tm))

    def step(cur, nxt, sem_cur, sem_nxt):
        wait_tile(cur, sem_cur)
        g2 = mod_ref[0, 5:6, :]
        for r0 in range(0, tm, FIN_GROUP):
            gather(dnext_ref, nxt, sem_nxt, range(r0, r0 + FIN_GROUP))
            rows = slice(r0, r0 + FIN_GROUP)
            cw = cw_ref[rows, :]
            c0, c1 = cw[:, 0:1], cw[:, 1:2]
            lo0, hi0 = _unpack_pair(_load_rows(cur.at[0], r0, FIN_GROUP))
            lo1, hi1 = _unpack_pair(_load_rows(cur.at[1], r0, FIN_GROUP))
            xlo = x_ref[rows, :HALF] + g2[:, :HALF] * (lo0 * c0 + lo1 * c1)
            xhi = x_ref[rows, HALF:] + g2[:, HALF:] * (hi0 * c0 + hi1 * c1)
            ms = (jnp.sum(xlo * xlo, axis=-1, keepdims=True)
                  + jnp.sum(xhi * xhi, axis=-1, keepdims=True)) * (1.0 / D_MODEL)
            r = lax.rsqrt(ms + EPS)
            o_ref[rows, :HALF] = xlo * r * g_ref[:, :HALF]
            o_ref[rows, HALF:] = xhi * r * g_ref[:, HALF:]

        @pl.when(i == n - 1)
        def _():
            wait_tile(nxt, sem_nxt)

    pl.when(i % 2 == 0)(functools.partial(step, ya, yb, sems.at[0], sems.at[1]))
    pl.when(i % 2 == 1)(functools.partial(step, yb, ya, sems.at[1], sems.at[0]))


def _final(dest3, x1, cw, mod3, g, ybuf, seq):
    t, d = x1.shape
    tm = FIN_TM
    nt = t // tm
    per_b = seq // tm

    def dspec(shift):
        return pl.BlockSpec((TOP_K, 1, 1, tm), lambda i: (0, jnp.minimum(i + shift, nt - 1), 0, 0),
                            memory_space=pltpu.SMEM)

    return pl.pallas_call(
        _final_kernel,
        grid=(nt,),
        in_specs=[dspec(0), dspec(1),
                  pl.BlockSpec((tm, d), lambda i: (i, 0)),
                  pl.BlockSpec((tm, LANES), lambda i: (i, 0)),
                  pl.BlockSpec((1, 6, d), lambda i: (i // per_b, 0, 0)),
                  pl.BlockSpec((1, d), lambda i: (0, 0)),
                  pl.BlockSpec(memory_space=pl.ANY)],
        out_specs=pl.BlockSpec((tm, d), lambda i: (i, 0)),
        out_shape=jax.ShapeDtypeStruct((t, d), F32),
        scratch_shapes=[pltpu.VMEM((TOP_K, tm * ROW_SUB, LANES), U32),
                        pltpu.VMEM((TOP_K, tm * ROW_SUB, LANES), U32),
                        pltpu.SemaphoreType.DMA((2,))],
        compiler_params=_cparams(("arbitrary",)),
        name="final",
    )(dest3, dest3, x1, cw, mod3, g, ybuf)


def kernel(x, c, w_ada, b_ada, norm1_g, w_in, conv_w, w_conv_out, w_ret_out, w_o, norm2_g,
           w_router_group, b_router_group, w_router_expert, b_router_expert,
           w_gate, w_up, w_down, norm_f_g):
    batch, seq, d = x.shape
    t = batch * seq
    depth = w_ada.shape[0]
    assert d == D_MODEL and w_in.shape[2] == IN_COLS and w_gate.shape[1] == N_EXPERTS
    assert depth == 1, "the final kernel fuses the last rmsnorm into the single layer"
    n_rows = t * 2 + N_EXPERTS * EXPERT_BLOCK
    n_blk = n_rows // EXPERT_BLOCK
    assert n_blk <= BLK_ROWS

    c_pad = jnp.pad(c, ((0, 8 - batch), (0, 0)))
    xs = x.reshape(t, d)
    for l in range(depth):
        mod = _ada(c_pad, w_ada[l], b_ada[l].reshape(1, -1))[:batch]
        mod3 = mod.reshape(batch, 6, d)

        u = _in_proj(xs, mod3, norm1_g[l].reshape(1, d), w_in[l], seq)
        z, wc_bf, wr_bf, wo_bf = _retention(u, w_conv_out[l], w_ret_out[l], w_o[l], batch, seq)
        w_r = jnp.pad(jnp.concatenate([w_router_group[l], w_router_expert[l]], axis=1),
                      ((0, 0), (0, LANES - N_GROUPS - N_EXPERTS)))
        b_r = jnp.pad(jnp.concatenate([b_router_group[l], b_router_expert[l]]),
                      (0, LANES - N_GROUPS - N_EXPERTS)).reshape(1, LANES)
        x1, h2, ef, cw = _mix(u, z, conv_w[l], wc_bf, wr_bf, wo_bf, xs, mod3,
                              norm2_g[l].reshape(1, d), w_r, b_r, seq)

        dest, blk = _meta(ef)
        blk_e = blk[:n_blk, 0]
        used = blk[:n_blk, 1]
        prev_e = jnp.concatenate([jnp.full((1,), -1, I32), blk_e[:-1]])
        first = ((blk_e != prev_e) & (used == 1)).astype(I32)
        run_id = jnp.maximum(jnp.cumsum(first) - 1, 0).astype(I32)
        n_runs = jnp.sum(first).astype(I32).reshape(1)
        present = jnp.any((blk_e[None, :] == jnp.arange(N_EXPERTS, dtype=I32)[:, None])
                          & (used[None, :] == 1), axis=1)
        eids = jnp.arange(N_EXPERTS, dtype=I32)
        rank = jnp.cumsum(present.astype(I32)) - 1
        run_e = jnp.sum(jnp.where(present[None, :] & (rank[None, :] == eids[:, None]),
                                  eids[None, :], 0), axis=1).astype(I32)

        dest2 = dest[:TOP_K]
        xbuf = _dispatch(blk[:n_blk, 2], dest2.reshape(TOP_K, t // DISP_TM, 1, DISP_TM), h2, n_rows)
        ybuf = _moe(run_e, run_id, used, n_runs, xbuf, w_gate[l], w_up[l], w_down[l])
        xs = _final(dest2.reshape(TOP_K, t // FIN_TM, 1, FIN_TM), x1, cw, mod3,
                    norm_f_g.reshape(1, d), ybuf, seq)
    return xs.reshape(batch, seq, d)
```

```python
import functools

import numpy as np
import jax
import jax.numpy as jnp
from jax import lax
from jax.experimental import pallas as pl
from jax.experimental.pallas import tpu as pltpu

F32 = jnp.float32
BF16 = jnp.bfloat16
I32 = jnp.int32

D_MODEL = 2048
CONV_CH = 1024
CONV_K = 3
RET_HEADS = 8
RET_DK = 128
RET_DV = 256
CHUNK = 128
ROPE_BASE = 10000.0
N_GROUPS = 4
EXPERTS_PER_GROUP = 8
N_EXPERTS = 32
EXPERT_BLOCK = 128
EPS = 1e-6
IN_COLS = 13312
LANES = 128
COL_BLK = 1024
CB_BLK, CC_BLK, CX_BLK, Q_BLK, K_BLK, V_BLK, RG_BLK, GA_BLK, GB_BLK = 0, 1, 2, 3, 4, 5, 7, 9, 11

V7X_VMEM_BYTES = 64 * 1024 * 1024
VMEM_LIMIT = V7X_VMEM_BYTES - 8 * 1024 * 1024
VMEM_LIMIT_MOE = V7X_VMEM_BYTES - 4 * 1024 * 1024


def _cparams(sem, vmem=VMEM_LIMIT):
    return pltpu.CompilerParams(dimension_semantics=sem, vmem_limit_bytes=vmem)


def _sigmoid(v):
    return 1.0 / (1.0 + jnp.exp(-v))


def _silu(v):
    return v * _sigmoid(v)


def _split_bf16(v):
    hi = v.astype(BF16)
    lo = (v - hi.astype(F32)).astype(BF16)
    return hi, lo


def _ada_kernel(c_ref, w_ref, b_ref, o_ref):
    rows = c_ref.shape[0]
    ch, cl = _split_bf16(_silu(c_ref[...]))
    wh, wl = _split_bf16(w_ref[...])
    p = jnp.dot(jnp.concatenate([ch, cl], axis=0), wh, preferred_element_type=F32)
    o_ref[...] = (p[:rows] + (p[rows:] + jnp.dot(ch, wl, preferred_element_type=F32))) + b_ref[...]


def _ada(c_pad, w, b):
    rows, d = c_pad.shape
    n = w.shape[1]
    tn = 1024
    return pl.pallas_call(
        _ada_kernel,
        grid=(n // tn,),
        in_specs=[pl.BlockSpec((rows, d), lambda j: (0, 0)),
                  pl.BlockSpec((d, tn), lambda j: (0, j)),
                  pl.BlockSpec((1, tn), lambda j: (0, j))],
        out_specs=pl.BlockSpec((rows, tn), lambda j: (0, j)),
        out_shape=jax.ShapeDtypeStruct((rows, n), F32),
        compiler_params=_cparams(("arbitrary",)),
        name="ada",
    )(c_pad, w, b)


NORM_ROWS = 128


IN_SPLIT = 2


def _in_kernel(x_ref, mod_ref, g_ref, w_ref, o_ref, h_ref):
    i, j = pl.program_id(0), pl.program_id(1)
    nj = pl.num_programs(1)
    tm, tn = o_ref.shape
    slot = i % 2

    def norm_rows(dst, rows):
        gain = g_ref[...] * (1.0 + mod_ref[0, 1:2, :])
        x = x_ref[rows, :]
        r = lax.rsqrt(jnp.mean(x * x, axis=-1, keepdims=True) + EPS)
        h_ref[dst, rows, :] = (x * r * gain + mod_ref[0, 0:1, :]).astype(BF16)

    @pl.when((i == 0) & (j == 0))
    def _():
        def body(k, _):
            norm_rows(0, pl.ds(pl.multiple_of(k * NORM_ROWS, NORM_ROWS), NORM_ROWS))
            return 0
        lax.fori_loop(0, tm // NORM_ROWS, body, 0)

    @pl.when(j < nj - 1)
    def _():
        o_ref[...] = jnp.dot(h_ref[slot], w_ref[...].astype(BF16),
                             preferred_element_type=F32).astype(o_ref.dtype)

    @pl.when(j == nj - 1)
    def _():
        h = h_ref[slot]
        cw, rw = tn // IN_SPLIT, tm // IN_SPLIT
        for q in range(IN_SPLIT):
            cols = slice(q * cw, (q + 1) * cw)
            o_ref[:, cols] = jnp.dot(h, w_ref[:, cols].astype(BF16),
                                     preferred_element_type=F32).astype(o_ref.dtype)
            for k in range(rw // NORM_ROWS):
                norm_rows(1 - slot, slice(q * rw + k * NORM_ROWS, q * rw + (k + 1) * NORM_ROWS))


def _in_proj(x2, mod3, g, w, seq):
    t, d = x2.shape
    n = w.shape[1]
    tm, tn = 1024, 1024
    nt, nj = t // tm, n // tn
    per_b = seq // tm

    def xtile(i, j):
        return jnp.minimum(i + (j == nj - 1).astype(jnp.int32), nt - 1)

    return pl.pallas_call(
        _in_kernel,
        grid=(nt, nj),
        in_specs=[pl.BlockSpec((tm, d), lambda i, j: (xtile(i, j), 0)),
                  pl.BlockSpec((1, 6, d), lambda i, j: (xtile(i, j) // per_b, 0, 0)),
                  pl.BlockSpec((1, d), lambda i, j: (0, 0)),
                  pl.BlockSpec((d, tn), lambda i, j: (0, j))],
        out_specs=pl.BlockSpec((tm, tn), lambda i, j: (i, j)),
        out_shape=jax.ShapeDtypeStruct((t, n), BF16),
        scratch_shapes=[pltpu.VMEM((2, tm, d), BF16)],
        compiler_params=_cparams(("arbitrary", "arbitrary")),
        name="in_proj",
    )(x2, mod3, g, w)


RET_CPS = 4


def _ret_kernel(q_ref, k_ref, v0_ref, v1_ref, rg0_ref, rg1_ref, cos_ref, sin_ref,
                idec_ref, qdec_ref, kdec_ref, wc_ref, wr_ref, wo_ref,
                o_ref, wcb_ref, wrb_ref, wob_ref, state_ref, *, chunk_decay):
    @pl.when(pl.program_id(1) == 0)
    def _():
        state_ref[...] = jnp.zeros_like(state_ref)

    wcb_ref[...] = wc_ref[...].astype(BF16)
    wrb_ref[...] = wr_ref[...].astype(BF16)
    wob_ref[...] = wo_ref[...].astype(BF16)

    def rot(t, cos, sin):
        return t * cos + pltpu.roll(t, RET_DK // 2, axis=1) * sin

    for ci in range(RET_CPS):
        rows = slice(ci * CHUNK, (ci + 1) * CHUNK)
        cos = cos_ref[rows, :]
        sin = sin_ref[rows, :]
        for h in range(RET_HEADS):
            v_ref, rg_ref = (v0_ref, rg0_ref) if h < RET_HEADS // 2 else (v1_ref, rg1_ref)
            vc = (h % (RET_HEADS // 2)) * RET_DV
            qh = rot(q_ref[rows, h * RET_DK:(h + 1) * RET_DK].astype(F32), cos, sin)
            kh = rot(k_ref[rows, h * RET_DK:(h + 1) * RET_DK].astype(F32), cos, sin) * (RET_DK ** -0.5)
            vh = v_ref[rows, vc:vc + RET_DV]
            qb = qh.astype(BF16)
            kb = kh.astype(BF16)
            scores = lax.dot_general(qb, kb, (((1,), (1,)), ((), ())),
                                     preferred_element_type=F32) * idec_ref[h]
            inner = jnp.dot(scores.astype(BF16), vh, preferred_element_type=F32)
            state = state_ref[h]
            cross = jnp.dot((qh * qdec_ref[h]).astype(BF16), state.astype(BF16),
                            preferred_element_type=F32)
            kv = lax.dot_general((kh * kdec_ref[h]).astype(BF16), vh, (((0,), (0,)), ((), ())),
                                 preferred_element_type=F32)
            state_ref[h] = state * chunk_decay[h] + kv
            r = inner + cross
            r = r * lax.rsqrt(jnp.mean(r * r, axis=-1, keepdims=True) + EPS)
            gate = _silu(rg_ref[rows, vc:vc + RET_DV].astype(F32))
            o_ref[rows, h * RET_DV:(h + 1) * RET_DV] = (gate * r).astype(o_ref.dtype)


def _retention(u, w_conv_out, w_ret_out, w_o, batch, seq):
    t = u.shape[0]
    f32 = np.float32
    inv = (f32(ROPE_BASE) ** (-np.arange(0, RET_DK, 2, dtype=f32) / f32(RET_DK))).astype(f32)
    ang = np.arange(seq, dtype=f32)[:, None] * inv[None, :]
    cos, sin = np.cos(ang).astype(f32), np.sin(ang).astype(f32)
    cos_t = np.concatenate([cos, cos], axis=-1)
    sin_t = np.concatenate([-sin, sin], axis=-1)
    log_g = np.log1p(-(f32(2.0) ** (f32(-5.0) - np.arange(RET_HEADS, dtype=f32)))).astype(f32)
    i = np.arange(CHUNK, dtype=f32)
    diff = i[:, None] - i[None, :]
    idec = np.where(diff >= 0, np.exp(log_g[:, None, None] * np.maximum(diff, f32(0.0))), f32(0.0)).astype(f32)
    qdec = np.broadcast_to(np.exp(log_g[:, None] * (i + f32(1.0))).astype(f32)[:, :, None],
                           (RET_HEADS, CHUNK, RET_DK))
    kdec = np.broadcast_to(np.exp(log_g[:, None] * (f32(CHUNK - 1.0) - i)).astype(f32)[:, :, None],
                           (RET_HEADS, CHUNK, RET_DK))
    chunk_decay = tuple(float(np.exp(lg * f32(CHUNK))) for lg in log_g)

    rows = RET_CPS * CHUNK
    nst = seq // rows

    def ublk(c):
        return pl.BlockSpec((rows, COL_BLK), lambda b, n, c=c: (b * nst + n, c))

    def table():
        return pl.BlockSpec((RET_HEADS, CHUNK, RET_DK), lambda b, n: (0, 0, 0))

    def slab(w):
        n_steps = batch * nst
        assert w.shape[0] % (8 * n_steps) == 0
        return pl.BlockSpec((w.shape[0] // n_steps, w.shape[1]), lambda b, n: (b * nst + n, 0))

    weights = (w_conv_out, w_ret_out, w_o)
    return pl.pallas_call(
        functools.partial(_ret_kernel, chunk_decay=chunk_decay),
        grid=(batch, nst),
        in_specs=[ublk(Q_BLK), ublk(K_BLK), ublk(V_BLK), ublk(V_BLK + 1),
                  ublk(RG_BLK), ublk(RG_BLK + 1),
                  pl.BlockSpec((rows, RET_DK), lambda b, n: (n, 0)),
                  pl.BlockSpec((rows, RET_DK), lambda b, n: (n, 0)),
                  table(), table(), table()] + [slab(w) for w in weights],
        out_specs=[pl.BlockSpec((rows, RET_HEADS * RET_DV), lambda b, n: (b * nst + n, 0))]
        + [slab(w) for w in weights],
        out_shape=[jax.ShapeDtypeStruct((t, RET_HEADS * RET_DV), BF16)]
        + [jax.ShapeDtypeStruct(w.shape, BF16) for w in weights],
        scratch_shapes=[pltpu.VMEM((RET_HEADS, RET_DK, RET_DV), F32)],
        compiler_params=_cparams(("arbitrary", "arbitrary")),
        name="retention",
    )(u, u, u, u, u, u, *(jnp.asarray(tab) for tab in (cos_t, sin_t, idec, qdec, kdec)), *weights)


HALO = 8


MIX_TN = 512


def _mix_kernel(cb_ref, cc_ref, cx_ref, hcc_ref, hcx_ref, z_ref, ga0_ref, ga1_ref, gb0_ref, gb1_ref,
                cw_ref, wc_ref, wr_ref, wo_ref, x_ref, mod_ref, modp_ref, g2_ref, wrt_ref, brt_ref,
                o_ref, hp_ref, ef_ref, rw_ref, m_ref, mg_ref, x1p_ref, *, tiles_per_seq, n_tiles):
    tm = cb_ref.shape[0]
    d = o_ref.shape[1]
    step = pl.program_id(0)
    slot = step % 2
    prev = 1 - slot
    nchunk = d // MIX_TN

    @pl.when(step == 0)
    def _():
        x1p_ref[1] = jnp.zeros(x1p_ref.shape[1:], F32)

    def body(with_tile):
        if with_tile:
            m = cc_ref[...].astype(F32) * cx_ref[...].astype(F32)
            halo = hcc_ref[...].astype(F32) * hcx_ref[...].astype(F32)
            first = (step % tiles_per_seq) == 0
            m_ref[0:HALO, :] = jnp.where(first, 0.0, halo)
            m_ref[HALO:HALO + tm, :] = m
            conv = (m_ref[HALO - 2:HALO - 2 + tm, :] * cw_ref[0:1, :]
                    + m_ref[HALO - 1:HALO - 1 + tm, :] * cw_ref[1:2, :]
                    + m * cw_ref[2:3, :])
            p = (cb_ref[...].astype(F32) * conv).astype(BF16)
            z = z_ref[...]
        ssq = jnp.zeros((tm, 1), F32)
        for n in range(nchunk):
            cols = slice(n * MIX_TN, (n + 1) * MIX_TN)
            if with_tile:
                ga_ref, gb_ref = (ga0_ref, gb0_ref) if n < COL_BLK // MIX_TN else (ga1_ref, gb1_ref)
                gcols = slice((n * MIX_TN) % COL_BLK, (n * MIX_TN) % COL_BLK + MIX_TN)
                ya = jnp.dot(p, wc_ref[:, cols], preferred_element_type=F32)
                yb = jnp.dot(z, wr_ref[:, cols], preferred_element_type=F32)
                merged = (_sigmoid(ga_ref[:, gcols].astype(F32)) * ya
                          + _sigmoid(gb_ref[:, gcols].astype(F32)) * yb)
                mg_ref[:, cols] = merged.astype(BF16)
            xp = x1p_ref[prev, :, cols]
            ssq = ssq + jnp.sum(xp * xp, axis=-1, keepdims=True)
        rnorm = lax.rsqrt(ssq * (1.0 / d) + EPS)
        gain = g2_ref[...] * (1.0 + modp_ref[0, 4:5, :])
        shift = modp_ref[0, 3:4, :]
        wh, wl = _split_bf16(wrt_ref[...])
        whl = jnp.concatenate([wh, wl], axis=1)
        acc_hw = jnp.zeros((tm, 2 * LANES), F32)
        acc_lw = jnp.zeros((tm, LANES), F32)
        if with_tile:
            mg = mg_ref[...]
        for n in range(nchunk):
            cols = slice(n * MIX_TN, (n + 1) * MIX_TN)
            if with_tile:
                acc = jnp.dot(mg, wo_ref[:, cols], preferred_element_type=F32)
                x1 = x_ref[:, cols] + mod_ref[0, 2:3, cols] * acc
                o_ref[:, cols] = x1
                x1p_ref[slot, :, cols] = x1
            if n % 2 == 0:
                c = (n // 2) * MIX_TN
                halves = []
                for c0 in (c, c + HALF):
                    hcols = slice(c0, c0 + MIX_TN)
                    h2 = x1p_ref[prev, :, hcols] * rnorm * gain[:, hcols] + shift[:, hcols]
                    hh, hl = _split_bf16(h2)
                    acc_hw = acc_hw + jnp.dot(hh, whl[hcols, :], preferred_element_type=F32)
                    acc_lw = acc_lw + jnp.dot(hl, wh[hcols, :], preferred_element_type=F32)
                    halves.append(h2)
                hp_ref[:, c:c + MIX_TN] = _pack_pair(halves[0], halves[1])
        logits = (acc_hw[:, :LANES] + (acc_hw[:, LANES:] + acc_lw)) + brt_ref[...]
        ef_ref[...], rw_ref[...] = _route_from_logits(logits)

    pl.when(step < n_tiles)(functools.partial(body, True))
    pl.when(step == n_tiles)(functools.partial(body, False))


def _mix(u, z, conv_w, wc_bf, wr_bf, wo_bf, x2, mod3, g2, w_rt, b_rt, seq):
    t, d = x2.shape
    tm = 256
    nt = t // tm
    hb = tm // HALO
    per_b = seq // tm

    def cur(i):
        return jnp.minimum(i, nt - 1)

    def prev(i):
        return jnp.maximum(i - 1, 0)

    def ublk(c):
        return pl.BlockSpec((tm, COL_BLK), lambda i, c=c: (cur(i), c))

    def halo(c):
        return pl.BlockSpec((HALO, COL_BLK), lambda i, c=c: (jnp.maximum(cur(i) * hb - 1, 0), c))

    def resident(shape):
        return pl.BlockSpec(shape, lambda i: (0, 0), pipeline_mode=pl.Buffered(1))

    return pl.pallas_call(
        functools.partial(_mix_kernel, tiles_per_seq=per_b, n_tiles=nt),
        grid=(nt + 1,),
        in_specs=[ublk(CB_BLK), ublk(CC_BLK), ublk(CX_BLK), halo(CC_BLK), halo(CX_BLK),
                  pl.BlockSpec((tm, z.shape[1]), lambda i: (cur(i), 0)),
                  ublk(GA_BLK), ublk(GA_BLK + 1), ublk(GB_BLK), ublk(GB_BLK + 1),
                  pl.BlockSpec((CONV_K, CONV_CH), lambda i: (0, 0)),
                  resident(wc_bf.shape), resident(wr_bf.shape), resident(wo_bf.shape),
                  pl.BlockSpec((tm, d), lambda i: (cur(i), 0)),
                  pl.BlockSpec((1, 6, d), lambda i: (cur(i) // per_b, 0, 0)),
                  pl.BlockSpec((1, 6, d), lambda i: (prev(i) // per_b, 0, 0)),
                  pl.BlockSpec((1, d), lambda i: (0, 0)),
                  pl.BlockSpec((d, LANES), lambda i: (0, 0)),
                  pl.BlockSpec((1, LANES), lambda i: (0, 0))],
        out_specs=[pl.BlockSpec((tm, d), lambda i: (cur(i), 0)),
                   pl.BlockSpec((tm, HALF), lambda i: (prev(i), 0)),
                   pl.BlockSpec((tm, LANES), lambda i: (prev(i), 0)),
                   pl.BlockSpec((tm, LANES), lambda i: (prev(i), 0))],
        out_shape=[jax.ShapeDtypeStruct((t, d), F32),
                   jax.ShapeDtypeStruct((t, HALF), U32),
                   jax.ShapeDtypeStruct((t, LANES), F32),
                   jax.ShapeDtypeStruct((t, LANES), F32)],
        scratch_shapes=[pltpu.VMEM((tm + HALO, CONV_CH), F32),
                        pltpu.VMEM((tm, d), BF16),
                        pltpu.VMEM((2, tm, d), F32)],
        compiler_params=_cparams(("arbitrary",)),
        name="mix",
    )(u, u, u, u, u, z, u, u, u, u, conv_w, wc_bf, wr_bf, wo_bf, x2, mod3, mod3, g2, w_rt, b_rt)


ROUTE_E0 = N_GROUPS


U32 = jnp.uint32
HALF = D_MODEL // 2


def _pack_pair(lo, hi):
    ulo = lax.bitcast_convert_type(lo.astype(BF16).astype(F32), U32)
    uhi = lax.bitcast_convert_type(hi.astype(BF16).astype(F32), U32)
    return (ulo >> 16) | uhi


def _unpack_pair(w):
    lo = lax.bitcast_convert_type(w << 16, F32)
    hi = lax.bitcast_convert_type(w & jnp.uint32(0xFFFF0000), F32)
    return lo, hi


ROW_SUB = HALF // LANES


def _row_slab(r):
    if isinstance(r, int):
        return pl.ds(r * ROW_SUB, ROW_SUB)
    return pl.ds(pl.multiple_of(r * ROW_SUB, ROW_SUB), ROW_SUB)


def _store_rows(ref, row0, v):
    for c in range(ROW_SUB):
        ref[pl.ds(row0 * ROW_SUB + c, v.shape[0], stride=ROW_SUB), :] = v[:, c * LANES:(c + 1) * LANES]


def _load_rows(ref, row0, n):
    return jnp.concatenate(
        [ref[pl.ds(row0 * ROW_SUB + c, n, stride=ROW_SUB), :] for c in range(ROW_SUB)], axis=1)


def _route_from_logits(logits):
    lane = lax.broadcasted_iota(I32, logits.shape, 1).astype(F32)
    neg = -jnp.inf
    big = float(LANES)
    gl = jnp.where(lane < N_GROUPS, logits, neg)
    gmax = jnp.max(gl, axis=-1, keepdims=True)
    gsel = jnp.min(jnp.where(gl == gmax, lane, big), axis=-1, keepdims=True)
    gsum = jnp.sum(jnp.exp(gl - gmax), axis=-1, keepdims=True)
    g_w = 1.0 / gsum
    lo_lane = ROUTE_E0 + EXPERTS_PER_GROUP * gsel
    emask = (lane >= lo_lane) & (lane < lo_lane + EXPERTS_PER_GROUP)
    el = jnp.where(emask, logits, neg)
    m1 = jnp.max(el, axis=-1, keepdims=True)
    i1 = jnp.min(jnp.where(el == m1, lane, big), axis=-1, keepdims=True)
    el2 = jnp.where(lane == i1, neg, el)
    m2 = jnp.max(el2, axis=-1, keepdims=True)
    i2 = jnp.min(jnp.where(el2 == m2, lane, big), axis=-1, keepdims=True)
    esum = jnp.sum(jnp.exp(el - m1), axis=-1, keepdims=True)
    p1 = 1.0 / esum
    p2 = jnp.exp(m2 - m1) / esum
    tot = p1 + p2
    c1 = g_w * (p1 / tot)
    c2 = g_w * (p2 / tot)
    ef = jnp.where(lane == 0.0, i1 - ROUTE_E0, jnp.where(lane == 1.0, i2 - ROUTE_E0, 0.0))
    cw = jnp.where(lane == 0.0, c1, jnp.where(lane == 1.0, c2, 0.0))
    return ef, cw


META_TB = 512
DEST_ROWS = 8
BLK_ROWS = 256


def _meta_kernel(ef_ref, dest_ref, blk_ref, pref_ref):
    t = ef_ref.shape[0]
    nb = t // META_TB
    lane = lax.broadcasted_iota(I32, (META_TB, LANES), 1).astype(F32)
    r_i = lax.broadcasted_iota(I32, (META_TB, META_TB), 0)
    c_i = lax.broadcasted_iota(I32, (META_TB, META_TB), 1)
    tril = jnp.where(r_i > c_i, 1.0, 0.0).astype(BF16)

    def onehots(i):
        ef = ef_ref[pl.ds(i * META_TB, META_TB), :]
        oh1 = jnp.where(lane == ef[:, 0:1], 1.0, 0.0)
        oh2 = jnp.where(lane == ef[:, 1:2], 1.0, 0.0)
        return oh1, oh2

    def pass1(i, carry):
        oh1, oh2 = onehots(i)
        cnt = oh1 + oh2
        pref = jnp.dot(tril, cnt.astype(BF16), preferred_element_type=F32) + carry
        pref_ref[pl.ds(i * META_TB, META_TB), :] = pref
        return carry + jnp.sum(cnt, axis=0, keepdims=True)

    counts = lax.fori_loop(0, nb, pass1, jnp.zeros((1, LANES), F32))
    nblk = jnp.floor((counts + (EXPERT_BLOCK - 1)) * (1.0 / EXPERT_BLOCK))
    u_r = lax.broadcasted_iota(I32, (LANES, LANES), 0)
    u_c = lax.broadcasted_iota(I32, (LANES, LANES), 1)
    upper = jnp.where(u_r <= u_c, 1.0, 0.0).astype(BF16)
    nblk8 = jnp.broadcast_to(nblk, (8, LANES))
    pend_blk = jnp.dot(nblk8.astype(BF16), upper, preferred_element_type=F32)[0:1, :]
    pstart_blk = pend_blk - nblk
    pstart = pstart_blk * float(EXPERT_BLOCK)

    for i in range(nb):
        oh1, oh2 = onehots(i)
        base = pref_ref[pl.ds(i * META_TB, META_TB), :] + pstart
        d1 = jnp.sum(oh1 * base, axis=-1, keepdims=True)
        d2 = jnp.sum(oh2 * base, axis=-1, keepdims=True)
        dest = jnp.where(lane == 0.0, d1, jnp.where(lane == 1.0, d2, 0.0))
        dest_ref[:, i * META_TB:(i + 1) * META_TB] = dest.T[:DEST_ROWS, :].astype(I32)

    blane = lax.broadcasted_iota(I32, (BLK_ROWS, LANES), 1).astype(F32)
    brow = lax.broadcasted_iota(I32, (BLK_ROWS, LANES), 0).astype(F32)
    emask = blane < N_EXPERTS
    owner = jnp.sum(jnp.where(emask & (pend_blk <= brow), 1.0, 0.0), axis=-1, keepdims=True)
    owner = jnp.minimum(owner, float(N_EXPERTS - 1))
    total = jnp.sum(jnp.where(emask, nblk, 0.0), axis=-1, keepdims=True)
    used = jnp.where(brow < total, 1.0, 0.0)
    mine = emask & (blane == owner)
    cnt_b = jnp.sum(jnp.where(mine, counts, 0.0), axis=-1, keepdims=True)
    ps_b = jnp.sum(jnp.where(mine, pstart_blk, 0.0), axis=-1, keepdims=True)
    nvalid = jnp.clip(cnt_b - (brow - ps_b) * float(EXPERT_BLOCK), 0.0, float(EXPERT_BLOCK))
    pad_lo = nvalid * used
    blk_ref[...] = jnp.where(blane == 0.0, owner,
                             jnp.where(blane == 1.0, used,
                                       jnp.where(blane == 2.0, pad_lo, 0.0))).astype(I32)


def _meta(ef):
    t = ef.shape[0]
    return pl.pallas_call(
        _meta_kernel,
        out_shape=[jax.ShapeDtypeStruct((DEST_ROWS, t), I32),
                   jax.ShapeDtypeStruct((BLK_ROWS, LANES), I32)],
        scratch_shapes=[pltpu.VMEM((t, LANES), F32)],
        compiler_params=pltpu.CompilerParams(vmem_limit_bytes=VMEM_LIMIT),
        name="meta",
    )(ef)


DISP_TM = 1024
TOP_K = 2


def _dispatch_kernel(pad_lo_ref, dest_ref, h_ref, xbuf_ref, stage, zrow, sems, zsem):
    i = pl.program_id(0)
    n = pl.num_programs(0)
    tm = h_ref.shape[0]
    n_blk = pad_lo_ref.shape[0]
    slot = i % 2

    def pad_rows(action):
        def blk(b, _):
            lo = pad_lo_ref[b]
            base = pl.multiple_of(b * EXPERT_BLOCK, EXPERT_BLOCK)

            @pl.when(lo == 0)
            def _():
                whole = pl.ds(pl.multiple_of(base * ROW_SUB, EXPERT_BLOCK * ROW_SUB),
                              EXPERT_BLOCK * ROW_SUB)
                action(pltpu.make_async_copy(zrow, xbuf_ref.at[whole], zsem))

            @pl.when(lo > 0)
            def _():
                def row(r, _):
                    action(pltpu.make_async_copy(zrow.at[_row_slab(0)],
                                                 xbuf_ref.at[_row_slab(base + r)], zsem))
                    return 0
                lax.fori_loop(lo, EXPERT_BLOCK, row, 0)
            return 0
        lax.fori_loop(0, n_blk, blk, 0)

    @pl.when(i == 0)
    def _():
        zrow[...] = jnp.zeros_like(zrow)
        pad_rows(lambda cp: cp.start())

    _store_rows(stage.at[slot], 0, h_ref[...])
    for j in range(tm):
        for k in range(TOP_K):
            d = dest_ref[k, 0, 0, j]
            pltpu.make_async_copy(stage.at[slot, _row_slab(j)], xbuf_ref.at[_row_slab(d)],
                                  sems.at[slot]).start(priority=k)

    def wait_tile(s):
        for _ in range(TOP_K):
            pltpu.make_async_copy(stage.at[s], xbuf_ref.at[pl.ds(0, tm * ROW_SUB)], sems.at[s]).wait()

    @pl.when(i > 0)
    def _():
        wait_tile(1 - slot)

    @pl.when(i == n - 1)
    def _():
        wait_tile(slot)
        pad_rows(lambda cp: cp.wait())


def _dispatch(pad_lo, dest3, hp, n_rows):
    t, w = hp.shape
    tm = DISP_TM
    grid_spec = pltpu.PrefetchScalarGridSpec(
        num_scalar_prefetch=1,
        grid=(t // tm,),
        in_specs=[pl.BlockSpec((TOP_K, 1, 1, tm), lambda i, *_: (0, i, 0, 0), memory_space=pltpu.SMEM),
                  pl.BlockSpec((tm, w), lambda i, *_: (i, 0))],
        out_specs=pl.BlockSpec(memory_space=pl.ANY),
        scratch_shapes=[pltpu.VMEM((2, tm * ROW_SUB, LANES), U32),
                        pltpu.VMEM((EXPERT_BLOCK * ROW_SUB, LANES), U32),
                        pltpu.SemaphoreType.DMA((2,)), pltpu.SemaphoreType.DMA(())],
    )
    return pl.pallas_call(
        _dispatch_kernel,
        grid_spec=grid_spec,
        out_shape=jax.ShapeDtypeStruct((n_rows * ROW_SUB, LANES), U32),
        compiler_params=_cparams(("arbitrary",)),
        name="dispatch",
    )(pad_lo, dest3, hp)


MOE_SUB = 4
W_RING = 2
CAST_ROWS = 256
SWITCH_AFTER = 4


def _moe_kernel(run_e_ref, run_id_ref, used_ref, nruns_ref, x_ref, wg_hbm, wu_hbm, wd_hbm, o_ref,
                wg_st, wu_st, wd_st, wg_bf, wu_bf, wd_bf, sems, state):
    step = pl.program_id(0)
    n_runs = nruns_ref[0]
    streams = ((wg_hbm, wg_st, wg_bf), (wu_hbm, wu_st, wu_bf), (wd_hbm, wd_st, wd_bf))

    def weight_copy(m, run):
        hbm, stage, _ = streams[m]
        return pltpu.make_async_copy(hbm.at[run_e_ref[run]], stage, sems.at[m])

    @pl.when(step == 0)
    def _():
        state[0] = 0
        state[1] = 0
        for m in range(len(streams)):
            weight_copy(m, 0).start(priority=1)

    def switch_to(run, b):
        ring = lax.rem(run, W_RING)
        for m, (_, stage, bf) in enumerate(streams):
            weight_copy(m, run).wait()

            def body(i, _, stage=stage, bf=bf):
                rows = pl.ds(pl.multiple_of(i * CAST_ROWS, CAST_ROWS), CAST_ROWS)
                bf[ring, rows, :] = stage[rows, :].astype(BF16)
                return 0
            lax.fori_loop(0, stage.shape[0] // CAST_ROWS, body, 0)

            @pl.when(run + 1 < n_runs)
            def _(m=m):
                weight_copy(m, run + 1).start(priority=1)
        state[0] = run + 1
        state[1] = b + SWITCH_AFTER

    def run_block(b, row0):
        run = run_id_ref[b]
        pl.when(state[0] == run)(functools.partial(switch_to, run, b))
        pl.when((state[0] == run + 1) & (run + 1 < n_runs) & (b >= state[1]))(
            functools.partial(switch_to, run + 1, b))

        ring = lax.rem(run, W_RING)
        xlo, xhi = _unpack_pair(_load_rows(x_ref, row0, EXPERT_BLOCK))
        xb = jnp.concatenate([xlo.astype(BF16), xhi.astype(BF16)], axis=1)
        g = jnp.dot(xb, wg_bf[ring], preferred_element_type=F32)
        u = jnp.dot(xb, wu_bf[ring], preferred_element_type=F32)
        hmid = (_silu(g) * u).astype(BF16)
        y = jnp.dot(hmid, wd_bf[ring], preferred_element_type=F32)
        _store_rows(o_ref, row0, _pack_pair(y[:, :HALF], y[:, HALF:]))

    for j in range(MOE_SUB):
        b = step * MOE_SUB + j
        row0 = j * EXPERT_BLOCK
        pl.when(used_ref[b] == 1)(functools.partial(run_block, b, row0))

        @pl.when(used_ref[b] == 0)
        def _():
            o_ref[pl.ds(row0 * ROW_SUB, EXPERT_BLOCK * ROW_SUB), :] = jnp.zeros(
                (EXPERT_BLOCK * ROW_SUB, LANES), U32)


def _moe(run_e, run_id, used, n_runs, xbuf, w_gate, w_up, w_down):
    r = xbuf.shape[0]
    rows = MOE_SUB * EXPERT_BLOCK * ROW_SUB
    d, de = w_gate.shape[1:]

    def xmap(s, run_e_ref, run_id_ref, used_ref, nruns_ref):
        del run_e_ref, run_id_ref, nruns_ref
        return (jnp.where(used_ref[s * MOE_SUB] == 1, s, 0), 0)

    grid_spec = pltpu.PrefetchScalarGridSpec(
        num_scalar_prefetch=4,
        grid=(r // rows,),
        in_specs=[pl.BlockSpec((rows, LANES), xmap),
                  pl.BlockSpec(memory_space=pl.ANY),
                  pl.BlockSpec(memory_space=pl.ANY),
                  pl.BlockSpec(memory_space=pl.ANY)],
        out_specs=pl.BlockSpec((rows, LANES), lambda s, *_: (s, 0)),
        scratch_shapes=[pltpu.VMEM((d, de), F32), pltpu.VMEM((d, de), F32), pltpu.VMEM((de, d), F32),
                        pltpu.VMEM((W_RING, d, de), BF16), pltpu.VMEM((W_RING, d, de), BF16),
                        pltpu.VMEM((W_RING, de, d), BF16),
                        pltpu.SemaphoreType.DMA((3,)), pltpu.SMEM((2,), I32)],
    )
    return pl.pallas_call(
        _moe_kernel,
        grid_spec=grid_spec,
        out_shape=jax.ShapeDtypeStruct((r, LANES), U32),
        compiler_params=_cparams(("arbitrary",), vmem=VMEM_LIMIT_MOE),
        name="moe",
    )(run_e, run_id, used, n_runs, xbuf, w_gate, w_up, w_down)


FIN_TM = 256


def _final_kernel(dest_ref, dnext_ref, x_ref, cw_ref, mod_ref, g_ref, y_hbm, o_ref, ybuf, sems):
    i = pl.program_id(0)
    n = pl.num_programs(0)
    tm = x_ref.shape[0]
    slot = i % 2

    def gather(d_ref, s):
        for j in range(tm):
            for k in range(TOP_K):
                d = d_ref[k, 0, 0, j]
                pltpu.make_async_copy(y_hbm.at[_row_slab(d)], ybuf.at[s, k, _row_slab(j)],
                                      sems.at[s]).start(priority=k)

    @pl.when(i == 0)
    def _():
        gather(dest_ref, 0)

    @pl.when(i + 1 < n)
    def _():
        gather(dnext_ref, 1 - slot)

    for k in range(TOP_K):
        pltpu.make_async_copy(y_hbm.at[pl.ds(0, tm * ROW_SUB)], ybuf.at[slot, k], sems.at[slot]).wait()

    cw = cw_ref[...]
    c0, c1 = cw[:, 0:1], cw[:, 1:2]
    lo0, hi0 = _unpack_pair(_load_rows(ybuf.at[slot, 0], 0, tm))
    lo1, hi1 = _unpack_pair(_load_rows(ybuf.at[slot, 1], 0, tm))
    g2 = mod_ref[0, 5:6, :]
    xlo = x_ref[:, :HALF] + g2[:, :HALF] * (lo0 * c0 + lo1 * c1)
    xhi = x_ref[:, HALF:] + g2[:, HALF:] * (hi0 * c0 + hi1 * c1)
    ms = (jnp.sum(xlo * xlo, axis=-1, keepdims=True)
          + jnp.sum(xhi * xhi, axis=-1, keepdims=True)) * (1.0 / D_MODEL)
    r = lax.rsqrt(ms + EPS)
    o_ref[:, :HALF] = xlo * r * g_ref[:, :HALF]
    o_ref[:, HALF:] = xhi * r * g_ref[:, HALF:]


def _final(dest3, x1, cw, mod3, g, ybuf, seq):
    t, d = x1.shape
    tm = FIN_TM
    nt = t // tm
    per_b = seq // tm

    def dspec(shift):
        return pl.BlockSpec((TOP_K, 1, 1, tm), lambda i: (0, jnp.minimum(i + shift, nt - 1), 0, 0),
                            memory_space=pltpu.SMEM)

    return pl.pallas_call(
        _final_kernel,
        grid=(nt,),
        in_specs=[dspec(0), dspec(1),
                  pl.BlockSpec((tm, d), lambda i: (i, 0)),
                  pl.BlockSpec((tm, LANES), lambda i: (i, 0)),
                  pl.BlockSpec((1, 6, d), lambda i: (i // per_b, 0, 0)),
                  pl.BlockSpec((1, d), lambda i: (0, 0)),
                  pl.BlockSpec(memory_space=pl.ANY)],
        out_specs=pl.BlockSpec((tm, d), lambda i: (i, 0)),
        out_shape=jax.ShapeDtypeStruct((t, d), F32),
        scratch_shapes=[pltpu.VMEM((2, TOP_K, tm * ROW_SUB, LANES), U32),
                        pltpu.SemaphoreType.DMA((2,))],
        compiler_params=_cparams(("arbitrary",)),
        name="final",
    )(dest3, dest3, x1, cw, mod3, g, ybuf)


def kernel(x, c, w_ada, b_ada, norm1_g, w_in, conv_w, w_conv_out, w_ret_out, w_o, norm2_g,
           w_router_group, b_router_group, w_router_expert, b_router_expert,
           w_gate, w_up, w_down, norm_f_g):
    batch, seq, d = x.shape
    t = batch * seq
    depth = w_ada.shape[0]
    assert d == D_MODEL and w_in.shape[2] == IN_COLS and w_gate.shape[1] == N_EXPERTS
    assert depth == 1, "the final kernel fuses the last rmsnorm into the single layer"
    n_rows = t * 2 + N_EXPERTS * EXPERT_BLOCK
    n_blk = n_rows // EXPERT_BLOCK
    assert n_blk <= BLK_ROWS

    c_pad = jnp.pad(c, ((0, 8 - batch), (0, 0)))
    xs = x.reshape(t, d)
    for l in range(depth):
        mod = _ada(c_pad, w_ada[l], b_ada[l].reshape(1, -1))[:batch]
        mod3 = mod.reshape(batch, 6, d)

        u = _in_proj(xs, mod3, norm1_g[l].reshape(1, d), w_in[l], seq)
        z, wc_bf, wr_bf, wo_bf = _retention(u, w_conv_out[l], w_ret_out[l], w_o[l], batch, seq)
        w_r = jnp.pad(jnp.concatenate([w_router_group[l], w_router_expert[l]], axis=1),
                      ((0, 0), (0, LANES - N_GROUPS - N_EXPERTS)))
        b_r = jnp.pad(jnp.concatenate([b_router_group[l], b_router_expert[l]]),
                      (0, LANES - N_GROUPS - N_EXPERTS)).reshape(1, LANES)
        x1, h2, ef, cw = _mix(u, z, conv_w[l], wc_bf, wr_bf, wo_bf, xs, mod3,
                              norm2_g[l].reshape(1, d), w_r, b_r, seq)

        dest, blk = _meta(ef)
        blk_e = blk[:n_blk, 0]
        used = blk[:n_blk, 1]
        prev_e = jnp.concatenate([jnp.full((1,), -1, I32), blk_e[:-1]])
        first = ((blk_e != prev_e) & (used == 1)).astype(I32)
        run_id = jnp.maximum(jnp.cumsum(first) - 1, 0).astype(I32)
        n_runs = jnp.sum(first).astype(I32).reshape(1)
        present = jnp.any((blk_e[None, :] == jnp.arange(N_EXPERTS, dtype=I32)[:, None])
                          & (used[None, :] == 1), axis=1)
        eids = jnp.arange(N_EXPERTS, dtype=I32)
        rank = jnp.cumsum(present.astype(I32)) - 1
        run_e = jnp.sum(jnp.where(present[None, :] & (rank[None, :] == eids[:, None]),
                                  eids[None, :], 0), axis=1).astype(I32)

        dest2 = dest[:TOP_K]
        xbuf = _dispatch(blk[:n_blk, 2], dest2.reshape(TOP_K, t // DISP_TM, 1, DISP_TM), h2, n_rows)
        ybuf = _moe(run_e, run_id, used, n_runs, xbuf, w_gate[l], w_up[l], w_down[l])
        xs = _final(dest2.reshape(TOP_K, t // FIN_TM, 1, FIN_TM), x1, cw, mod3,
                    norm_f_g.reshape(1, d), ybuf, seq)
    return xs.reshape(batch, seq, d)
```

```python
import functools

import numpy as np
import jax
import jax.numpy as jnp
from jax import lax
from jax.experimental import pallas as pl
from jax.experimental.pallas import tpu as pltpu

F32 = jnp.float32
BF16 = jnp.bfloat16
I32 = jnp.int32

D_MODEL = 2048
CONV_CH = 1024
CONV_K = 3
RET_HEADS = 8
RET_DK = 128
RET_DV = 256
CHUNK = 128
ROPE_BASE = 10000.0
N_GROUPS = 4
EXPERTS_PER_GROUP = 8
N_EXPERTS = 32
EXPERT_BLOCK = 128
EPS = 1e-6
IN_COLS = 13312
LANES = 128
COL_BLK = 1024
CB_BLK, CC_BLK, CX_BLK, Q_BLK, K_BLK, V_BLK, RG_BLK, GA_BLK, GB_BLK = 0, 1, 2, 3, 4, 5, 7, 9, 11

V7X_VMEM_BYTES = 64 * 1024 * 1024
VMEM_LIMIT = V7X_VMEM_BYTES - 8 * 1024 * 1024
VMEM_LIMIT_MOE = V7X_VMEM_BYTES - 4 * 1024 * 1024


def _cparams(sem, vmem=VMEM_LIMIT):
    return pltpu.CompilerParams(dimension_semantics=sem, vmem_limit_bytes=vmem)


def _sigmoid(v):
    return 1.0 / (1.0 + jnp.exp(-v))


def _silu(v):
    return v * _sigmoid(v)


def _split_bf16(v):
    hi = v.astype(BF16)
    lo = (v - hi.astype(F32)).astype(BF16)
    return hi, lo


def _ada_kernel(c_ref, w_ref, b_ref, o_ref):
    rows = c_ref.shape[0]
    ch, cl = _split_bf16(_silu(c_ref[...]))
    wh, wl = _split_bf16(w_ref[...])
    p = jnp.dot(jnp.concatenate([ch, cl], axis=0), wh, preferred_element_type=F32)
    o_ref[...] = (p[:rows] + (p[rows:] + jnp.dot(ch, wl, preferred_element_type=F32))) + b_ref[...]


def _ada(c_pad, w, b):
    rows, d = c_pad.shape
    n = w.shape[1]
    tn = 1024
    return pl.pallas_call(
        _ada_kernel,
        grid=(n // tn,),
        in_specs=[pl.BlockSpec((rows, d), lambda j: (0, 0)),
                  pl.BlockSpec((d, tn), lambda j: (0, j)),
                  pl.BlockSpec((1, tn), lambda j: (0, j))],
        out_specs=pl.BlockSpec((rows, tn), lambda j: (0, j)),
        out_shape=jax.ShapeDtypeStruct((rows, n), F32),
        compiler_params=_cparams(("arbitrary",)),
        name="ada",
    )(c_pad, w, b)


NORM_ROWS = 128


IN_SPLIT = 8


def _in_kernel(x_ref, mod_ref, g_ref, w_ref, o_ref, h_ref):
    i, j = pl.program_id(0), pl.program_id(1)
    nj = pl.num_programs(1)
    tm, tn = o_ref.shape
    slot = i % 2

    def norm_rows(dst, rows):
        gain = g_ref[...] * (1.0 + mod_ref[0, 1:2, :])
        x = x_ref[rows, :]
        r = lax.rsqrt(jnp.mean(x * x, axis=-1, keepdims=True) + EPS)
        h_ref[dst, rows, :] = (x * r * gain + mod_ref[0, 0:1, :]).astype(BF16)

    @pl.when((i == 0) & (j == 0))
    def _():
        def body(k, _):
            norm_rows(0, pl.ds(pl.multiple_of(k * NORM_ROWS, NORM_ROWS), NORM_ROWS))
            return 0
        lax.fori_loop(0, tm // NORM_ROWS, body, 0)

    @pl.when(j < nj - 1)
    def _():
        o_ref[...] = jnp.dot(h_ref[slot], w_ref[...].astype(BF16),
                             preferred_element_type=F32).astype(o_ref.dtype)

    @pl.when(j == nj - 1)
    def _():
        wb = w_ref[...].astype(BF16)
        rw = tm // IN_SPLIT
        edge = None
        for q in range(IN_SPLIT):
            rows = slice(q * rw, (q + 1) * rw)
            lhs = h_ref[slot, rows, :]
            if edge is not None:
                lhs = jnp.concatenate([lhs[:, :LANES] + edge, lhs[:, LANES:]], axis=1)
            o_ref[rows, :] = jnp.dot(lhs, wb, preferred_element_type=F32).astype(o_ref.dtype)
            for k in range(rw // NORM_ROWS):
                norm_rows(1 - slot, slice(q * rw + k * NORM_ROWS, q * rw + (k + 1) * NORM_ROWS))
            bits = pltpu.bitcast(h_ref[1 - slot, rows, 0:LANES], jnp.uint32)
            edge = pltpu.bitcast((bits >> 16) >> 16, BF16)


def _in_proj(x2, mod3, g, w, seq):
    t, d = x2.shape
    n = w.shape[1]
    tm, tn = 1024, 1024
    nt, nj = t // tm, n // tn
    per_b = seq // tm

    def xtile(i, j):
        return jnp.minimum(i + (j == nj - 1).astype(jnp.int32), nt - 1)

    return pl.pallas_call(
        _in_kernel,
        grid=(nt, nj),
        in_specs=[pl.BlockSpec((tm, d), lambda i, j: (xtile(i, j), 0)),
                  pl.BlockSpec((1, 6, d), lambda i, j: (xtile(i, j) // per_b, 0, 0)),
                  pl.BlockSpec((1, d), lambda i, j: (0, 0)),
                  pl.BlockSpec((d, tn), lambda i, j: (0, j))],
        out_specs=pl.BlockSpec((tm, tn), lambda i, j: (i, j)),
        out_shape=jax.ShapeDtypeStruct((t, n), BF16),
        scratch_shapes=[pltpu.VMEM((2, tm, d), BF16)],
        compiler_params=_cparams(("arbitrary", "arbitrary")),
        name="in_proj",
    )(x2, mod3, g, w)


RET_CPS = 4


def _ret_kernel(q_ref, k_ref, v0_ref, v1_ref, rg0_ref, rg1_ref, cos_ref, sin_ref,
                idec_ref, qdec_ref, kdec_ref, wc_ref, wr_ref, wo_ref,
                o_ref, wcb_ref, wrb_ref, wob_ref, state_ref, *, chunk_decay):
    @pl.when(pl.program_id(1) == 0)
    def _():
        state_ref[...] = jnp.zeros_like(state_ref)

    wcb_ref[...] = wc_ref[...].astype(BF16)
    wrb_ref[...] = wr_ref[...].astype(BF16)
    wob_ref[...] = wo_ref[...].astype(BF16)

    def rot(t, cos, sin):
        return t * cos + pltpu.roll(t, RET_DK // 2, axis=1) * sin

    for ci in range(RET_CPS):
        rows = slice(ci * CHUNK, (ci + 1) * CHUNK)
        cos = cos_ref[rows, :]
        sin = sin_ref[rows, :]
        for h in range(RET_HEADS):
            v_ref, rg_ref = (v0_ref, rg0_ref) if h < RET_HEADS // 2 else (v1_ref, rg1_ref)
            vc = (h % (RET_HEADS // 2)) * RET_DV
            qh = rot(q_ref[rows, h * RET_DK:(h + 1) * RET_DK].astype(F32), cos, sin)
            kh = rot(k_ref[rows, h * RET_DK:(h + 1) * RET_DK].astype(F32), cos, sin) * (RET_DK ** -0.5)
            vh = v_ref[rows, vc:vc + RET_DV]
            qb = qh.astype(BF16)
            kb = kh.astype(BF16)
            scores = lax.dot_general(qb, kb, (((1,), (1,)), ((), ())),
                                     preferred_element_type=F32) * idec_ref[h]
            inner = jnp.dot(scores.astype(BF16), vh, preferred_element_type=F32)
            state = state_ref[h]
            cross = jnp.dot((qh * qdec_ref[h]).astype(BF16), state.astype(BF16),
                            preferred_element_type=F32)
            kv = lax.dot_general((kh * kdec_ref[h]).astype(BF16), vh, (((0,), (0,)), ((), ())),
                                 preferred_element_type=F32)
            state_ref[h] = state * chunk_decay[h] + kv
            r = inner + cross
            r = r * lax.rsqrt(jnp.mean(r * r, axis=-1, keepdims=True) + EPS)
            gate = _silu(rg_ref[rows, vc:vc + RET_DV].astype(F32))
            o_ref[rows, h * RET_DV:(h + 1) * RET_DV] = (gate * r).astype(o_ref.dtype)


def _retention(u, w_conv_out, w_ret_out, w_o, batch, seq):
    t = u.shape[0]
    f32 = np.float32
    inv = (f32(ROPE_BASE) ** (-np.arange(0, RET_DK, 2, dtype=f32) / f32(RET_DK))).astype(f32)
    ang = np.arange(seq, dtype=f32)[:, None] * inv[None, :]
    cos, sin = np.cos(ang).astype(f32), np.sin(ang).astype(f32)
    cos_t = np.concatenate([cos, cos], axis=-1)
    sin_t = np.concatenate([-sin, sin], axis=-1)
    log_g = np.log1p(-(f32(2.0) ** (f32(-5.0) - np.arange(RET_HEADS, dtype=f32)))).astype(f32)
    i = np.arange(CHUNK, dtype=f32)
    diff = i[:, None] - i[None, :]
    idec = np.where(diff >= 0, np.exp(log_g[:, None, None] * np.maximum(diff, f32(0.0))), f32(0.0)).astype(f32)
    qdec = np.broadcast_to(np.exp(log_g[:, None] * (i + f32(1.0))).astype(f32)[:, :, None],
                           (RET_HEADS, CHUNK, RET_DK))
    kdec = np.broadcast_to(np.exp(log_g[:, None] * (f32(CHUNK - 1.0) - i)).astype(f32)[:, :, None],
                           (RET_HEADS, CHUNK, RET_DK))
    chunk_decay = tuple(float(np.exp(lg * f32(CHUNK))) for lg in log_g)

    rows = RET_CPS * CHUNK
    nst = seq // rows

    def ublk(c):
        return pl.BlockSpec((rows, COL_BLK), lambda b, n, c=c: (b * nst + n, c))

    def table():
        return pl.BlockSpec((RET_HEADS, CHUNK, RET_DK), lambda b, n: (0, 0, 0))

    def slab(w):
        n_steps = batch * nst
        assert w.shape[0] % (8 * n_steps) == 0
        return pl.BlockSpec((w.shape[0] // n_steps, w.shape[1]), lambda b, n: (b * nst + n, 0))

    weights = (w_conv_out, w_ret_out, w_o)
    return pl.pallas_call(
        functools.partial(_ret_kernel, chunk_decay=chunk_decay),
        grid=(batch, nst),
        in_specs=[ublk(Q_BLK), ublk(K_BLK), ublk(V_BLK), ublk(V_BLK + 1),
                  ublk(RG_BLK), ublk(RG_BLK + 1),
                  pl.BlockSpec((rows, RET_DK), lambda b, n: (n, 0)),
                  pl.BlockSpec((rows, RET_DK), lambda b, n: (n, 0)),
                  table(), table(), table()] + [slab(w) for w in weights],
        out_specs=[pl.BlockSpec((rows, RET_HEADS * RET_DV), lambda b, n: (b * nst + n, 0))]
        + [slab(w) for w in weights],
        out_shape=[jax.ShapeDtypeStruct((t, RET_HEADS * RET_DV), BF16)]
        + [jax.ShapeDtypeStruct(w.shape, BF16) for w in weights],
        scratch_shapes=[pltpu.VMEM((RET_HEADS, RET_DK, RET_DV), F32)],
        compiler_params=_cparams(("arbitrary", "arbitrary")),
        name="retention",
    )(u, u, u, u, u, u, *(jnp.asarray(tab) for tab in (cos_t, sin_t, idec, qdec, kdec)), *weights)


HALO = 8


MIX_TN = 512


def _mix_kernel(cb_ref, cc_ref, cx_ref, hcc_ref, hcx_ref, z_ref, ga0_ref, ga1_ref, gb0_ref, gb1_ref,
                cw_ref, wc_ref, wr_ref, wo_ref, x_ref, mod_ref, modp_ref, g2_ref, wrt_ref, brt_ref,
                o_ref, hp_ref, ef_ref, rw_ref, m_ref, mg_ref, x1p_ref, *, tiles_per_seq, n_tiles):
    tm = cb_ref.shape[0]
    d = o_ref.shape[1]
    step = pl.program_id(0)
    slot = step % 2
    prev = 1 - slot
    nchunk = d // MIX_TN

    @pl.when(step == 0)
    def _():
        x1p_ref[1] = jnp.zeros(x1p_ref.shape[1:], F32)

    def body(with_tile):
        if with_tile:
            m = cc_ref[...].astype(F32) * cx_ref[...].astype(F32)
            halo = hcc_ref[...].astype(F32) * hcx_ref[...].astype(F32)
            first = (step % tiles_per_seq) == 0
            m_ref[0:HALO, :] = jnp.where(first, 0.0, halo)
            m_ref[HALO:HALO + tm, :] = m
            conv = (m_ref[HALO - 2:HALO - 2 + tm, :] * cw_ref[0:1, :]
                    + m_ref[HALO - 1:HALO - 1 + tm, :] * cw_ref[1:2, :]
                    + m * cw_ref[2:3, :])
            p = (cb_ref[...].astype(F32) * conv).astype(BF16)
            z = z_ref[...]
        ssq = jnp.zeros((tm, 1), F32)
        for n in range(nchunk):
            cols = slice(n * MIX_TN, (n + 1) * MIX_TN)
            if with_tile:
                ga_ref, gb_ref = (ga0_ref, gb0_ref) if n < COL_BLK // MIX_TN else (ga1_ref, gb1_ref)
                gcols = slice((n * MIX_TN) % COL_BLK, (n * MIX_TN) % COL_BLK + MIX_TN)
                ya = jnp.dot(p, wc_ref[:, cols], preferred_element_type=F32)
                yb = jnp.dot(z, wr_ref[:, cols], preferred_element_type=F32)
                merged = (_sigmoid(ga_ref[:, gcols].astype(F32)) * ya
                          + _sigmoid(gb_ref[:, gcols].astype(F32)) * yb)
                mg_ref[:, cols] = merged.astype(BF16)
            xp = x1p_ref[prev, :, cols]
            ssq = ssq + jnp.sum(xp * xp, axis=-1, keepdims=True)
        rnorm = lax.rsqrt(ssq * (1.0 / d) + EPS)
        gain = g2_ref[...] * (1.0 + modp_ref[0, 4:5, :])
        shift = modp_ref[0, 3:4, :]
        wh, wl = _split_bf16(wrt_ref[...])
        whl = jnp.concatenate([wh, wl], axis=1)
        acc_hw = jnp.zeros((tm, 2 * LANES), F32)
        acc_lw = jnp.zeros((tm, LANES), F32)
        if with_tile:
            mg = mg_ref[...]
        for n in range(nchunk):
            cols = slice(n * MIX_TN, (n + 1) * MIX_TN)
            if with_tile:
                acc = jnp.dot(mg, wo_ref[:, cols], preferred_element_type=F32)
                x1 = x_ref[:, cols] + mod_ref[0, 2:3, cols] * acc
                o_ref[:, cols] = x1
                x1p_ref[slot, :, cols] = x1
            if n % 2 == 0:
                c = (n // 2) * MIX_TN
                halves = []
                for c0 in (c, c + HALF):
                    hcols = slice(c0, c0 + MIX_TN)
                    h2 = x1p_ref[prev, :, hcols] * rnorm * gain[:, hcols] + shift[:, hcols]
                    hh, hl = _split_bf16(h2)
                    acc_hw = acc_hw + jnp.dot(hh, whl[hcols, :], preferred_element_type=F32)
                    acc_lw = acc_lw + jnp.dot(hl, wh[hcols, :], preferred_element_type=F32)
                    halves.append(h2)
                hp_ref[:, c:c + MIX_TN] = _pack_pair(halves[0], halves[1])
        logits = (acc_hw[:, :LANES] + (acc_hw[:, LANES:] + acc_lw)) + brt_ref[...]
        ef_ref[...], rw_ref[...] = _route_from_logits(logits)

    pl.when(step < n_tiles)(functools.partial(body, True))
    pl.when(step == n_tiles)(functools.partial(body, False))


def _mix(u, z, conv_w, wc_bf, wr_bf, wo_bf, x2, mod3, g2, w_rt, b_rt, seq):
    t, d = x2.shape
    tm = 256
    nt = t // tm
    hb = tm // HALO
    per_b = seq // tm

    def cur(i):
        return jnp.minimum(i, nt - 1)

    def prev(i):
        return jnp.maximum(i - 1, 0)

    def ublk(c):
        return pl.BlockSpec((tm, COL_BLK), lambda i, c=c: (cur(i), c))

    def halo(c):
        return pl.BlockSpec((HALO, COL_BLK), lambda i, c=c: (jnp.maximum(cur(i) * hb - 1, 0), c))

    def resident(shape):
        return pl.BlockSpec(shape, lambda i: (0, 0), pipeline_mode=pl.Buffered(1))

    return pl.pallas_call(
        functools.partial(_mix_kernel, tiles_per_seq=per_b, n_tiles=nt),
        grid=(nt + 1,),
        in_specs=[ublk(CB_BLK), ublk(CC_BLK), ublk(CX_BLK), halo(CC_BLK), halo(CX_BLK),
                  pl.BlockSpec((tm, z.shape[1]), lambda i: (cur(i), 0)),
                  ublk(GA_BLK), ublk(GA_BLK + 1), ublk(GB_BLK), ublk(GB_BLK + 1),
                  pl.BlockSpec((CONV_K, CONV_CH), lambda i: (0, 0)),
                  resident(wc_bf.shape), resident(wr_bf.shape), resident(wo_bf.shape),
                  pl.BlockSpec((tm, d), lambda i: (cur(i), 0)),
                  pl.BlockSpec((1, 6, d), lambda i: (cur(i) // per_b, 0, 0)),
                  pl.BlockSpec((1, 6, d), lambda i: (prev(i) // per_b, 0, 0)),
                  pl.BlockSpec((1, d), lambda i: (0, 0)),
                  pl.BlockSpec((d, LANES), lambda i: (0, 0)),
                  pl.BlockSpec((1, LANES), lambda i: (0, 0))],
        out_specs=[pl.BlockSpec((tm, d), lambda i: (cur(i), 0)),
                   pl.BlockSpec((tm, HALF), lambda i: (prev(i), 0)),
                   pl.BlockSpec((tm, LANES), lambda i: (prev(i), 0)),
                   pl.BlockSpec((tm, LANES), lambda i: (prev(i), 0))],
        out_shape=[jax.ShapeDtypeStruct((t, d), F32),
                   jax.ShapeDtypeStruct((t, HALF), U32),
                   jax.ShapeDtypeStruct((t, LANES), F32),
                   jax.ShapeDtypeStruct((t, LANES), F32)],
        scratch_shapes=[pltpu.VMEM((tm + HALO, CONV_CH), F32),
                        pltpu.VMEM((tm, d), BF16),
                        pltpu.VMEM((2, tm, d), F32)],
        compiler_params=_cparams(("arbitrary",)),
        name="mix",
    )(u, u, u, u, u, z, u, u, u, u, conv_w, wc_bf, wr_bf, wo_bf, x2, mod3, mod3, g2, w_rt, b_rt)


ROUTE_E0 = N_GROUPS


U32 = jnp.uint32
HALF = D_MODEL // 2


def _pack_pair(lo, hi):
    ulo = lax.bitcast_convert_type(lo.astype(BF16).astype(F32), U32)
    uhi = lax.bitcast_convert_type(hi.astype(BF16).astype(F32), U32)
    return (ulo >> 16) | uhi


def _unpack_pair(w):
    lo = lax.bitcast_convert_type(w << 16, F32)
    hi = lax.bitcast_convert_type(w & jnp.uint32(0xFFFF0000), F32)
    return lo, hi


ROW_SUB = HALF // LANES


def _row_slab(r):
    if isinstance(r, int):
        return pl.ds(r * ROW_SUB, ROW_SUB)
    return pl.ds(pl.multiple_of(r * ROW_SUB, ROW_SUB), ROW_SUB)


def _store_rows(ref, row0, v):
    for c in range(ROW_SUB):
        ref[pl.ds(row0 * ROW_SUB + c, v.shape[0], stride=ROW_SUB), :] = v[:, c * LANES:(c + 1) * LANES]


def _load_rows(ref, row0, n):
    return jnp.concatenate(
        [ref[pl.ds(row0 * ROW_SUB + c, n, stride=ROW_SUB), :] for c in range(ROW_SUB)], axis=1)


def _route_from_logits(logits):
    lane = lax.broadcasted_iota(I32, logits.shape, 1).astype(F32)
    neg = -jnp.inf
    big = float(LANES)
    gl = jnp.where(lane < N_GROUPS, logits, neg)
    gmax = jnp.max(gl, axis=-1, keepdims=True)
    gsel = jnp.min(jnp.where(gl == gmax, lane, big), axis=-1, keepdims=True)
    gsum = jnp.sum(jnp.exp(gl - gmax), axis=-1, keepdims=True)
    g_w = 1.0 / gsum
    lo_lane = ROUTE_E0 + EXPERTS_PER_GROUP * gsel
    emask = (lane >= lo_lane) & (lane < lo_lane + EXPERTS_PER_GROUP)
    el = jnp.where(emask, logits, neg)
    m1 = jnp.max(el, axis=-1, keepdims=True)
    i1 = jnp.min(jnp.where(el == m1, lane, big), axis=-1, keepdims=True)
    el2 = jnp.where(lane == i1, neg, el)
    m2 = jnp.max(el2, axis=-1, keepdims=True)
    i2 = jnp.min(jnp.where(el2 == m2, lane, big), axis=-1, keepdims=True)
    esum = jnp.sum(jnp.exp(el - m1), axis=-1, keepdims=True)
    p1 = 1.0 / esum
    p2 = jnp.exp(m2 - m1) / esum
    tot = p1 + p2
    c1 = g_w * (p1 / tot)
    c2 = g_w * (p2 / tot)
    ef = jnp.where(lane == 0.0, i1 - ROUTE_E0, jnp.where(lane == 1.0, i2 - ROUTE_E0, 0.0))
    cw = jnp.where(lane == 0.0, c1, jnp.where(lane == 1.0, c2, 0.0))
    return ef, cw


META_TB = 512
DEST_ROWS = 8
BLK_ROWS = 256


def _meta_kernel(ef_ref, dest_ref, blk_ref, pref_ref):
    t = ef_ref.shape[0]
    nb = t // META_TB
    lane = lax.broadcasted_iota(I32, (META_TB, LANES), 1).astype(F32)
    r_i = lax.broadcasted_iota(I32, (META_TB, META_TB), 0)
    c_i = lax.broadcasted_iota(I32, (META_TB, META_TB), 1)
    tril = jnp.where(r_i > c_i, 1.0, 0.0).astype(BF16)

    def onehots(i):
        ef = ef_ref[pl.ds(i * META_TB, META_TB), :]
        oh1 = jnp.where(lane == ef[:, 0:1], 1.0, 0.0)
        oh2 = jnp.where(lane == ef[:, 1:2], 1.0, 0.0)
        return oh1, oh2

    def pass1(i, carry):
        oh1, oh2 = onehots(i)
        cnt = oh1 + oh2
        pref = jnp.dot(tril, cnt.astype(BF16), preferred_element_type=F32) + carry
        pref_ref[pl.ds(i * META_TB, META_TB), :] = pref
        return carry + jnp.sum(cnt, axis=0, keepdims=True)

    counts = lax.fori_loop(0, nb, pass1, jnp.zeros((1, LANES), F32))
    nblk = jnp.floor((counts + (EXPERT_BLOCK - 1)) * (1.0 / EXPERT_BLOCK))
    u_r = lax.broadcasted_iota(I32, (LANES, LANES), 0)
    u_c = lax.broadcasted_iota(I32, (LANES, LANES), 1)
    upper = jnp.where(u_r <= u_c, 1.0, 0.0).astype(BF16)
    nblk8 = jnp.broadcast_to(nblk, (8, LANES))
    pend_blk = jnp.dot(nblk8.astype(BF16), upper, preferred_element_type=F32)[0:1, :]
    pstart_blk = pend_blk - nblk
    pstart = pstart_blk * float(EXPERT_BLOCK)

    for i in range(nb):
        oh1, oh2 = onehots(i)
        base = pref_ref[pl.ds(i * META_TB, META_TB), :] + pstart
        d1 = jnp.sum(oh1 * base, axis=-1, keepdims=True)
        d2 = jnp.sum(oh2 * base, axis=-1, keepdims=True)
        dest = jnp.where(lane == 0.0, d1, jnp.where(lane == 1.0, d2, 0.0))
        dest_ref[:, i * META_TB:(i + 1) * META_TB] = dest.T[:DEST_ROWS, :].astype(I32)

    blane = lax.broadcasted_iota(I32, (BLK_ROWS, LANES), 1).astype(F32)
    brow = lax.broadcasted_iota(I32, (BLK_ROWS, LANES), 0).astype(F32)
    emask = blane < N_EXPERTS
    owner = jnp.sum(jnp.where(emask & (pend_blk <= brow), 1.0, 0.0), axis=-1, keepdims=True)
    owner = jnp.minimum(owner, float(N_EXPERTS - 1))
    total = jnp.sum(jnp.where(emask, nblk, 0.0), axis=-1, keepdims=True)
    used = jnp.where(brow < total, 1.0, 0.0)
    mine = emask & (blane == owner)
    cnt_b = jnp.sum(jnp.where(mine, counts, 0.0), axis=-1, keepdims=True)
    ps_b = jnp.sum(jnp.where(mine, pstart_blk, 0.0), axis=-1, keepdims=True)
    nvalid = jnp.clip(cnt_b - (brow - ps_b) * float(EXPERT_BLOCK), 0.0, float(EXPERT_BLOCK))
    pad_lo = nvalid * used
    blk_ref[...] = jnp.where(blane == 0.0, owner,
                             jnp.where(blane == 1.0, used,
                                       jnp.where(blane == 2.0, pad_lo, 0.0))).astype(I32)


def _meta(ef):
    t = ef.shape[0]
    return pl.pallas_call(
        _meta_kernel,
        out_shape=[jax.ShapeDtypeStruct((DEST_ROWS, t), I32),
                   jax.ShapeDtypeStruct((BLK_ROWS, LANES), I32)],
        scratch_shapes=[pltpu.VMEM((t, LANES), F32)],
        compiler_params=pltpu.CompilerParams(vmem_limit_bytes=VMEM_LIMIT),
        name="meta",
    )(ef)


DISP_TM = 1024
TOP_K = 2


def _dispatch_kernel(pad_lo_ref, dest_ref, h_ref, xbuf_ref, stage, zrow, sems, zsem):
    i = pl.program_id(0)
    n = pl.num_programs(0)
    tm = h_ref.shape[0]
    n_blk = pad_lo_ref.shape[0]
    slot = i % 2

    def pad_rows(action):
        def blk(b, _):
            lo = pad_lo_ref[b]
            base = pl.multiple_of(b * EXPERT_BLOCK, EXPERT_BLOCK)

            @pl.when(lo == 0)
            def _():
                whole = pl.ds(pl.multiple_of(base * ROW_SUB, EXPERT_BLOCK * ROW_SUB),
                              EXPERT_BLOCK * ROW_SUB)
                action(pltpu.make_async_copy(zrow, xbuf_ref.at[whole], zsem))

            @pl.when(lo > 0)
            def _():
                def row(r, _):
                    action(pltpu.make_async_copy(zrow.at[_row_slab(0)],
                                                 xbuf_ref.at[_row_slab(base + r)], zsem))
                    return 0
                lax.fori_loop(lo, EXPERT_BLOCK, row, 0)
            return 0
        lax.fori_loop(0, n_blk, blk, 0)

    @pl.when(i == 0)
    def _():
        zrow[...] = jnp.zeros_like(zrow)
        pad_rows(lambda cp: cp.start())

    _store_rows(stage.at[slot], 0, h_ref[...])
    for j in range(tm):
        for k in range(TOP_K):
            d = dest_ref[k, 0, 0, j]
            pltpu.make_async_copy(stage.at[slot, _row_slab(j)], xbuf_ref.at[_row_slab(d)],
                                  sems.at[slot]).start(priority=k)

    def wait_tile(s):
        for _ in range(TOP_K):
            pltpu.make_async_copy(stage.at[s], xbuf_ref.at[pl.ds(0, tm * ROW_SUB)], sems.at[s]).wait()

    @pl.when(i > 0)
    def _():
        wait_tile(1 - slot)

    @pl.when(i == n - 1)
    def _():
        wait_tile(slot)
        pad_rows(lambda cp: cp.wait())


def _dispatch(pad_lo, dest3, hp, n_rows):
    t, w = hp.shape
    tm = DISP_TM
    grid_spec = pltpu.PrefetchScalarGridSpec(
        num_scalar_prefetch=1,
        grid=(t // tm,),
        in_specs=[pl.BlockSpec((TOP_K, 1, 1, tm), lambda i, *_: (0, i, 0, 0), memory_space=pltpu.SMEM),
                  pl.BlockSpec((tm, w), lambda i, *_: (i, 0))],
        out_specs=pl.BlockSpec(memory_space=pl.ANY),
        scratch_shapes=[pltpu.VMEM((2, tm * ROW_SUB, LANES), U32),
                        pltpu.VMEM((EXPERT_BLOCK * ROW_SUB, LANES), U32),
                        pltpu.SemaphoreType.DMA((2,)), pltpu.SemaphoreType.DMA(())],
    )
    return pl.pallas_call(
        _dispatch_kernel,
        grid_spec=grid_spec,
        out_shape=jax.ShapeDtypeStruct((n_rows * ROW_SUB, LANES), U32),
        compiler_params=_cparams(("arbitrary",)),
        name="dispatch",
    )(pad_lo, dest3, hp)


MOE_SUB = 4
W_RING = 2
CAST_ROWS = 256
SWITCH_AFTER = 4


def _moe_kernel(run_e_ref, run_id_ref, used_ref, nruns_ref, x_ref, wg_hbm, wu_hbm, wd_hbm, o_ref,
                wg_st, wu_st, wd_st, wg_bf, wu_bf, wd_bf, sems, state):
    step = pl.program_id(0)
    n_runs = nruns_ref[0]
    streams = ((wg_hbm, wg_st, wg_bf), (wu_hbm, wu_st, wu_bf), (wd_hbm, wd_st, wd_bf))

    def weight_copy(m, run):
        hbm, stage, _ = streams[m]
        return pltpu.make_async_copy(hbm.at[run_e_ref[run]], stage, sems.at[m])

    @pl.when(step == 0)
    def _():
        state[0] = 0
        state[1] = 0
        for m in range(len(streams)):
            weight_copy(m, 0).start(priority=1)

    def switch_to(run, b):
        ring = lax.rem(run, W_RING)
        for m, (_, stage, bf) in enumerate(streams):
            weight_copy(m, run).wait()

            def body(i, _, stage=stage, bf=bf):
                rows = pl.ds(pl.multiple_of(i * CAST_ROWS, CAST_ROWS), CAST_ROWS)
                bf[ring, rows, :] = stage[rows, :].astype(BF16)
                return 0
            lax.fori_loop(0, stage.shape[0] // CAST_ROWS, body, 0)

            @pl.when(run + 1 < n_runs)
            def _(m=m):
                weight_copy(m, run + 1).start(priority=1)
        state[0] = run + 1
        state[1] = b + SWITCH_AFTER

    def run_block(b, row0):
        run = run_id_ref[b]
        pl.when(state[0] == run)(functools.partial(switch_to, run, b))
        pl.when((state[0] == run + 1) & (run + 1 < n_runs) & (b >= state[1]))(
            functools.partial(switch_to, run + 1, b))

        ring = lax.rem(run, W_RING)
        xlo, xhi = _unpack_pair(_load_rows(x_ref, row0, EXPERT_BLOCK))
        xb = jnp.concatenate([xlo.astype(BF16), xhi.astype(BF16)], axis=1)
        g = jnp.dot(xb, wg_bf[ring], preferred_element_type=F32)
        u = jnp.dot(xb, wu_bf[ring], preferred_element_type=F32)
        hmid = (_silu(g) * u).astype(BF16)
        y = jnp.dot(hmid, wd_bf[ring], preferred_element_type=F32)
        _store_rows(o_ref, row0, _pack_pair(y[:, :HALF], y[:, HALF:]))

    for j in range(MOE_SUB):
        b = step * MOE_SUB + j
        row0 = j * EXPERT_BLOCK
        pl.when(used_ref[b] == 1)(functools.partial(run_block, b, row0))

        @pl.when(used_ref[b] == 0)
        def _():
            o_ref[pl.ds(row0 * ROW_SUB, EXPERT_BLOCK * ROW_SUB), :] = jnp.zeros(
                (EXPERT_BLOCK * ROW_SUB, LANES), U32)


def _moe(run_e, run_id, used, n_runs, xbuf, w_gate, w_up, w_down):
    r = xbuf.shape[0]
    rows = MOE_SUB * EXPERT_BLOCK * ROW_SUB
    d, de = w_gate.shape[1:]

    def xmap(s, run_e_ref, run_id_ref, used_ref, nruns_ref):
        del run_e_ref, run_id_ref, nruns_ref
        return (jnp.where(used_ref[s * MOE_SUB] == 1, s, 0), 0)

    grid_spec = pltpu.PrefetchScalarGridSpec(
        num_scalar_prefetch=4,
        grid=(r // rows,),
        in_specs=[pl.BlockSpec((rows, LANES), xmap),
                  pl.BlockSpec(memory_space=pl.ANY),
                  pl.BlockSpec(memory_space=pl.ANY),
                  pl.BlockSpec(memory_space=pl.ANY)],
        out_specs=pl.BlockSpec((rows, LANES), lambda s, *_: (s, 0)),
        scratch_shapes=[pltpu.VMEM((d, de), F32), pltpu.VMEM((d, de), F32), pltpu.VMEM((de, d), F32),
                        pltpu.VMEM((W_RING, d, de), BF16), pltpu.VMEM((W_RING, d, de), BF16),
                        pltpu.VMEM((W_RING, de, d), BF16),
                        pltpu.SemaphoreType.DMA((3,)), pltpu.SMEM((2,), I32)],
    )
    return pl.pallas_call(
        _moe_kernel,
        grid_spec=grid_spec,
        out_shape=jax.ShapeDtypeStruct((r, LANES), U32),
        compiler_params=_cparams(("arbitrary",), vmem=VMEM_LIMIT_MOE),
        name="moe",
    )(run_e, run_id, used, n_runs, xbuf, w_gate, w_up, w_down)


FIN_TM = 256


FIN_GROUP = 32
FIN_AHEAD = 2


def _final_kernel(dest_ref, dnext_ref, dahead_ref, x_ref, cw_ref, mod_ref, g_ref, y_hbm, o_ref,
                  y0, y1, y2, sems):
    i = pl.program_id(0)
    n = pl.num_programs(0)
    tm = x_ref.shape[0]
    bufs = (y0, y1, y2)

    def gather(d_ref, buf, sem, rows):
        for j in rows:
            for k in range(TOP_K):
                d = d_ref[k, 0, 0, j]
                pltpu.make_async_copy(y_hbm.at[_row_slab(d)], buf.at[k, _row_slab(j)],
                                      sem).start(priority=k)

    def wait_tile(buf, sem):
        for k in range(TOP_K):
            pltpu.make_async_copy(y_hbm.at[pl.ds(0, tm * ROW_SUB)], buf.at[k], sem).wait()

    @pl.when(i == 0)
    def _():
        gather(dest_ref, y0, sems.at[0], range(tm))
        gather(dnext_ref, y1, sems.at[1], range(tm))

    def step(p):
        cur, sem_cur = bufs[p], sems.at[p]
        tgt, sem_tgt = bufs[(p + FIN_AHEAD) % 3], sems.at[(p + FIN_AHEAD) % 3]
        wait_tile(cur, sem_cur)
        g2 = mod_ref[0, 5:6, :]
        for r0 in range(0, tm, FIN_GROUP):
            gather(dahead_ref, tgt, sem_tgt, range(r0, r0 + FIN_GROUP))
            rows = slice(r0, r0 + FIN_GROUP)
            cw = cw_ref[rows, :]
            c0, c1 = cw[:, 0:1], cw[:, 1:2]
            lo0, hi0 = _unpack_pair(_load_rows(cur.at[0], r0, FIN_GROUP))
            lo1, hi1 = _unpack_pair(_load_rows(cur.at[1], r0, FIN_GROUP))
            xlo = x_ref[rows, :HALF] + g2[:, :HALF] * (lo0 * c0 + lo1 * c1)
            xhi = x_ref[rows, HALF:] + g2[:, HALF:] * (hi0 * c0 + hi1 * c1)
            ms = (jnp.sum(xlo * xlo, axis=-1, keepdims=True)
                  + jnp.sum(xhi * xhi, axis=-1, keepdims=True)) * (1.0 / D_MODEL)
            r = lax.rsqrt(ms + EPS)
            o_ref[rows, :HALF] = xlo * r * g_ref[:, :HALF]
            o_ref[rows, HALF:] = xhi * r * g_ref[:, HALF:]

        @pl.when(i == n - 1)
        def _():
            wait_tile(bufs[(p + 1) % 3], sems.at[(p + 1) % 3])
            wait_tile(tgt, sem_tgt)

    for p in range(3):
        pl.when(i % 3 == p)(functools.partial(step, p))


def _final(dest3, x1, cw, mod3, g, ybuf, seq):
    t, d = x1.shape
    tm = FIN_TM
    nt = t // tm
    per_b = seq // tm

    def dspec(shift):
        return pl.BlockSpec((TOP_K, 1, 1, tm), lambda i: (0, jnp.minimum(i + shift, nt - 1), 0, 0),
                            memory_space=pltpu.SMEM)

    return pl.pallas_call(
        _final_kernel,
        grid=(nt,),
        in_specs=[dspec(0), dspec(1), dspec(FIN_AHEAD),
                  pl.BlockSpec((tm, d), lambda i: (i, 0)),
                  pl.BlockSpec((tm, LANES), lambda i: (i, 0)),
                  pl.BlockSpec((1, 6, d), lambda i: (i // per_b, 0, 0)),
                  pl.BlockSpec((1, d), lambda i: (0, 0)),
                  pl.BlockSpec(memory_space=pl.ANY)],
        out_specs=pl.BlockSpec((tm, d), lambda i: (i, 0)),
        out_shape=jax.ShapeDtypeStruct((t, d), F32),
        scratch_shapes=[pltpu.VMEM((TOP_K, tm * ROW_SUB, LANES), U32)] * (FIN_AHEAD + 1)
        + [pltpu.SemaphoreType.DMA((FIN_AHEAD + 1,))],
        compiler_params=_cparams(("arbitrary",)),
        name="final",
    )(dest3, dest3, dest3, x1, cw, mod3, g, ybuf)


def kernel(x, c, w_ada, b_ada, norm1_g, w_in, conv_w, w_conv_out, w_ret_out, w_o, norm2_g,
           w_router_group, b_router_group, w_router_expert, b_router_expert,
           w_gate, w_up, w_down, norm_f_g):
    batch, seq, d = x.shape
    t = batch * seq
    depth = w_ada.shape[0]
    assert d == D_MODEL and w_in.shape[2] == IN_COLS and w_gate.shape[1] == N_EXPERTS
    assert depth == 1, "the final kernel fuses the last rmsnorm into the single layer"
    n_rows = t * 2 + N_EXPERTS * EXPERT_BLOCK
    n_blk = n_rows // EXPERT_BLOCK
    assert n_blk <= BLK_ROWS

    c_pad = jnp.pad(c, ((0, 8 - batch), (0, 0)))
    xs = x.reshape(t, d)
    for l in range(depth):
        mod = _ada(c_pad, w_ada[l], b_ada[l].reshape(1, -1))[:batch]
        mod3 = mod.reshape(batch, 6, d)

        u = _in_proj(xs, mod3, norm1_g[l].reshape(1, d), w_in[l], seq)
        z, wc_bf, wr_bf, wo_bf = _retention(u, w_conv_out[l], w_ret_out[l], w_o[l], batch, seq)
        w_r = jnp.pad(jnp.concatenate([w_router_group[l], w_router_expert[l]], axis=1),
                      ((0, 0), (0, LANES - N_GROUPS - N_EXPERTS)))
        b_r = jnp.pad(jnp.concatenate([b_router_group[l], b_router_expert[l]]),
                      (0, LANES - N_GROUPS - N_EXPERTS)).reshape(1, LANES)
        x1, h2, ef, cw = _mix(u, z, conv_w[l], wc_bf, wr_bf, wo_bf, xs, mod3,
                              norm2_g[l].reshape(1, d), w_r, b_r, seq)

        dest, blk = _meta(ef)
        blk_e = blk[:n_blk, 0]
        used = blk[:n_blk, 1]
        prev_e = jnp.concatenate([jnp.full((1,), -1, I32), blk_e[:-1]])
        first = ((blk_e != prev_e) & (used == 1)).astype(I32)
        run_id = jnp.maximum(jnp.cumsum(first) - 1, 0).astype(I32)
        n_runs = jnp.sum(first).astype(I32).reshape(1)
        present = jnp.any((blk_e[None, :] == jnp.arange(N_EXPERTS, dtype=I32)[:, None])
                          & (used[None, :] == 1), axis=1)
        eids = jnp.arange(N_EXPERTS, dtype=I32)
        rank = jnp.cumsum(present.astype(I32)) - 1
        run_e = jnp.sum(jnp.where(present[None, :] & (rank[None, :] == eids[:, None]),
                                  eids[None, :], 0), axis=1).astype(I32)

        dest2 = dest[:TOP_K]
        xbuf = _dispatch(blk[:n_blk, 2], dest2.reshape(TOP_K, t // DISP_TM, 1, DISP_TM), h2, n_rows)
        ybuf = _moe(run_e, run_id, used, n_runs, xbuf, w_gate[l], w_up[l], w_down[l])
        xs = _final(dest2.reshape(TOP_K, t // FIN_TM, 1, FIN_TM), x1, cw, mod3,
                    norm_f_g.reshape(1, d), ybuf, seq)
    return xs.reshape(batch, seq, d)
```
